```python
import math
import jax
import jax.numpy as jnp
from jax import lax
import numpy as np

D_MODEL = 1024
BATCH = 32
SEQ = 2048
DEPTH = 2
DEC_BATCH = 8
DEC_SEQ = 2048
PAST_LEN = 128

EPS = 1e-6
A_HEADS = 6
A_HEAD_DIM = 64
A_WIDTH = A_HEADS * A_HEAD_DIM
CONV_K = 5
A_CHUNK = 64
B_HEADS = 6
B_Q_RANK = 256
B_KV_RANK = 128
B_NOPE = 64
B_ROPE = 32
B_V = 64
B_WIDTH = B_HEADS * B_V
ROPE_THETA = 10000.0
Q_BLOCK = 128
C_HEADS = 4
C_EXPAND = 128
C_HEAD_DIM = 64
C_WIDTH = C_HEADS * C_HEAD_DIM
C_FDIM = C_HEADS * C_EXPAND
C_CHUNK = 64
D_MIX = A_WIDTH + B_WIDTH + C_WIDTH
A_IN = 4 * A_WIDTH + 4 * A_HEADS
B_IN = B_Q_RANK + B_KV_RANK + B_ROPE
C_IN = 3 * C_FDIM + 2 * C_WIDTH
D_IN = A_IN + B_IN + C_IN
N_GROUPS = 4
EXPERTS_PER_GROUP = 8
N_EXPERTS = N_GROUPS * EXPERTS_PER_GROUP
TOP_K = 2
D_EXPERT = 512
MOE_BLOCK = 256

kernel_name = 'hybrid_parallel_heads_encoder'


def rmsnorm(x, w):
    xf = x.astype(jnp.float32)
    y = xf * lax.rsqrt(jnp.mean(xf * xf, axis=-1, keepdims=True) + EPS)
    return (y * w.astype(jnp.float32)).astype(x.dtype)


def l2norm(x):
    return x * lax.rsqrt(jnp.sum(x * x, axis=-1, keepdims=True) + EPS)


def centred_short_conv(x, w):
    L = x.shape[1]
    pad = CONV_K // 2
    xp = jnp.pad(x, ((0, 0), (pad, pad), (0, 0)))
    y = xp[:, 0:L] * w[0]
    for t in range(1, CONV_K):
        y = y + xp[:, t:t + L] * w[t]
    return jax.nn.silu(y)


def to_chunks(t, c):
    b, L, h = t.shape[:3]
    t = t.reshape((b, L // c, c, h) + t.shape[3:])
    return jnp.moveaxis(t, 3, 1)


def from_chunks(o):
    n, b, h, c, d = o.shape
    return jnp.transpose(o, (1, 0, 3, 2, 4)).reshape(b, n * c, h, d)


def gated_delta_chunked(q, k, v, g, beta):
    c = A_CHUNK
    dv = v.shape[-1]
    q = to_chunks(q * q.shape[-1] ** -0.5, c)
    k = to_chunks(k, c)
    v = to_chunks(v, c)
    g = jnp.cumsum(to_chunks(g, c), axis=-1)
    beta = to_chunks(beta, c)
    tril = jnp.tril(jnp.ones((c, c), dtype=bool))
    strict = jnp.tril(jnp.ones((c, c), dtype=bool), -1)
    decay = jnp.where(tril, jnp.exp(jnp.minimum(g[..., :, None] - g[..., None, :], 0.0)), 0.0)
    k_beta = k * beta[..., None]
    kk = jnp.einsum('bhncd,bhnsd->bhncs', k_beta, k) * decay
    lower = jnp.where(strict, kk, 0.0) + jnp.eye(c, dtype=kk.dtype)
    rhs = jnp.concatenate([v * beta[..., None], k_beta * jnp.exp(g)[..., None]], axis=-1)
    sol = lax.linalg.triangular_solve(lower, rhs, left_side=True, lower=True, unit_diagonal=True)
    u, w = sol[..., :dv], sol[..., dv:]
    qk = jnp.einsum('bhncd,bhnsd->bhncs', q, k) * decay
    q_dec = q * jnp.exp(g)[..., None]
    k_dec = k * jnp.exp(g[..., -1:] - g)[..., None]
    g_last = jnp.exp(g[..., -1])

    def step(S, inp):
        qk_c, qd_c, kd_c, u_c, w_c, gl_c = inp
        v_new = u_c - jnp.einsum('bhcd,bhde->bhce', w_c, S)
        o = jnp.einsum('bhcd,bhde->bhce', qd_c, S) + jnp.einsum('bhcs,bhse->bhce', qk_c, v_new)
        S = S * gl_c[..., None, None] + jnp.einsum('bhcd,bhce->bhde', kd_c, v_new)
        return S, o

    xs = tuple(jnp.moveaxis(t, 2, 0) for t in (qk, q_dec, k_dec, u, w, g_last))
    b, h = q.shape[:2]
    S0 = jnp.zeros((b, h, q.shape[-1], dv), jnp.float32)
    _, o = lax.scan(step, S0, xs)
    return from_chunks(o)


def hgrn2_chunked(q, k, v, logf):
    c = C_CHUNK
    q, k, v = (to_chunks(t, c) for t in (q, k, v))
    b = jnp.cumsum(to_chunks(logf, c), axis=-2)
    q_dec = q * jnp.exp(b)
    k_dec = k * jnp.exp(b[..., -1:, :] - b)
    f_last = jnp.exp(b[..., -1, :])
    tril = jnp.tril(jnp.ones((c, c), dtype=bool))[..., None]

    def step(S, inp):
        q_c, k_c, v_c, b_c, qd_c, kd_c, fl_c = inp
        decay = jnp.where(tril, jnp.exp(jnp.minimum(b_c[:, :, :, None, :] - b_c[:, :, None, :, :], 0.0)), 0.0)
        attn = jnp.einsum('bhtd,bhsd,bhtsd->bhts', q_c, k_c, decay)
        o = jnp.einsum('bhts,bhse->bhte', attn, v_c) + jnp.einsum('bhtd,bhde->bhte', qd_c, S)
        S = S * fl_c[..., None] + jnp.einsum('bhsd,bhse->bhde', kd_c, v_c)
        return S, o

    xs = tuple(jnp.moveaxis(t, 2, 0) for t in (q, k, v, b, q_dec, k_dec, f_last))
    bsz, h = q.shape[:2]
    S0 = jnp.zeros((bsz, h, C_EXPAND, C_HEAD_DIM), jnp.float32)
    _, o = lax.scan(step, S0, xs)
    return from_chunks(o)


def flip(t):
    return jnp.flip(t, axis=1)


def mixer_gdn(a_in, conv_w, a_log, dt_bias, norm_w):
    bsz, L, _ = a_in.shape
    qkv = centred_short_conv(a_in[..., :3 * A_WIDTH], conv_w).astype(jnp.float32)
    q = l2norm(qkv[..., :A_WIDTH].reshape(bsz, L, A_HEADS, A_HEAD_DIM))
    k = l2norm(qkv[..., A_WIDTH:2 * A_WIDTH].reshape(bsz, L, A_HEADS, A_HEAD_DIM))
    v = qkv[..., 2 * A_WIDTH:].reshape(bsz, L, A_HEADS, A_HEAD_DIM)
    gate = a_in[..., 3 * A_WIDTH:4 * A_WIDTH].astype(jnp.float32).reshape(bsz, L, A_HEADS, A_HEAD_DIM)
    gates = a_in[..., 4 * A_WIDTH:].astype(jnp.float32).reshape(bsz, L, 4, A_HEADS)
    beta = jax.nn.sigmoid(gates[:, :, 0:2])
    g = -jnp.exp(a_log.astype(jnp.float32)) * jax.nn.softplus(gates[:, :, 2:4] + dt_bias.astype(jnp.float32))
    o_f = gated_delta_chunked(q, k, v, g[:, :, 0], beta[:, :, 0])
    o_b = flip(gated_delta_chunked(flip(q), flip(k), flip(v), flip(g[:, :, 1]), flip(beta[:, :, 1])))
    o = rmsnorm(o_f + o_b, norm_w) * jax.nn.silu(gate)
    return o.reshape(bsz, L, A_WIDTH).astype(a_in.dtype)


def rope_tables(L):
    inv = ROPE_THETA ** (-jnp.arange(0, B_ROPE, 2, dtype=jnp.float32) / B_ROPE)
    ang = jnp.arange(L, dtype=jnp.float32)[:, None] * inv[None, :]
    return jnp.cos(ang), jnp.sin(ang)


def apply_rope(x, cos, sin):
    half = x.shape[-1] // 2
    x1, x2 = x[..., :half], x[..., half:]
    cos = cos.astype(x.dtype)
    sin = sin.astype(x.dtype)
    return jnp.concatenate([x1 * cos - x2 * sin, x2 * cos + x1 * sin], axis=-1)


def mixer_mla(b_in, q_a_norm, w_qb, kv_a_norm, w_kvb, qn_nope, qn_rope, kn_nope, kn_rope):
    bsz, L, _ = b_in.shape
    cq = rmsnorm(b_in[..., :B_Q_RANK], q_a_norm)
    ckv = rmsnorm(b_in[..., B_Q_RANK:B_Q_RANK + B_KV_RANK], kv_a_norm)
    k_rope = b_in[..., B_Q_RANK + B_KV_RANK:]
    q = (cq @ w_qb).reshape(bsz, L, B_HEADS, B_NOPE + B_ROPE)
    kv = (ckv @ w_kvb).reshape(bsz, L, B_HEADS, B_NOPE + B_V)
    q_nope = rmsnorm(q[..., :B_NOPE], qn_nope)
    q_rope = rmsnorm(q[..., B_NOPE:], qn_rope)
    k_nope = rmsnorm(kv[..., :B_NOPE], kn_nope)
    v = kv[..., B_NOPE:]
    k_rope = rmsnorm(k_rope, kn_rope)
    cos, sin = rope_tables(L)
    q_rope = apply_rope(q_rope, cos[:, None, :], sin[:, None, :])
    k_rope = apply_rope(k_rope, cos, sin)
    nq = L // Q_BLOCK
    q_nope_b = jnp.moveaxis(q_nope.reshape(bsz, nq, Q_BLOCK, B_HEADS, B_NOPE), 1, 0)
    q_rope_b = jnp.moveaxis(q_rope.reshape(bsz, nq, Q_BLOCK, B_HEADS, B_ROPE), 1, 0)
    scale = (B_NOPE + B_ROPE) ** -0.5

    def attend(blk):
        qn, qr = blk
        s = jnp.einsum('bqhd,bkhd->bhqk', qn, k_nope) + jnp.einsum('bqhr,bkr->bhqk', qr, k_rope)
        p = jax.nn.softmax(s.astype(jnp.float32) * scale, axis=-1).astype(v.dtype)
        return jnp.einsum('bhqk,bkhd->bqhd', p, v)

    o = lax.map(attend, (q_nope_b, q_rope_b))
    return jnp.moveaxis(o, 0, 1).reshape(bsz, L, B_WIDTH).astype(b_in.dtype)


def mixer_hgrn2(c_in, lb, norm_w):
    bsz, L, _ = c_in.shape
    cf = c_in.astype(jnp.float32)
    q = jax.nn.silu(cf[..., :C_FDIM]).reshape(bsz, L, C_HEADS, C_EXPAND)
    f_pre = cf[..., C_FDIM:3 * C_FDIM].reshape(bsz, L, 2, C_FDIM)
    i_in = cf[..., 3 * C_FDIM:3 * C_FDIM + C_WIDTH].reshape(bsz, L, C_HEADS, C_HEAD_DIM)
    gate = cf[..., 3 * C_FDIM + C_WIDTH:].reshape(bsz, L, C_HEADS, C_HEAD_DIM)
    logf = jnp.logaddexp(jnp.log(lb), jnp.log1p(-lb) + jax.nn.log_sigmoid(f_pre))
    k = (1.0 - lb) * jax.nn.sigmoid(-f_pre)
    logf = logf.reshape(bsz, L, 2, C_HEADS, C_EXPAND)
    k = k.reshape(bsz, L, 2, C_HEADS, C_EXPAND)
    o_f = hgrn2_chunked(q, k[:, :, 0], i_in, logf[:, :, 0])
    o_b = flip(hgrn2_chunked(flip(q), flip(k[:, :, 1]), flip(i_in), flip(logf[:, :, 1])))
    o = rmsnorm(o_f + o_b, norm_w) * jax.nn.silu(gate)
    return o.reshape(bsz, L, C_WIDTH).astype(c_in.dtype)


def hier_moe(x, w_group, b_group, w_router, b_router, w1, w3, w2):
    T, D = x.shape
    gp = jax.nn.softmax((x @ w_group).astype(jnp.float32) + b_group.astype(jnp.float32), axis=-1)
    g_idx = jnp.argmax(gp, axis=-1)
    g_w = jnp.take_along_axis(gp, g_idx[:, None], axis=-1)[:, 0]
    el = ((x @ w_router).astype(jnp.float32) + b_router.astype(jnp.float32)).reshape(T, N_GROUPS, EXPERTS_PER_GROUP)
    el_g = jnp.take_along_axis(el, g_idx[:, None, None], axis=1)[:, 0]
    top_w, top_i = lax.top_k(jax.nn.softmax(el_g, axis=-1), TOP_K)
    top_w = top_w / jnp.sum(top_w, axis=-1, keepdims=True)
    weights = g_w[:, None] * top_w
    expert = g_idx[:, None] * EXPERTS_PER_GROUP + top_i
    A = T * TOP_K
    e_flat = expert.reshape(A).astype(jnp.int32)
    w_flat = weights.reshape(A)
    tok_flat = jnp.repeat(jnp.arange(T, dtype=jnp.int32), TOP_K)
    order = jnp.argsort(e_flat)
    e_s, tok_s, w_s = e_flat[order], tok_flat[order], w_flat[order]
    counts = jax.ops.segment_sum(jnp.ones((A,), jnp.int32), e_flat, num_segments=N_EXPERTS)
    starts = jnp.cumsum(counts) - counts
    padded = (counts + MOE_BLOCK - 1) // MOE_BLOCK * MOE_BLOCK
    pends = jnp.cumsum(padded)
    pstarts = pends - padded
    dest = pstarts[e_s] + (jnp.arange(A, dtype=jnp.int32) - starts[e_s])
    P = A + N_EXPERTS * MOE_BLOCK
    nblk = P // MOE_BLOCK
    buf_tok = jnp.zeros((P,), jnp.int32).at[dest].set(tok_s)
    buf_w = jnp.zeros((P,), jnp.float32).at[dest].set(w_s)
    blk_e = jnp.minimum(jnp.searchsorted(pends, jnp.arange(nblk, dtype=jnp.int32) * MOE_BLOCK, side='right'), N_EXPERTS - 1)

    def run_block(args):
        tok, wgt, e = args
        xb = x[tok]
        h = jax.nn.silu(xb @ w1[e]) * (xb @ w3[e])
        return (h @ w2[e]) * wgt[:, None].astype(x.dtype)

    y = lax.map(run_block, (buf_tok.reshape(nblk, MOE_BLOCK), buf_w.reshape(nblk, MOE_BLOCK), blk_e))
    return jax.ops.segment_sum(y.reshape(P, D), buf_tok, num_segments=T)


def setup_inputs(seed: int = 0) -> dict:
    key = jax.random.key(seed)
    ks = jax.random.split(key, 32)

    def nrm(k, shape, scale):
        return jax.random.normal(k, shape, jnp.float32) * scale

    def gain(k, shape):
        return 1.0 + 0.01 * jax.random.normal(k, shape, jnp.float32)

    dt = jnp.exp(jax.random.uniform(ks[6], (DEPTH, 2, A_HEADS), jnp.float32, math.log(1e-3), math.log(1e-1)))
    return {
        'x_prompt': nrm(ks[0], (BATCH, SEQ, D_MODEL), 1.0),
        'x_sample': nrm(ks[1], (DEC_BATCH, DEC_SEQ, D_MODEL), 1.0),
        'ln1': gain(ks[2], (DEPTH, D_MODEL)),
        'w_in': nrm(ks[3], (DEPTH, D_MODEL, D_IN), D_MODEL ** -0.5),
        'conv_w': nrm(ks[4], (DEPTH, CONV_K, 3 * A_WIDTH), CONV_K ** -0.5),
        'a_log': jnp.log(jax.random.uniform(ks[5], (DEPTH, 2, A_HEADS), jnp.float32, 1.0, 16.0)),
        'dt_bias': dt + jnp.log(-jnp.expm1(-dt)),
        'a_norm': gain(ks[7], (DEPTH, A_HEAD_DIM)),
        'q_a_norm': gain(ks[8], (DEPTH, B_Q_RANK)),
        'w_qb': nrm(ks[9], (DEPTH, B_Q_RANK, B_HEADS * (B_NOPE + B_ROPE)), B_Q_RANK ** -0.5),
        'kv_a_norm': gain(ks[10], (DEPTH, B_KV_RANK)),
        'w_kvb': nrm(ks[11], (DEPTH, B_KV_RANK, B_HEADS * (B_NOPE + B_V)), B_KV_RANK ** -0.5),
        'qn_nope': gain(ks[12], (DEPTH, B_NOPE)),
        'qn_rope': gain(ks[13], (DEPTH, B_ROPE)),
        'kn_nope': gain(ks[14], (DEPTH, B_NOPE)),
        'kn_rope': gain(ks[15], (DEPTH, B_ROPE)),
        'c_lower_bounds': nrm(ks[16], (DEPTH, 2, C_FDIM), 0.5),
        'c_norm': gain(ks[17], (DEPTH, C_HEAD_DIM)),
        'w_out': nrm(ks[18], (DEPTH, D_MIX, D_MODEL), D_MIX ** -0.5),
        'ln2': gain(ks[19], (DEPTH, D_MODEL)),
        'w_group': nrm(ks[20], (DEPTH, D_MODEL, N_GROUPS), D_MODEL ** -0.5),
        'b_group': nrm(ks[21], (DEPTH, N_GROUPS), 0.01),
        'w_router': nrm(ks[22], (DEPTH, D_MODEL, N_EXPERTS), D_MODEL ** -0.5),
        'b_router': nrm(ks[23], (DEPTH, N_EXPERTS), 0.01),
        'w1': nrm(ks[24], (DEPTH, N_EXPERTS, D_MODEL, D_EXPERT), D_MODEL ** -0.5),
        'w3': nrm(ks[25], (DEPTH, N_EXPERTS, D_MODEL, D_EXPERT), D_MODEL ** -0.5),
        'w2': nrm(ks[26], (DEPTH, N_EXPERTS, D_EXPERT, D_MODEL), D_EXPERT ** -0.5),
    }


def reference(x_prompt, x_sample, ln1, w_in, conv_w, a_log, dt_bias, a_norm, q_a_norm, w_qb, kv_a_norm, w_kvb,
              qn_nope, qn_rope, kn_nope, kn_rope, c_lower_bounds, c_norm, w_out, ln2, w_group, b_group,
              w_router, b_router, w1, w3, w2):
    lbs = jnp.cumsum(jax.nn.softmax(c_lower_bounds.astype(jnp.float32), axis=0), axis=0)
    lbs = lbs - lbs[0:1]

    def trunk(x):
        bsz, L, D = x.shape
        for l in range(DEPTH):
            h = rmsnorm(x, ln1[l])
            proj = h @ w_in[l]
            o_a = mixer_gdn(proj[..., :A_IN], conv_w[l], a_log[l], dt_bias[l], a_norm[l])
            o_b = mixer_mla(proj[..., A_IN:A_IN + B_IN], q_a_norm[l], w_qb[l], kv_a_norm[l], w_kvb[l],
                            qn_nope[l], qn_rope[l], kn_nope[l], kn_rope[l])
            o_c = mixer_hgrn2(proj[..., A_IN + B_IN:], lbs[l], c_norm[l])
            o = jnp.concatenate([o_a, o_b, o_c], axis=-1)
            x = x + o @ w_out[l]
            h = rmsnorm(x, ln2[l])
            y = hier_moe(h.reshape(bsz * L, D), w_group[l], b_group[l], w_router[l], b_router[l], w1[l], w3[l], w2[l])
            x = x + y.reshape(bsz, L, D)
        return x

    y_prompt = trunk(x_prompt)
    y_sample = trunk(x_sample)
    return (y_prompt, y_sample)
```

```python
import functools
import math

import numpy as np
import jax
import jax.numpy as jnp
from jax import lax
from jax.experimental import pallas as pl
from jax.experimental.pallas import tpu as pltpu

F32 = jnp.float32
BF16 = jnp.bfloat16

D_MODEL = 1024
DEPTH = 2
EPS = 1e-6
A_HEADS, A_HEAD_DIM, CONV_K = 6, 64, 5
A_WIDTH = A_HEADS * A_HEAD_DIM
B_HEADS, B_Q_RANK, B_KV_RANK, B_NOPE, B_ROPE, B_V = 6, 256, 128, 64, 32, 64
B_WIDTH = B_HEADS * B_V
ROPE_THETA = 10000.0
C_HEADS, C_EXPAND, C_HEAD_DIM = 4, 128, 64
C_WIDTH = C_HEADS * C_HEAD_DIM
C_FDIM = C_HEADS * C_EXPAND
D_MIX = A_WIDTH + B_WIDTH + C_WIDTH
A_IN = 4 * A_WIDTH + 4 * A_HEADS
B_IN = B_Q_RANK + B_KV_RANK + B_ROPE
C_IN = 3 * C_FDIM + 2 * C_WIDTH
N_GROUPS, EXPERTS_PER_GROUP, TOP_K, D_EXPERT = 4, 8, 2, 512
N_EXPERTS = N_GROUPS * EXPERTS_PER_GROUP

LANE = 128
CHUNK = 64
A_PAD = 4 * A_WIDTH + LANE
B_PAD = 512
C_PAD = C_IN
N_PAD = A_PAD + B_PAD + C_PAD
HEAD_PAD = LANE
MOE_ROWS = 256
TOK_TILE = 256
VMEM_LIMIT = 56 * 1024 * 1024


def _cparams(sem):
    return pltpu.CompilerParams(dimension_semantics=sem, vmem_limit_bytes=VMEM_LIMIT)


def _dot(a, b):
    return jnp.dot(a.astype(BF16), b.astype(BF16), preferred_element_type=F32)


def _dot_nt(a, b):
    return lax.dot_general(a.astype(BF16), b.astype(BF16), (((1,), (1,)), ((), ())), preferred_element_type=F32)


def _dot_tn(a, b):
    return lax.dot_general(a.astype(BF16), b.astype(BF16), (((0,), (0,)), ((), ())), preferred_element_type=F32)


def _split(x):
    hi = x.astype(BF16)
    lo = (x - hi.astype(F32)).astype(BF16)
    return hi, lo


def _sel_dot(sel, x):
    hi, lo = _split(x)
    return jnp.dot(sel, hi, preferred_element_type=F32) + jnp.dot(sel, lo, preferred_element_type=F32)


def _dot_sel(x, sel):
    hi, lo = _split(x)
    return jnp.dot(hi, sel, preferred_element_type=F32) + jnp.dot(lo, sel, preferred_element_type=F32)


def _sigmoid(x):
    return 1.0 / (1.0 + jnp.exp(-x))


def _silu(x):
    return x * _sigmoid(x)


def _softplus(x):
    return jnp.maximum(x, 0.0) + jnp.log(1.0 + jnp.exp(-jnp.abs(x)))


def _rms(x, w):
    return x * lax.rsqrt(jnp.mean(x * x, axis=-1, keepdims=True) + EPS) * w


def _full(shape):
    return pl.BlockSpec(shape, lambda *_: (0,) * len(shape))


def _proj_kernel(x_ref, ln_ref, w_ref, a_ref, b_ref, c_ref):
    h = _rms(x_ref[...], ln_ref[...]).astype(BF16)
    a_ref[...] = jnp.dot(h, w_ref[:, 0:A_PAD], preferred_element_type=F32)
    b_ref[...] = jnp.dot(h, w_ref[:, A_PAD:A_PAD + B_PAD], preferred_element_type=F32)
    c_ref[...] = jnp.dot(h, w_ref[:, A_PAD + B_PAD:N_PAD], preferred_element_type=F32)


def _proj(x, ln, w_pad):
    T = x.shape[0]
    tm = TOK_TILE
    return pl.pallas_call(
        _proj_kernel,
        grid=(T // tm,),
        in_specs=[pl.BlockSpec((tm, D_MODEL), lambda i: (i, 0)), _full((1, D_MODEL)), _full((D_MODEL, N_PAD))],
        out_specs=[pl.BlockSpec((tm, A_PAD), lambda i: (i, 0)), pl.BlockSpec((tm, B_PAD), lambda i: (i, 0)),
                   pl.BlockSpec((tm, C_PAD), lambda i: (i, 0))],
        out_shape=[jax.ShapeDtypeStruct((T, A_PAD), F32), jax.ShapeDtypeStruct((T, B_PAD), F32),
                   jax.ShapeDtypeStruct((T, C_PAD), F32)],
        compiler_params=_cparams(("parallel",)),
        name="proj",
    )(x, ln, w_pad)


def _gdn_kernel(qkv_ref, gate_ref, gates_ref, convw_ref, alog_ref, dtb_ref, anorm_ref, ind_ref, expb_ref, expg_ref,
                out_ref, q_s, k_s, v_s, o_s, s_s, *, L):
    nc = L // CHUNK
    W = A_WIDTH
    row = lax.broadcasted_iota(jnp.int32, (CHUNK, CHUNK), 0)
    col = lax.broadcasted_iota(jnp.int32, (CHUNK, CHUNK), 1)
    lane = lax.broadcasted_iota(jnp.int32, (CHUNK, LANE), 1)
    ind = ind_ref[...]

    def conv_chunk(c, carry):
        t0 = pl.multiple_of(c * CHUNK, CHUNK)
        tp = pl.multiple_of(jnp.maximum(t0 - 8, 0), 8)
        tn = pl.multiple_of(jnp.minimum(t0 + CHUNK, L - 8), 8)
        for g in range(3):
            cs = slice(g * W, (g + 1) * W)
            main = qkv_ref[pl.ds(t0, CHUNK), cs]
            prev = jnp.where(c > 0, qkv_ref[pl.ds(tp, 8), cs], 0.0)
            nxt = jnp.where(c < nc - 1, qkv_ref[pl.ds(tn, 8), cs], 0.0)
            win = jnp.concatenate([prev, main, nxt], axis=0)
            acc = win * convw_ref[2:3, cs]
            for j in (0, 1, 3, 4):
                acc = acc + pltpu.roll(win, (2 - j) % (CHUNK + 16), 0) * convw_ref[j:j + 1, cs]
            y = _silu(acc[8:8 + CHUNK])
            if g < 2:
                y = y * lax.rsqrt(_dot_sel(y * y, ind) + EPS)
            if g == 0:
                q_s[pl.ds(t0, CHUNK), :] = (y * (A_HEAD_DIM ** -0.5)).astype(BF16)
            elif g == 1:
                k_s[pl.ds(t0, CHUNK), :] = y.astype(BF16)
            else:
                v_s[pl.ds(t0, CHUNK), :] = y
        return carry

    lax.fori_loop(0, nc, conv_chunk, 0)

    coef = -jnp.exp(alog_ref[...])
    dtb = dtb_ref[...]
    g_lane = (lane >= 2 * A_HEADS) & (lane < 4 * A_HEADS)

    def scan_chunk(c, d):
        t0 = pl.multiple_of(c * CHUNK, CHUNK)
        qc = q_s[pl.ds(t0, CHUNK), :].astype(F32)
        kc = k_s[pl.ds(t0, CHUNK), :].astype(F32)
        vc = v_s[pl.ds(t0, CHUNK), :]
        gt = gates_ref[pl.ds(t0, CHUNK), :]
        beta = _sigmoid(gt)
        gl = jnp.where(g_lane, coef * _softplus(gt + dtb), 0.0)
        incl = (col <= row) if d == 0 else (col >= row)
        strict = (col < row) if d == 0 else (col > row)
        gcum = _sel_dot(incl.astype(BF16), gl)
        total = gcum[CHUNK - 1:CHUNK] if d == 0 else gcum[0:1]
        e_in = jnp.exp(gcum)
        e_out = jnp.exp(total - gcum)
        g_last = jnp.broadcast_to(jnp.exp(total), (8, LANE))
        expb = expb_ref[d]
        expg = expg_ref[d]
        beta_w = jnp.dot(beta.astype(BF16), expb, preferred_element_type=F32)
        ein_w = jnp.dot(e_in.astype(BF16), expg, preferred_element_type=F32)
        eout_w = jnp.dot(e_out.astype(BF16), expg, preferred_element_type=F32)
        glast_w = _dot_sel(g_last, expg)
        gcol_w = _dot_sel(gcum, expg)
        kb = kc * beta_w
        vb = vc * beta_w
        kbe = kb * ein_w
        qe = qc * ein_w
        kd = kc * eout_w
        outs = []
        for h in range(A_HEADS):
            hs = slice(h * A_HEAD_DIM, (h + 1) * A_HEAD_DIM)
            lg = 2 * A_HEADS + A_HEADS * d + h
            pick = (lane == lg).astype(BF16)
            ghi, glo = _split(gcum)
            grow = (lax.dot_general(pick, ghi, (((1,), (1,)), ((), ())), preferred_element_type=F32)
                    + lax.dot_general(pick, glo, (((1,), (1,)), ((), ())), preferred_element_type=F32))
            dec = jnp.exp(jnp.minimum(gcol_w[:, hs] - grow, 0.0))
            k_h = kc[:, hs]
            n = jnp.where(strict, _dot_nt(kb[:, hs], k_h) * dec, 0.0)
            qk = jnp.where(incl, _dot_nt(qc[:, hs], k_h) * dec, 0.0)
            x = jnp.concatenate([vb[:, hs], kbe[:, hs]], axis=1)
            x = x - _dot(n, x)
            p = n
            for _ in range(5):
                p = _dot(p, p)
                x = x + _dot(p, x)
            u = x[:, :A_HEAD_DIM]
            w = x[:, A_HEAD_DIM:]
            s = s_s[d * A_HEADS + h]
            v_new = u - _dot(w, s)
            outs.append(_dot(qe[:, hs], s) + _dot(qk, v_new))
            s_s[d * A_HEADS + h] = s * glast_w[0:1, hs] + _dot_tn(kd[:, hs], v_new)
        o = jnp.concatenate(outs, axis=1)
        if d == 0:
            o_s[pl.ds(t0, CHUNK), :] = o
        else:
            o_s[pl.ds(t0, CHUNK), :] = o_s[pl.ds(t0, CHUNK), :] + o

    s_s[...] = jnp.zeros_like(s_s)

    def fwd(i, carry):
        scan_chunk(i, 0)
        return carry

    def bwd(i, carry):
        scan_chunk(nc - 1 - i, 1)
        return carry

    lax.fori_loop(0, nc, fwd, 0)
    lax.fori_loop(0, nc, bwd, 0)

    rows = 256

    def finish(i, carry):
        t0 = pl.multiple_of(i * rows, rows)
        o = o_s[pl.ds(t0, rows), :]
        ms = _dot_sel(o * o, ind) * (1.0 / A_HEAD_DIM)
        o = o * lax.rsqrt(ms + EPS) * anorm_ref[...]
        out_ref[pl.ds(t0, rows), :] = o * _silu(gate_ref[pl.ds(t0, rows), :])
        return carry

    lax.fori_loop(0, L // rows, finish, 0)


def _gdn(a_in, convw, alog, dtb, anorm, ind, expb, expg):
    B, L, _ = a_in.shape
    W = A_WIDTH
    kern = functools.partial(_gdn_kernel, L=L)
    one = pl.Buffered(1)
    return pl.pallas_call(
        kern,
        grid=(B,),
        in_specs=[pl.BlockSpec((None, L, 3 * W), lambda b: (b, 0, 0), pipeline_mode=one),
                  pl.BlockSpec((None, L, W), lambda b: (b, 0, 3), pipeline_mode=one),
                  pl.BlockSpec((None, L, LANE), lambda b: (b, 0, 4 * W // LANE), pipeline_mode=one),
                  _full((8, 3 * W)), _full((1, LANE)), _full((1, LANE)), _full((1, W)), _full((W, W)),
                  _full((2, LANE, W)), _full((2, LANE, W))],
        out_specs=pl.BlockSpec((None, L, W), lambda b: (b, 0, 0)),
        out_shape=jax.ShapeDtypeStruct((B, L, W), F32),
        scratch_shapes=[pltpu.VMEM((L, W), BF16), pltpu.VMEM((L, W), BF16), pltpu.VMEM((L, W), F32),
                        pltpu.VMEM((L, W), F32), pltpu.VMEM((2 * A_HEADS, A_HEAD_DIM, A_HEAD_DIM), F32)],
        compiler_params=_cparams(("parallel",)),
        name="gdn",
    )(a_in, a_in, a_in, convw, alog, dtb, anorm, ind, expb, expg)


def _mla_prep_kernel(b_ref, tab_ref, qan_ref, kvan_ref, wqb_ref, wk_ref, wv_ref, qw_ref, kw_ref, knr_ref,
                     indq_ref, invn_ref, place_ref, q_out, k_out, v_out):
    b = b_ref[...]
    tab = tab_ref[...]
    cq = _rms(b[:, :B_Q_RANK], qan_ref[...])
    ckv = _rms(b[:, B_Q_RANK:B_Q_RANK + B_KV_RANK], kvan_ref[...])
    kr = b[:, B_Q_RANK + B_KV_RANK:]
    indq = indq_ref[...]
    invn = invn_ref[...]

    q = _dot(cq, wqb_ref[...])
    qn = q * lax.rsqrt(_dot_sel(q * q, indq) * invn + EPS) * qw_ref[...]
    scale = (B_NOPE + B_ROPE) ** -0.5
    pieces = []
    for h in range(B_HEADS):
        x = qn[:, h * HEAD_PAD:(h + 1) * HEAD_PAD]
        half = B_ROPE // 2
        r = (x * tab[:, 0:LANE] + pltpu.roll(x, LANE - half, 1) * tab[:, LANE:2 * LANE]
             + pltpu.roll(x, half, 1) * tab[:, 2 * LANE:3 * LANE])
        pieces.append(r * scale)
    q_out[...] = jnp.concatenate(pieces, axis=1).astype(BF16)

    kk = _dot(ckv, wk_ref[...])
    kn = kk * lax.rsqrt(_dot_sel(kk * kk, indq) * invn + EPS) * kw_ref[...]
    krn = kr * lax.rsqrt(jnp.sum(kr * kr, axis=-1, keepdims=True) * (1.0 / B_ROPE) + EPS) * knr_ref[...]
    half = B_ROPE // 2
    krr = (krn * tab[:, 3 * LANE:4 * LANE] + pltpu.roll(krn, LANE - half, 1) * tab[:, 4 * LANE:5 * LANE]
           + pltpu.roll(krn, half, 1) * tab[:, 5 * LANE:6 * LANE])
    k_out[...] = (kn + jnp.dot(krr.astype(BF16), place_ref[...], preferred_element_type=F32)).astype(BF16)
    v_out[...] = _dot(ckv, wv_ref[...]).astype(BF16)


def _mla_prep(b_in, tab, qan, kvan, wqb, wk, wv, qw, kw, knr, indq, invn, place):
    B, L, _ = b_in.shape
    tm = TOK_TILE
    HW = B_HEADS * HEAD_PAD
    return pl.pallas_call(
        _mla_prep_kernel,
        grid=(B, L // tm),
        in_specs=[pl.BlockSpec((None, tm, B_PAD), lambda b, j: (b, j, 0)),
                  pl.BlockSpec((tm, 6 * LANE), lambda b, j: (j, 0)),
                  _full((1, B_Q_RANK)), _full((1, B_KV_RANK)), _full((B_Q_RANK, HW)), _full((B_KV_RANK, HW)),
                  _full((B_KV_RANK, B_WIDTH)), _full((1, HW)), _full((1, HW)), _full((1, LANE)),
                  _full((HW, HW)), _full((1, HW)), _full((LANE, HW))],
        out_specs=[pl.BlockSpec((None, tm, HW), lambda b, j: (b, j, 0)),
                   pl.BlockSpec((None, tm, HW), lambda b, j: (b, j, 0)),
                   pl.BlockSpec((None, tm, B_WIDTH), lambda b, j: (b, j, 0))],
        out_shape=[jax.ShapeDtypeStruct((B, L, HW), BF16), jax.ShapeDtypeStruct((B, L, HW), BF16),
                   jax.ShapeDtypeStruct((B, L, B_WIDTH), BF16)],
        compiler_params=_cparams(("parallel", "parallel")),
        name="mla_prep",
    )(b_in, tab, qan, kvan, wqb, wk, wv, qw, kw, knr, indq, invn, place)


def _attn_kernel(q_ref, k_ref, v_ref, o_ref):
    v = v_ref[...]
    outs = []
    for hh in range(2):
        q = q_ref[:, hh * HEAD_PAD:(hh + 1) * HEAD_PAD]
        k = k_ref[:, hh * HEAD_PAD:(hh + 1) * HEAD_PAD]
        s = lax.dot_general(q, k, (((1,), (1,)), ((), ())), preferred_element_type=F32)
        p = jnp.exp(s - jnp.max(s, axis=-1, keepdims=True))
        den = jnp.sum(p, axis=-1, keepdims=True)
        outs.append(jnp.dot(p.astype(BF16), v, preferred_element_type=F32) / den)
    lane = lax.broadcasted_iota(jnp.int32, outs[0].shape, 1)
    o_ref[...] = jnp.where(lane < B_V, outs[0], outs[1])


def _attn(q, k, v):
    B, L, _ = q.shape
    tq = 256
    return pl.pallas_call(
        _attn_kernel,
        grid=(B, B_HEADS // 2, L // tq),
        in_specs=[pl.BlockSpec((None, tq, 2 * HEAD_PAD), lambda b, p, j: (b, j, p)),
                  pl.BlockSpec((None, L, 2 * HEAD_PAD), lambda b, p, j: (b, 0, p)),
                  pl.BlockSpec((None, L, 2 * B_V), lambda b, p, j: (b, 0, p))],
        out_specs=pl.BlockSpec((None, tq, 2 * B_V), lambda b, p, j: (b, j, p)),
        out_shape=jax.ShapeDtypeStruct((B, L, B_WIDTH), F32),
        compiler_params=_cparams(("parallel", "parallel", "arbitrary")),
        name="mla_attn",
    )(q, k, v)


def _hgrn_tables():
    C = CHUNK
    m_all = np.zeros((8 * C, C), np.float32)
    masks = np.zeros((7, C, C), np.float32)
    for lvl in range(6):
        m = 1 << lvl
        for t in range(C):
            bd = (t // (2 * m)) * 2 * m + m - 1
            if t % (2 * m) >= m:
                m_all[lvl * C + t, bd + 1:t + 1] = 1.0
            else:
                m_all[lvl * C + t, t + 1:bd + 1] = 1.0
            for s in range(C):
                if t // (2 * m) == s // (2 * m) and t % (2 * m) >= m and s % (2 * m) < m:
                    masks[lvl, t, s] = 1.0
    for t in range(C):
        m_all[6 * C + t, :t + 1] = 1.0
        m_all[7 * C + t, t + 1:] = 1.0
    masks[6] = np.eye(C, dtype=np.float32)
    m_f = m_all.reshape(8, C, C)
    m_b = m_f[:, ::-1, ::-1]
    mk_b = masks[:, ::-1, ::-1]
    return (np.stack([m_f.reshape(8 * C, C), m_b.reshape(8 * C, C)]), np.stack([masks, mk_b]))


def _hgrn_kernel(q_ref, ff_ref, fb_ref, i_ref, gate_ref, lb_ref, cnorm_ref, mall_ref, masks_ref, ind_ref,
                 out_ref, o_s, s_s, *, L):
    nc = L // CHUNK
    C = CHUNK

    def scan_chunk(c, d):
        t0 = pl.multiple_of(c * C, C)
        q = _silu(q_ref[pl.ds(t0, C), :])
        fp = (ff_ref if d == 0 else fb_ref)[pl.ds(t0, C), :]
        lb = lb_ref[d:d + 1, :]
        lsig = jnp.minimum(fp, 0.0) - jnp.log(1.0 + jnp.exp(-jnp.abs(fp)))
        a = jnp.log(lb)
        bb = jnp.log(1.0 - lb) + lsig
        logf = jnp.maximum(a, bb) + jnp.log(1.0 + jnp.exp(-jnp.abs(a - bb)))
        kk = (1.0 - lb) * _sigmoid(-fp)
        e_all = jnp.exp(_sel_dot(mall_ref[d], logf))
        v = i_ref[pl.ds(t0, C), :]
        outs = []
        for h in range(C_HEADS):
            ls = slice(h * C_EXPAND, (h + 1) * C_EXPAND)
            qh = q[:, ls]
            kh = kk[:, ls]
            vh = v[:, h * C_HEAD_DIM:(h + 1) * C_HEAD_DIM]
            attn = _dot_nt(qh, kh) * masks_ref[d, 6]
            for lvl in range(6):
                e = e_all[lvl * C:(lvl + 1) * C, ls]
                attn = attn + _dot_nt(qh * e, kh * e) * masks_ref[d, lvl]
            e_in = e_all[6 * C:7 * C, ls]
            e_out = e_all[7 * C:8 * C, ls]
            f_last = e_in[C - 1:C] if d == 0 else e_in[0:1]
            st = s_s[d * C_HEADS + h]
            outs.append(_dot(attn, vh) + _dot_nt(qh * e_in, st))
            s_s[d * C_HEADS + h] = st * f_last + _dot_tn(vh, kh * e_out)
        o = jnp.concatenate(outs, axis=1)
        if d == 0:
            o_s[pl.ds(t0, C), :] = o
        else:
            o_s[pl.ds(t0, C), :] = o_s[pl.ds(t0, C), :] + o

    s_s[...] = jnp.zeros_like(s_s)

    def fwd(i, carry):
        scan_chunk(i, 0)
        return carry

    def bwd(i, carry):
        scan_chunk(nc - 1 - i, 1)
        return carry

    lax.fori_loop(0, nc, fwd, 0)
    lax.fori_loop(0, nc, bwd, 0)

    rows = 256
    ind = ind_ref[...]

    def finish(i, carry):
        t0 = pl.multiple_of(i * rows, rows)
        o = o_s[pl.ds(t0, rows), :]
        ms = _dot_sel(o * o, ind) * (1.0 / C_HEAD_DIM)
        o = o * lax.rsqrt(ms + EPS) * cnorm_ref[...]
        out_ref[pl.ds(t0, rows), :] = o * _silu(gate_ref[pl.ds(t0, rows), :])
        return carry

    lax.fori_loop(0, L // rows, finish, 0)


def _hgrn(c_in, lb, cnorm, mall, masks, ind):
    B, L, _ = c_in.shape
    kern = functools.partial(_hgrn_kernel, L=L)
    one = pl.Buffered(1)
    F, W = C_FDIM, C_WIDTH
    return pl.pallas_call(
        kern,
        grid=(B,),
        in_specs=[pl.BlockSpec((None, L, F), lambda b: (b, 0, 0), pipeline_mode=one),
                  pl.BlockSpec((None, L, F), lambda b: (b, 0, 1), pipeline_mode=one),
                  pl.BlockSpec((None, L, F), lambda b: (b, 0, 2), pipeline_mode=one),
                  pl.BlockSpec((None, L, W), lambda b: (b, 0, 3 * F // W), pipeline_mode=one),
                  pl.BlockSpec((None, L, W), lambda b: (b, 0, 3 * F // W + 1), pipeline_mode=one),
                  _full((2, F)), _full((1, W)), _full((2, 8 * CHUNK, CHUNK)), _full((2, 7, CHUNK, CHUNK)),
                  _full((W, W))],
        out_specs=pl.BlockSpec((None, L, W), lambda b: (b, 0, 0)),
        out_shape=jax.ShapeDtypeStruct((B, L, W), F32),
        scratch_shapes=[pltpu.VMEM((L, W), F32), pltpu.VMEM((2 * C_HEADS, C_HEAD_DIM, C_EXPAND), F32)],
        compiler_params=_cparams(("parallel",)),
        name="hgrn2",
    )(c_in, c_in, c_in, c_in, c_in, lb, cnorm, mall, masks, ind)


def _outproj_kernel(x_ref, oa_ref, ob_ref, oc_ref, wa_ref, wb_ref, wc_ref, ln_ref, wr_ref, br_ref,
                    x1_ref, h_ref, route_ref):
    x1 = x_ref[...] + _dot(oa_ref[...], wa_ref[...]) + _dot(ob_ref[...], wb_ref[...]) + _dot(oc_ref[...], wc_ref[...])
    x1_ref[...] = x1
    h = _rms(x1, ln_ref[...])
    h_ref[...] = h
    hh, hl = _split(h)
    wr = wr_ref[...]
    wh, wl = _split(wr)
    lg = (jnp.dot(hh, wh, preferred_element_type=F32) + jnp.dot(hh, wl, preferred_element_type=F32)
          + jnp.dot(hl, wh, preferred_element_type=F32)) + br_ref[...]
    lane = lax.broadcasted_iota(jnp.int32, lg.shape, 1).astype(F32)
    neg = -1e30
    far = 1e9
    gmask = lane < N_GROUPS
    gl = jnp.where(gmask, lg, neg)
    gmax = jnp.max(gl, axis=-1, keepdims=True)
    g_idx = jnp.min(jnp.where(gl == gmax, lane, far), axis=-1, keepdims=True)
    g_w = 1.0 / jnp.sum(jnp.where(gmask, jnp.exp(gl - gmax), 0.0), axis=-1, keepdims=True)
    e_lane = lane - N_GROUPS
    emask = (e_lane >= g_idx * EXPERTS_PER_GROUP) & (e_lane < (g_idx + 1.0) * EXPERTS_PER_GROUP)
    el = jnp.where(emask, lg, neg)
    m1 = jnp.max(el, axis=-1, keepdims=True)
    i1 = jnp.min(jnp.where(el == m1, lane, far), axis=-1, keepdims=True)
    el2 = jnp.where(lane == i1, neg, el)
    m2 = jnp.max(el2, axis=-1, keepdims=True)
    i2 = jnp.min(jnp.where((el2 == m2) & emask & (lane != i1), lane, far), axis=-1, keepdims=True)
    r = jnp.exp(m2 - m1)
    w1 = g_w / (1.0 + r)
    w2 = g_w * r / (1.0 + r)
    route_ref[...] = jnp.where(lane == 0.0, i1 - N_GROUPS,
                               jnp.where(lane == 1.0, i2 - N_GROUPS,
                                         jnp.where(lane == 2.0, w1, jnp.where(lane == 3.0, w2, 0.0))))


def _outproj(x, oa, ob, oc, wa, wb, wc, ln, wr, br):
    T = x.shape[0]
    tm = TOK_TILE
    row = lambda w: pl.BlockSpec((tm, w), lambda i: (i, 0))
    return pl.pallas_call(
        _outproj_kernel,
        grid=(T // tm,),
        in_specs=[row(D_MODEL), row(A_WIDTH), row(B_WIDTH), row(C_WIDTH),
                  _full((A_WIDTH, D_MODEL)), _full((B_WIDTH, D_MODEL)), _full((C_WIDTH, D_MODEL)),
                  _full((1, D_MODEL)), _full((D_MODEL, LANE)), _full((1, LANE))],
        out_specs=[row(D_MODEL), row(D_MODEL), row(LANE)],
        out_shape=[jax.ShapeDtypeStruct((T, D_MODEL), F32), jax.ShapeDtypeStruct((T, D_MODEL), F32),
                   jax.ShapeDtypeStruct((T, LANE), F32)],
        compiler_params=_cparams(("parallel",)),
        name="outproj_router",
    )(x, oa, ob, oc, wa, wb, wc, ln, wr, br)


def _moe_kernel(blk_e_ref, nact_ref, ids_ref, h_hbm, w1_ref, w3_ref, w2_ref, y_ref, xb, sem):
    i = pl.program_id(0)

    def row_copy(r):
        t = ids_ref[0, 0, r]
        return pltpu.make_async_copy(h_hbm.at[pl.ds(t, 1)], xb.at[pl.ds(r, 1)], sem)

    @pl.when(i < nact_ref[0])
    def _():
        def issue(r, carry):
            row_copy(r).start()
            return carry

        lax.fori_loop(0, MOE_ROWS, issue, 0)

        def drain(r, carry):
            row_copy(r).wait()
            return carry

        lax.fori_loop(0, MOE_ROWS, drain, 0)
        x = xb[...].astype(BF16)
        a = jnp.dot(x, w1_ref[...], preferred_element_type=F32)
        g = jnp.dot(x, w3_ref[...], preferred_element_type=F32)
        y_ref[...] = jnp.dot((_silu(a) * g).astype(BF16), w2_ref[...], preferred_element_type=F32)

    @pl.when(i >= nact_ref[0])
    def _():
        y_ref[...] = jnp.zeros_like(y_ref)


def _moe(blk_e, nact, ids, h, w1, w3, w2):
    nblk = ids.shape[0]
    P = nblk * MOE_ROWS
    grid_spec = pltpu.PrefetchScalarGridSpec(
        num_scalar_prefetch=2,
        grid=(nblk,),
        in_specs=[pl.BlockSpec((1, 1, MOE_ROWS), lambda i, be, na: (i, 0, 0), memory_space=pltpu.SMEM),
                  pl.BlockSpec(memory_space=pl.ANY),
                  pl.BlockSpec((None, D_MODEL, D_EXPERT), lambda i, be, na: (be[i], 0, 0)),
                  pl.BlockSpec((None, D_MODEL, D_EXPERT), lambda i, be, na: (be[i], 0, 0)),
                  pl.BlockSpec((None, D_EXPERT, D_MODEL), lambda i, be, na: (be[i], 0, 0))],
        out_specs=pl.BlockSpec((MOE_ROWS, D_MODEL), lambda i, be, na: (i, 0)),
        scratch_shapes=[pltpu.VMEM((MOE_ROWS, D_MODEL), F32), pltpu.SemaphoreType.DMA(())],
    )
    return pl.pallas_call(
        _moe_kernel,
        grid_spec=grid_spec,
        out_shape=jax.ShapeDtypeStruct((P, D_MODEL), F32),
        compiler_params=_cparams(("arbitrary",)),
        name="moe_ffn",
    )(blk_e, nact, ids, h, w1, w3, w2)


def _combine_kernel(pos_ref, x_ref, route_ref, y_hbm, out_ref, yb, sem):
    tm = x_ref.shape[0]

    def row_copy(r):
        p = pos_ref[0, 0, r]
        return pltpu.make_async_copy(y_hbm.at[pl.ds(p, 1)], yb.at[pl.ds(r, 1)], sem)

    def issue(r, carry):
        row_copy(r).start()
        return carry

    lax.fori_loop(0, TOP_K * tm, issue, 0)

    def drain(r, carry):
        row_copy(r).wait()
        return carry

    lax.fori_loop(0, TOP_K * tm, drain, 0)
    route = route_ref[...]
    out_ref[...] = x_ref[...] + yb[0:tm, :] * route[:, 2:3] + yb[tm:2 * tm, :] * route[:, 3:4]


def _combine(pos, x1, route, y):
    T = x1.shape[0]
    tm = TOK_TILE
    return pl.pallas_call(
        _combine_kernel,
        grid=(T // tm,),
        in_specs=[pl.BlockSpec((1, 1, TOP_K * tm), lambda i: (i, 0, 0), memory_space=pltpu.SMEM),
                  pl.BlockSpec((tm, D_MODEL), lambda i: (i, 0)),
                  pl.BlockSpec((tm, LANE), lambda i: (i, 0)),
                  pl.BlockSpec(memory_space=pl.ANY)],
        out_specs=pl.BlockSpec((tm, D_MODEL), lambda i: (i, 0)),
        out_shape=jax.ShapeDtypeStruct((T, D_MODEL), F32),
        scratch_shapes=[pltpu.VMEM((TOP_K * tm, D_MODEL), F32), pltpu.SemaphoreType.DMA(())],
        compiler_params=_cparams(("arbitrary",)),
        name="moe_combine",
    )(pos, x1, route, y)


def _dispatch(route):
    T = route.shape[0]
    A = T * TOP_K
    e_flat = route[:, 0:TOP_K].astype(jnp.int32).reshape(A)
    tok_flat = jnp.repeat(jnp.arange(T, dtype=jnp.int32), TOP_K)
    order = jnp.argsort(e_flat)
    e_s, tok_s = e_flat[order], tok_flat[order]
    counts = jnp.zeros((N_EXPERTS,), jnp.int32).at[e_flat].add(1)
    starts = jnp.cumsum(counts) - counts
    padded = (counts + MOE_ROWS - 1) // MOE_ROWS * MOE_ROWS
    pends = jnp.cumsum(padded)
    pstarts = pends - padded
    dest = pstarts[e_s] + (jnp.arange(A, dtype=jnp.int32) - starts[e_s])
    P = A + N_EXPERTS * MOE_ROWS
    nblk = P // MOE_ROWS
    ids = jnp.zeros((P,), jnp.int32).at[dest].set(tok_s).reshape(nblk, 1, MOE_ROWS)
    blk_e = jnp.minimum(jnp.searchsorted(pends, jnp.arange(nblk, dtype=jnp.int32) * MOE_ROWS, side='right'),
                        N_EXPERTS - 1).astype(jnp.int32)
    nact = (pends[-1:] // MOE_ROWS).astype(jnp.int32)
    pos = jnp.zeros((A,), jnp.int32).at[order].set(dest).reshape(T // TOK_TILE, TOK_TILE, TOP_K)
    pos = jnp.transpose(pos, (0, 2, 1)).reshape(T // TOK_TILE, 1, TOP_K * TOK_TILE)
    return blk_e, nact, ids, pos


def _block_ind(width, seg):
    i = np.arange(width)
    return (i[:, None] // seg == i[None, :] // seg).astype(np.float32)


def _rope_tables(L):
    half = B_ROPE // 2
    inv = ROPE_THETA ** (-jnp.arange(0, B_ROPE, 2, dtype=F32) / B_ROPE)
    ang = jnp.arange(L, dtype=F32)[:, None] * inv[None, :]
    cos, sin = jnp.cos(ang), jnp.sin(ang)
    z = lambda n: jnp.zeros((L, n), F32)
    one = jnp.ones((L, B_NOPE), F32)
    rest = LANE - B_NOPE - B_ROPE
    cq = jnp.concatenate([one, cos, cos, z(rest)], axis=1)
    s1q = jnp.concatenate([z(B_NOPE), -sin, z(half), z(rest)], axis=1)
    s2q = jnp.concatenate([z(B_NOPE), z(half), sin, z(rest)], axis=1)
    ck = jnp.concatenate([cos, cos, z(LANE - B_ROPE)], axis=1)
    s1k = jnp.concatenate([-sin, z(LANE - half)], axis=1)
    s2k = jnp.concatenate([z(half), sin, z(LANE - B_ROPE)], axis=1)
    return jnp.concatenate([cq, s1q, s2q, ck, s1k, s2k], axis=1)


def _layer_params(l, ln1, w_in, conv_w, a_log, dt_bias, a_norm, q_a_norm, w_qb, kv_a_norm, w_kvb, qn_nope, qn_rope,
                  kn_nope, kn_rope, lbs, c_norm, w_out, ln2, w_group, b_group, w_router, b_router, w1, w3, w2):
    p = {}
    w = w_in[l]
    wp = jnp.zeros((D_MODEL, N_PAD), F32)
    wp = wp.at[:, 0:A_IN].set(w[:, 0:A_IN])
    wp = wp.at[:, A_PAD:A_PAD + B_IN].set(w[:, A_IN:A_IN + B_IN])
    wp = wp.at[:, A_PAD + B_PAD:].set(w[:, A_IN + B_IN:])
    p['w_in'] = wp.astype(BF16)
    p['ln1'] = ln1[l].reshape(1, D_MODEL)
    p['convw'] = jnp.zeros((8, 3 * A_WIDTH), F32).at[0:CONV_K].set(conv_w[l])
    gpad = lambda v: jnp.zeros((1, LANE), F32).at[0, 2 * A_HEADS:4 * A_HEADS].set(v.reshape(-1))
    p['alog'] = gpad(a_log[l])
    p['dtb'] = gpad(dt_bias[l])
    p['anorm'] = jnp.tile(a_norm[l], A_HEADS).reshape(1, A_WIDTH)
    half = B_ROPE // 2
    wq = w_qb[l].reshape(B_Q_RANK, B_HEADS, B_NOPE + B_ROPE)
    wq = jnp.pad(wq, ((0, 0), (0, 0), (0, HEAD_PAD - B_NOPE - B_ROPE)))
    p['wqb'] = wq.reshape(B_Q_RANK, B_HEADS * HEAD_PAD).astype(BF16)
    wkv = w_kvb[l].reshape(B_KV_RANK, B_HEADS, B_NOPE + B_V)
    wk = jnp.pad(wkv[:, :, :B_NOPE], ((0, 0), (0, 0), (0, HEAD_PAD - B_NOPE)))
    p['wk'] = wk.reshape(B_KV_RANK, B_HEADS * HEAD_PAD).astype(BF16)
    p['wv'] = wkv[:, :, B_NOPE:].reshape(B_KV_RANK, B_WIDTH).astype(BF16)
    zpad = jnp.zeros((HEAD_PAD - B_NOPE - B_ROPE,), F32)
    p['qw'] = jnp.tile(jnp.concatenate([qn_nope[l], qn_rope[l], zpad]), B_HEADS).reshape(1, -1)
    p['kw'] = jnp.tile(jnp.concatenate([kn_nope[l], jnp.zeros((HEAD_PAD - B_NOPE,), F32)]), B_HEADS).reshape(1, -1)
    p['knr'] = jnp.concatenate([kn_rope[l], jnp.zeros((LANE - B_ROPE,), F32)]).reshape(1, LANE)
    p['qan'] = q_a_norm[l].reshape(1, -1)
    p['kvan'] = kv_a_norm[l].reshape(1, -1)
    p['lb'] = lbs[l]
    p['cnorm'] = jnp.tile(c_norm[l], C_HEADS).reshape(1, C_WIDTH)
    wo = w_out[l].astype(BF16)
    p['wa'], p['wb'], p['wc'] = wo[:A_WIDTH], wo[A_WIDTH:A_WIDTH + B_WIDTH], wo[A_WIDTH + B_WIDTH:]
    p['ln2'] = ln2[l].reshape(1, D_MODEL)
    wr = jnp.zeros((D_MODEL, LANE), F32).at[:, :N_GROUPS].set(w_group[l])
    p['wr'] = wr.at[:, N_GROUPS:N_GROUPS + N_EXPERTS].set(w_router[l])
    br = jnp.zeros((1, LANE), F32).at[0, :N_GROUPS].set(b_group[l])
    p['br'] = br.at[0, N_GROUPS:N_GROUPS + N_EXPERTS].set(b_router[l])
    p['w1'], p['w3'], p['w2'] = w1[l].astype(BF16), w3[l].astype(BF16), w2[l].astype(BF16)
    return p


def _static_tables():
    t = {}
    t['ind_a'] = jnp.asarray(_block_ind(A_WIDTH, A_HEAD_DIM), BF16)
    t['ind_c'] = jnp.asarray(_block_ind(C_WIDTH, C_HEAD_DIM), BF16)
    expb = np.zeros((2, LANE, A_WIDTH), np.float32)
    expg = np.zeros((2, LANE, A_WIDTH), np.float32)
    for d in range(2):
        for h in range(A_HEADS):
            expb[d, A_HEADS * d + h, h * A_HEAD_DIM:(h + 1) * A_HEAD_DIM] = 1.0
            expg[d, 2 * A_HEADS + A_HEADS * d + h, h * A_HEAD_DIM:(h + 1) * A_HEAD_DIM] = 1.0
    t['expb'] = jnp.asarray(expb, BF16)
    t['expg'] = jnp.asarray(expg, BF16)
    HW = B_HEADS * HEAD_PAD
    lane = np.arange(HW)
    seg = np.where(lane % HEAD_PAD < B_NOPE, 0, np.where(lane % HEAD_PAD < B_NOPE + B_ROPE, 1, 2))
    head = lane // HEAD_PAD
    indq = (head[:, None] == head[None, :]) & (seg[:, None] == seg[None, :]) & (seg[:, None] < 2)
    t['indq'] = jnp.asarray(indq.astype(np.float32), BF16)
    t['invn'] = jnp.asarray(np.where(seg == 0, 1.0 / B_NOPE, 1.0 / B_ROPE).astype(np.float32)).reshape(1, HW)
    place = np.zeros((LANE, HW), np.float32)
    for h in range(B_HEADS):
        for r in range(B_ROPE):
            place[r, h * HEAD_PAD + B_NOPE + r] = 1.0
    t['place'] = jnp.asarray(place, BF16)
    mall, masks = _hgrn_tables()
    t['mall'] = jnp.asarray(mall, BF16)
    t['masks'] = jnp.asarray(masks, F32)
    return t


def _trunk(x, params, tabs, rope):
    B, L, D = x.shape
    T = B * L
    xf = x.reshape(T, D)
    for p in params:
        a_in, b_in, c_in = _proj(xf, p['ln1'], p['w_in'])
        o_a = _gdn(a_in.reshape(B, L, A_PAD), p['convw'], p['alog'], p['dtb'], p['anorm'], tabs['ind_a'],
                   tabs['expb'], tabs['expg'])
        q, k, v = _mla_prep(b_in.reshape(B, L, B_PAD), rope, p['qan'], p['kvan'], p['wqb'], p['wk'], p['wv'],
                            p['qw'], p['kw'], p['knr'], tabs['indq'], tabs['invn'], tabs['place'])
        o_b = _attn(q, k, v)
        o_c = _hgrn(c_in.reshape(B, L, C_PAD), p['lb'], p['cnorm'], tabs['mall'], tabs['masks'], tabs['ind_c'])
        x1, h, route = _outproj(xf, o_a.reshape(T, A_WIDTH), o_b.reshape(T, B_WIDTH), o_c.reshape(T, C_WIDTH),
                                p['wa'], p['wb'], p['wc'], p['ln2'], p['wr'], p['br'])
        blk_e, nact, ids, pos = _dispatch(route)
        y = _moe(blk_e, nact, ids, h, p['w1'], p['w3'], p['w2'])
        xf = _combine(pos, x1, route, y)
    return xf.reshape(B, L, D)


def kernel(x_prompt, x_sample, ln1, w_in, conv_w, a_log, dt_bias, a_norm, q_a_norm, w_qb, kv_a_norm, w_kvb, qn_nope,
           qn_rope, kn_nope, kn_rope, c_lower_bounds, c_norm, w_out, ln2, w_group, b_group, w_router, b_router,
           w1, w3, w2):
    lbs = jnp.cumsum(jax.nn.softmax(c_lower_bounds.astype(F32), axis=0), axis=0)
    lbs = lbs - lbs[0:1]
    params = [_layer_params(l, ln1, w_in, conv_w, a_log, dt_bias, a_norm, q_a_norm, w_qb, kv_a_norm, w_kvb, qn_nope,
                            qn_rope, kn_nope, kn_rope, lbs, c_norm, w_out, ln2, w_group, b_group, w_router, b_router,
                            w1, w3, w2) for l in range(DEPTH)]
    tabs = _static_tables()
    assert x_prompt.shape[1:] == x_sample.shape[1:]
    rope = _rope_tables(x_prompt.shape[1])
    nb = x_prompt.shape[0]
    y = _trunk(jnp.concatenate([x_prompt, x_sample], axis=0), params, tabs, rope)
    return (y[:nb], y[nb:])
```

```python
import functools
import math

import numpy as np
import jax
import jax.numpy as jnp
from jax import lax
from jax.experimental import pallas as pl
from jax.experimental.pallas import tpu as pltpu

F32 = jnp.float32
BF16 = jnp.bfloat16

D_MODEL = 1024
DEPTH = 2
EPS = 1e-6
A_HEADS, A_HEAD_DIM, CONV_K = 6, 64, 5
A_WIDTH = A_HEADS * A_HEAD_DIM
B_HEADS, B_Q_RANK, B_KV_RANK, B_NOPE, B_ROPE, B_V = 6, 256, 128, 64, 32, 64
B_WIDTH = B_HEADS * B_V
ROPE_THETA = 10000.0
C_HEADS, C_EXPAND, C_HEAD_DIM = 4, 128, 64
C_WIDTH = C_HEADS * C_HEAD_DIM
C_FDIM = C_HEADS * C_EXPAND
D_MIX = A_WIDTH + B_WIDTH + C_WIDTH
A_IN = 4 * A_WIDTH + 4 * A_HEADS
B_IN = B_Q_RANK + B_KV_RANK + B_ROPE
C_IN = 3 * C_FDIM + 2 * C_WIDTH
N_GROUPS, EXPERTS_PER_GROUP, TOP_K, D_EXPERT = 4, 8, 2, 512
N_EXPERTS = N_GROUPS * EXPERTS_PER_GROUP

LANE = 128
CHUNK = 64
A_PAD = 4 * A_WIDTH + LANE
B_PAD = 512
C_PAD = C_IN
N_PAD = A_PAD + B_PAD + C_PAD
HEAD_PAD = LANE
MOE_ROWS = 256
TOK_TILE = 256
VMEM_LIMIT = 56 * 1024 * 1024


def _cparams(sem):
    return pltpu.CompilerParams(dimension_semantics=sem, vmem_limit_bytes=VMEM_LIMIT)


def _dot(a, b):
    return jnp.dot(a.astype(BF16), b.astype(BF16), preferred_element_type=F32)


def _dot_nt(a, b):
    return lax.dot_general(a.astype(BF16), b.astype(BF16), (((1,), (1,)), ((), ())), preferred_element_type=F32)


def _dot_tn(a, b):
    return lax.dot_general(a.astype(BF16), b.astype(BF16), (((0,), (0,)), ((), ())), preferred_element_type=F32)


def _split(x):
    hi = x.astype(BF16)
    lo = (x - hi.astype(F32)).astype(BF16)
    return hi, lo


def _sel_dot(sel, x):
    hi, lo = _split(x)
    return jnp.dot(sel, hi, preferred_element_type=F32) + jnp.dot(sel, lo, preferred_element_type=F32)


def _sel2_dot(sel2, x):
    hi, lo = _split(x)
    return jnp.dot(sel2, jnp.concatenate([hi, lo], axis=0), preferred_element_type=F32)


def _dot_sel(x, sel):
    hi, lo = _split(x)
    return jnp.dot(hi, sel, preferred_element_type=F32) + jnp.dot(lo, sel, preferred_element_type=F32)


def _sigmoid(x):
    return 1.0 / (1.0 + jnp.exp(-x))


def _silu(x):
    return x * _sigmoid(x)


def _softplus(x):
    return jnp.maximum(x, 0.0) + jnp.log(1.0 + jnp.exp(-jnp.abs(x)))


def _rms(x, w):
    return x * lax.rsqrt(jnp.mean(x * x, axis=-1, keepdims=True) + EPS) * w


def _full(shape):
    return pl.BlockSpec(shape, lambda *_: (0,) * len(shape))


def _proj_kernel(x_ref, ln_ref, w_ref, a_ref, b_ref, c_ref):
    h = _rms(x_ref[...], ln_ref[...]).astype(BF16)
    a_ref[...] = jnp.dot(h, w_ref[:, 0:A_PAD], preferred_element_type=F32)
    b_ref[...] = jnp.dot(h, w_ref[:, A_PAD:A_PAD + B_PAD], preferred_element_type=F32)
    c_ref[...] = jnp.dot(h, w_ref[:, A_PAD + B_PAD:N_PAD], preferred_element_type=F32)


def _proj(x, ln, w_pad):
    T = x.shape[0]
    tm = TOK_TILE
    return pl.pallas_call(
        _proj_kernel,
        grid=(T // tm,),
        in_specs=[pl.BlockSpec((tm, D_MODEL), lambda i: (i, 0)), _full((1, D_MODEL)), _full((D_MODEL, N_PAD))],
        out_specs=[pl.BlockSpec((tm, A_PAD), lambda i: (i, 0)), pl.BlockSpec((tm, B_PAD), lambda i: (i, 0)),
                   pl.BlockSpec((tm, C_PAD), lambda i: (i, 0))],
        out_shape=[jax.ShapeDtypeStruct((T, A_PAD), F32), jax.ShapeDtypeStruct((T, B_PAD), F32),
                   jax.ShapeDtypeStruct((T, C_PAD), F32)],
        compiler_params=_cparams(("parallel",)),
        name="proj",
    )(x, ln, w_pad)


def _gdn_kernel(qkv_ref, gate_ref, gates_ref, convw_ref, alog_ref, dtb_ref, anorm_ref, ind_ref, expb_ref, expg_ref,
                pick_ref, out_ref, q_s, k_s, v_s, of_s, ob_s, s_s, *, L):
    nc = L // CHUNK
    W = A_WIDTH
    row = lax.broadcasted_iota(jnp.int32, (CHUNK, CHUNK), 0)
    col = lax.broadcasted_iota(jnp.int32, (CHUNK, CHUNK), 1)
    lane = lax.broadcasted_iota(jnp.int32, (CHUNK, LANE), 1)
    row2 = lax.broadcasted_iota(jnp.int32, (CHUNK, 2 * CHUNK), 0)
    col2 = lax.broadcasted_iota(jnp.int32, (CHUNK, 2 * CHUNK), 1) % CHUNK
    ind = ind_ref[...]

    def conv_chunk(c, carry):
        t0 = pl.multiple_of(c * CHUNK, CHUNK)
        tp = pl.multiple_of(jnp.maximum(t0 - 8, 0), 8)
        tn = pl.multiple_of(jnp.minimum(t0 + CHUNK, L - 8), 8)
        for g in range(3):
            cs = slice(g * W, (g + 1) * W)
            main = qkv_ref[pl.ds(t0, CHUNK), cs]
            prev = jnp.where(c > 0, qkv_ref[pl.ds(tp, 8), cs], 0.0)
            nxt = jnp.where(c < nc - 1, qkv_ref[pl.ds(tn, 8), cs], 0.0)
            win = jnp.concatenate([prev, main, nxt], axis=0)
            acc = win * convw_ref[2:3, cs]
            for j in (0, 1, 3, 4):
                acc = acc + pltpu.roll(win, (2 - j) % (CHUNK + 16), 0) * convw_ref[j:j + 1, cs]
            y = _silu(acc[8:8 + CHUNK])
            if g < 2:
                y = y * lax.rsqrt(_dot_sel(y * y, ind) + EPS)
            if g == 0:
                q_s[pl.ds(t0, CHUNK), :] = (y * (A_HEAD_DIM ** -0.5)).astype(BF16)
            elif g == 1:
                k_s[pl.ds(t0, CHUNK), :] = y.astype(BF16)
            else:
                v_s[pl.ds(t0, CHUNK), :] = y
        return carry

    lax.fori_loop(0, nc, conv_chunk, 0)

    coef = -jnp.exp(alog_ref[...])
    dtb = dtb_ref[...]
    g_lane = (lane >= 2 * A_HEADS) & (lane < 4 * A_HEADS)

    def wide(c, d):
        t0 = pl.multiple_of(c * CHUNK, CHUNK)
        qc = q_s[pl.ds(t0, CHUNK), :].astype(F32)
        kc = k_s[pl.ds(t0, CHUNK), :].astype(F32)
        vc = v_s[pl.ds(t0, CHUNK), :]
        gt = gates_ref[pl.ds(t0, CHUNK), :]
        beta = _sigmoid(gt)
        gl = jnp.where(g_lane, coef * _softplus(gt + dtb), 0.0)
        tri2 = ((col2 <= row2) if d == 0 else (col2 >= row2)).astype(BF16)
        gcum = _sel2_dot(tri2, gl)
        total = gcum[CHUNK - 1:CHUNK] if d == 0 else gcum[0:1]
        e_in = jnp.exp(gcum)
        e_out = jnp.exp(total - gcum)
        g_last = jnp.broadcast_to(jnp.exp(total), (8, LANE))
        expb = expb_ref[d]
        expg = expg_ref[d]
        beta_w = jnp.dot(beta.astype(BF16), expb, preferred_element_type=F32)
        ein_w = jnp.dot(e_in.astype(BF16), expg, preferred_element_type=F32)
        eout_w = jnp.dot(e_out.astype(BF16), expg, preferred_element_type=F32)
        chi, clo = _split(gcum)
        nt = (((1,), (1,)), ((), ()))
        pick = pick_ref[d]
        kb = kc * beta_w
        return dict(
            t0=t0, q=qc, k=kc, kb=kb, vb=vc * beta_w, kbe=kb * ein_w, qe=qc * ein_w, kd=kc * eout_w,
            glast=_dot_sel(g_last, expg),
            gcol=_dot_sel(gcum, expg),
            grow=(lax.dot_general(pick, chi, nt, preferred_element_type=F32)
                  + lax.dot_general(pick, clo, nt, preferred_element_type=F32)),
            incl=(col <= row) if d == 0 else (col >= row),
            strict=(col < row) if d == 0 else (col > row))

    def scan_pair(i, carry):
        wd = [wide(i, 0), wide(nc - 1 - i, 1)]
        ks = [(d, h) for d in (0, 1) for h in range(A_HEADS)]
        hs = lambda h: slice(h * A_HEAD_DIM, (h + 1) * A_HEAD_DIM)
        dec, n, qk, x = {}, {}, {}, {}
        for d, h in ks:
            w = wd[d]
            dec[d, h] = jnp.exp(jnp.minimum(w['gcol'][:, hs(h)] - w['grow'][h * CHUNK:(h + 1) * CHUNK], 0.0))
        for d, h in ks:
            w = wd[d]
            n[d, h] = jnp.where(w['strict'], _dot_nt(w['kb'][:, hs(h)], w['k'][:, hs(h)]) * dec[d, h], 0.0)
        for d, h in ks:
            w = wd[d]
            qk[d, h] = jnp.where(w['incl'], _dot_nt(w['q'][:, hs(h)], w['k'][:, hs(h)]) * dec[d, h], 0.0)
            x[d, h] = jnp.concatenate([w['vb'][:, hs(h)], w['kbe'][:, hs(h)]], axis=1)
        xw = 2 * A_HEAD_DIM
        z = {k: _dot(n[k], jnp.concatenate([x[k], n[k]], axis=1)) for k in ks}
        x = {k: x[k] - z[k][:, :xw] for k in ks}
        for _ in range(4):
            p = {k: z[k][:, xw:] for k in ks}
            z = {k: _dot(p[k], jnp.concatenate([x[k], p[k]], axis=1)) for k in ks}
            x = {k: x[k] + z[k][:, :xw] for k in ks}
        x = {k: x[k] + _dot(z[k][:, xw:], x[k]) for k in ks}
        s = {(d, h): s_s[d * A_HEADS + h] for d, h in ks}
        v_new = {k: x[k][:, :A_HEAD_DIM] - _dot(x[k][:, A_HEAD_DIM:], s[k]) for k in ks}
        o = {(d, h): _dot(wd[d]['qe'][:, hs(h)], s[d, h]) + _dot(qk[d, h], v_new[d, h]) for d, h in ks}
        for d, h in ks:
            s_s[d * A_HEADS + h] = (s[d, h] * wd[d]['glast'][0:1, hs(h)]
                                    + _dot_tn(wd[d]['kd'][:, hs(h)], v_new[d, h]))
        of_s[pl.ds(wd[0]['t0'], CHUNK), :] = jnp.concatenate([o[0, h] for h in range(A_HEADS)], axis=1)
        ob_s[pl.ds(wd[1]['t0'], CHUNK), :] = jnp.concatenate([o[1, h] for h in range(A_HEADS)], axis=1)
        return carry

    s_s[...] = jnp.zeros_like(s_s)
    lax.fori_loop(0, nc, scan_pair, 0)

    rows = 256

    def finish(i, carry):
        t0 = pl.multiple_of(i * rows, rows)
        o = of_s[pl.ds(t0, rows), :] + ob_s[pl.ds(t0, rows), :]
        ms = _dot_sel(o * o, ind) * (1.0 / A_HEAD_DIM)
        o = o * lax.rsqrt(ms + EPS) * anorm_ref[...]
        out_ref[pl.ds(t0, rows), :] = o * _silu(gate_ref[pl.ds(t0, rows), :])
        return carry

    lax.fori_loop(0, L // rows, finish, 0)


def _gdn(a_in, convw, alog, dtb, anorm, ind, expb, expg, pick):
    B, L, _ = a_in.shape
    W = A_WIDTH
    kern = functools.partial(_gdn_kernel, L=L)
    one = pl.Buffered(1)
    return pl.pallas_call(
        kern,
        grid=(B,),
        in_specs=[pl.BlockSpec((None, L, 3 * W), lambda b: (b, 0, 0), pipeline_mode=one),
                  pl.BlockSpec((None, L, W), lambda b: (b, 0, 3), pipeline_mode=one),
                  pl.BlockSpec((None, L, LANE), lambda b: (b, 0, 4 * W // LANE), pipeline_mode=one),
                  _full((8, 3 * W)), _full((1, LANE)), _full((1, LANE)), _full((1, W)), _full((W, W)),
                  _full((2, LANE, W)), _full((2, LANE, W)), _full((2, A_HEADS * CHUNK, LANE))],
        out_specs=pl.BlockSpec((None, L, W), lambda b: (b, 0, 0)),
        out_shape=jax.ShapeDtypeStruct((B, L, W), F32),
        scratch_shapes=[pltpu.VMEM((L, W), BF16), pltpu.VMEM((L, W), BF16), pltpu.VMEM((L, W), F32),
                        pltpu.VMEM((L, W), F32), pltpu.VMEM((L, W), F32),
                        pltpu.VMEM((2 * A_HEADS, A_HEAD_DIM, A_HEAD_DIM), F32)],
        compiler_params=_cparams(("parallel",)),
        name="gdn",
    )(a_in, a_in, a_in, convw, alog, dtb, anorm, ind, expb, expg, pick)


def _mla_prep_kernel(b_ref, tab_ref, qan_ref, kvan_ref, wqb_ref, wk_ref, wv_ref, qw_ref, kw_ref, knr_ref,
                     indq_ref, invn_ref, place_ref, q_out, k_out, v_out):
    b = b_ref[...]
    tab = tab_ref[...]
    cq = _rms(b[:, :B_Q_RANK], qan_ref[...])
    ckv = _rms(b[:, B_Q_RANK:B_Q_RANK + B_KV_RANK], kvan_ref[...])
    kr = b[:, B_Q_RANK + B_KV_RANK:]
    indq = indq_ref[...]
    invn = invn_ref[...]

    q = _dot(cq, wqb_ref[...])
    qn = q * lax.rsqrt(_dot_sel(q * q, indq) * invn + EPS) * qw_ref[...]
    scale = (B_NOPE + B_ROPE) ** -0.5
    pieces = []
    for h in range(B_HEADS):
        x = qn[:, h * HEAD_PAD:(h + 1) * HEAD_PAD]
        half = B_ROPE // 2
        r = (x * tab[:, 0:LANE] + pltpu.roll(x, LANE - half, 1) * tab[:, LANE:2 * LANE]
             + pltpu.roll(x, half, 1) * tab[:, 2 * LANE:3 * LANE])
        pieces.append(r * scale)
    q_out[...] = jnp.concatenate(pieces, axis=1).astype(BF16)

    kk = _dot(ckv, wk_ref[...])
    kn = kk * lax.rsqrt(_dot_sel(kk * kk, indq) * invn + EPS) * kw_ref[...]
    krn = kr * lax.rsqrt(jnp.sum(kr * kr, axis=-1, keepdims=True) * (1.0 / B_ROPE) + EPS) * knr_ref[...]
    half = B_ROPE // 2
    krr = (krn * tab[:, 3 * LANE:4 * LANE] + pltpu.roll(krn, LANE - half, 1) * tab[:, 4 * LANE:5 * LANE]
           + pltpu.roll(krn, half, 1) * tab[:, 5 * LANE:6 * LANE])
    k_out[...] = (kn + jnp.dot(krr.astype(BF16), place_ref[...], preferred_element_type=F32)).astype(BF16)
    v_out[...] = _dot(ckv, wv_ref[...]).astype(BF16)


def _mla_prep(b_in, tab, qan, kvan, wqb, wk, wv, qw, kw, knr, indq, invn, place):
    B, L, _ = b_in.shape
    tm = TOK_TILE
    HW = B_HEADS * HEAD_PAD
    return pl.pallas_call(
        _mla_prep_kernel,
        grid=(B, L // tm),
        in_specs=[pl.BlockSpec((None, tm, B_PAD), lambda b, j: (b, j, 0)),
                  pl.BlockSpec((tm, 6 * LANE), lambda b, j: (j, 0)),
                  _full((1, B_Q_RANK)), _full((1, B_KV_RANK)), _full((B_Q_RANK, HW)), _full((B_KV_RANK, HW)),
                  _full((B_KV_RANK, B_WIDTH)), _full((1, HW)), _full((1, HW)), _full((1, LANE)),
                  _full((HW, HW)), _full((1, HW)), _full((LANE, HW))],
        out_specs=[pl.BlockSpec((None, tm, HW), lambda b, j: (b, j, 0)),
                   pl.BlockSpec((None, tm, HW), lambda b, j: (b, j, 0)),
                   pl.BlockSpec((None, tm, B_WIDTH), lambda b, j: (b, j, 0))],
        out_shape=[jax.ShapeDtypeStruct((B, L, HW), BF16), jax.ShapeDtypeStruct((B, L, HW), BF16),
                   jax.ShapeDtypeStruct((B, L, B_WIDTH), BF16)],
        compiler_params=_cparams(("parallel", "parallel")),
        name="mla_prep",
    )(b_in, tab, qan, kvan, wqb, wk, wv, qw, kw, knr, indq, invn, place)


def _attn_kernel(q_ref, k_ref, v_ref, o_ref):
    v = v_ref[...]
    outs = []
    for hh in range(2):
        q = q_ref[:, hh * HEAD_PAD:(hh + 1) * HEAD_PAD]
        k = k_ref[:, hh * HEAD_PAD:(hh + 1) * HEAD_PAD]
        s = lax.dot_general(q, k, (((1,), (1,)), ((), ())), preferred_element_type=F32)
        p = jnp.exp(s - jnp.max(s, axis=-1, keepdims=True))
        den = jnp.sum(p, axis=-1, keepdims=True)
        outs.append(jnp.dot(p.astype(BF16), v, preferred_element_type=F32) / den)
    lane = lax.broadcasted_iota(jnp.int32, outs[0].shape, 1)
    o_ref[...] = jnp.where(lane < B_V, outs[0], outs[1])


def _attn(q, k, v):
    B, L, _ = q.shape
    tq = 256
    return pl.pallas_call(
        _attn_kernel,
        grid=(B, B_HEADS // 2, L // tq),
        in_specs=[pl.BlockSpec((None, tq, 2 * HEAD_PAD), lambda b, p, j: (b, j, p)),
                  pl.BlockSpec((None, L, 2 * HEAD_PAD), lambda b, p, j: (b, 0, p)),
                  pl.BlockSpec((None, L, 2 * B_V), lambda b, p, j: (b, 0, p))],
        out_specs=pl.BlockSpec((None, tq, 2 * B_V), lambda b, p, j: (b, j, p)),
        out_shape=jax.ShapeDtypeStruct((B, L, B_WIDTH), F32),
        compiler_params=_cparams(("parallel", "parallel", "arbitrary")),
        name="mla_attn",
    )(q, k, v)


def _hgrn_tables():
    C = CHUNK
    m_all = np.zeros((8 * C, C), np.float32)
    masks = np.zeros((7, C, C), np.float32)
    for lvl in range(6):
        m = 1 << lvl
        for t in range(C):
            bd = (t // (2 * m)) * 2 * m + m - 1
            if t % (2 * m) >= m:
                m_all[lvl * C + t, bd + 1:t + 1] = 1.0
            else:
                m_all[lvl * C + t, t + 1:bd + 1] = 1.0
            for s in range(C):
                if t // (2 * m) == s // (2 * m) and t % (2 * m) >= m and s % (2 * m) < m:
                    masks[lvl, t, s] = 1.0
    for t in range(C):
        m_all[6 * C + t, :t + 1] = 1.0
        m_all[7 * C + t, t + 1:] = 1.0
    masks[6] = np.eye(C, dtype=np.float32)
    m_f = m_all.reshape(8, C, C)
    m_b = m_f[:, ::-1, ::-1]
    mk_b = masks[:, ::-1, ::-1]
    m2 = np.stack([m_f.reshape(8 * C, C), m_b.reshape(8 * C, C)])
    return (np.concatenate([m2, m2], axis=2), np.stack([masks, mk_b]))


def _hgrn_kernel(q_ref, ff_ref, fb_ref, i_ref, gate_ref, lb_ref, cnorm_ref, mall_ref, masks_ref, ind_ref,
                 out_ref, of_s, ob_s, s_s, *, L):
    nc = L // CHUNK
    C = CHUNK

    def wide(c, d):
        t0 = pl.multiple_of(c * C, C)
        q = _silu(q_ref[pl.ds(t0, C), :])
        fp = (ff_ref if d == 0 else fb_ref)[pl.ds(t0, C), :]
        lb = lb_ref[d:d + 1, :]
        lsig = jnp.minimum(fp, 0.0) - jnp.log(1.0 + jnp.exp(-jnp.abs(fp)))
        a = jnp.log(lb)
        bb = jnp.log(1.0 - lb) + lsig
        logf = jnp.maximum(a, bb) + jnp.log(1.0 + jnp.exp(-jnp.abs(a - bb)))
        kk = (1.0 - lb) * _sigmoid(-fp)
        e_all = jnp.exp(_sel2_dot(mall_ref[d], logf))
        return dict(t0=t0, q=q, k=kk, e=e_all, v=i_ref[pl.ds(t0, C), :])

    def scan_pair(i, carry):
        wd = [wide(i, 0), wide(nc - 1 - i, 1)]
        ks = [(d, h) for d in (0, 1) for h in range(C_HEADS)]
        ls = lambda h: slice(h * C_EXPAND, (h + 1) * C_EXPAND)
        vs = lambda h: slice(h * C_HEAD_DIM, (h + 1) * C_HEAD_DIM)
        qh = {(d, h): wd[d]['q'][:, ls(h)] for d, h in ks}
        kh = {(d, h): wd[d]['k'][:, ls(h)] for d, h in ks}
        vh = {(d, h): wd[d]['v'][:, vs(h)] for d, h in ks}
        lvl_e = lambda d, h, j: wd[d]['e'][j * C:(j + 1) * C, ls(h)]
        attn = {(d, h): _dot_nt(qh[d, h], kh[d, h]) * masks_ref[d, 6] for d, h in ks}
        for lvl in range(6):
            for d, h in ks:
                e = lvl_e(d, h, lvl)
                attn[d, h] = attn[d, h] + _dot_nt(qh[d, h] * e, kh[d, h] * e) * masks_ref[d, lvl]
        st = {(d, h): s_s[d * C_HEADS + h] for d, h in ks}
        o = {}
        for d, h in ks:
            o[d, h] = _dot(attn[d, h], vh[d, h]) + _dot_nt(qh[d, h] * lvl_e(d, h, 6), st[d, h])
        for d, h in ks:
            e_in = lvl_e(d, h, 6)
            f_last = e_in[C - 1:C] if d == 0 else e_in[0:1]
            s_s[d * C_HEADS + h] = st[d, h] * f_last + _dot_tn(vh[d, h], kh[d, h] * lvl_e(d, h, 7))
        of_s[pl.ds(wd[0]['t0'], C), :] = jnp.concatenate([o[0, h] for h in range(C_HEADS)], axis=1)
        ob_s[pl.ds(wd[1]['t0'], C), :] = jnp.concatenate([o[1, h] for h in range(C_HEADS)], axis=1)
        return carry

    s_s[...] = jnp.zeros_like(s_s)
    lax.fori_loop(0, nc, scan_pair, 0)

    rows = 256
    ind = ind_ref[...]

    def finish(i, carry):
        t0 = pl.multiple_of(i * rows, rows)
        o = of_s[pl.ds(t0, rows), :] + ob_s[pl.ds(t0, rows), :]
        ms = _dot_sel(o * o, ind) * (1.0 / C_HEAD_DIM)
        o = o * lax.rsqrt(ms + EPS) * cnorm_ref[...]
        out_ref[pl.ds(t0, rows), :] = o * _silu(gate_ref[pl.ds(t0, rows), :])
        return carry

    lax.fori_loop(0, L // rows, finish, 0)


def _hgrn(c_in, lb, cnorm, mall, masks, ind):
    B, L, _ = c_in.shape
    kern = functools.partial(_hgrn_kernel, L=L)
    one = pl.Buffered(1)
    F, W = C_FDIM, C_WIDTH
    return pl.pallas_call(
        kern,
        grid=(B,),
        in_specs=[pl.BlockSpec((None, L, F), lambda b: (b, 0, 0), pipeline_mode=one),
                  pl.BlockSpec((None, L, F), lambda b: (b, 0, 1), pipeline_mode=one),
                  pl.BlockSpec((None, L, F), lambda b: (b, 0, 2), pipeline_mode=one),
                  pl.BlockSpec((None, L, W), lambda b: (b, 0, 3 * F // W), pipeline_mode=one),
                  pl.BlockSpec((None, L, W), lambda b: (b, 0, 3 * F // W + 1), pipeline_mode=one),
                  _full((2, F)), _full((1, W)), _full((2, 8 * CHUNK, 2 * CHUNK)), _full((2, 7, CHUNK, CHUNK)),
                  _full((W, W))],
        out_specs=pl.BlockSpec((None, L, W), lambda b: (b, 0, 0)),
        out_shape=jax.ShapeDtypeStruct((B, L, W), F32),
        scratch_shapes=[pltpu.VMEM((L, W), F32), pltpu.VMEM((L, W), F32),
                        pltpu.VMEM((2 * C_HEADS, C_HEAD_DIM, C_EXPAND), F32)],
        compiler_params=_cparams(("parallel",)),
        name="hgrn2",
    )(c_in, c_in, c_in, c_in, c_in, lb, cnorm, mall, masks, ind)


def _outproj_kernel(x_ref, oa_ref, ob_ref, oc_ref, wa_ref, wb_ref, wc_ref, ln_ref, wr_ref, br_ref,
                    x1_ref, h_ref, route_ref):
    x1 = x_ref[...] + _dot(oa_ref[...], wa_ref[...]) + _dot(ob_ref[...], wb_ref[...]) + _dot(oc_ref[...], wc_ref[...])
    x1_ref[...] = x1
    h = _rms(x1, ln_ref[...])
    h_ref[...] = h
    hh, hl = _split(h)
    wr = wr_ref[...]
    wh, wl = _split(wr)
    lg = (jnp.dot(hh, wh, preferred_element_type=F32) + jnp.dot(hh, wl, preferred_element_type=F32)
          + jnp.dot(hl, wh, preferred_element_type=F32)) + br_ref[...]
    lane = lax.broadcasted_iota(jnp.int32, lg.shape, 1).astype(F32)
    neg = -1e30
    far = 1e9
    gmask = lane < N_GROUPS
    gl = jnp.where(gmask, lg, neg)
    gmax = jnp.max(gl, axis=-1, keepdims=True)
    g_idx = jnp.min(jnp.where(gl == gmax, lane, far), axis=-1, keepdims=True)
    g_w = 1.0 / jnp.sum(jnp.where(gmask, jnp.exp(gl - gmax), 0.0), axis=-1, keepdims=True)
    e_lane = lane - N_GROUPS
    emask = (e_lane >= g_idx * EXPERTS_PER_GROUP) & (e_lane < (g_idx + 1.0) * EXPERTS_PER_GROUP)
    el = jnp.where(emask, lg, neg)
    m1 = jnp.max(el, axis=-1, keepdims=True)
    i1 = jnp.min(jnp.where(el == m1, lane, far), axis=-1, keepdims=True)
    el2 = jnp.where(lane == i1, neg, el)
    m2 = jnp.max(el2, axis=-1, keepdims=True)
    i2 = jnp.min(jnp.where((el2 == m2) & emask & (lane != i1), lane, far), axis=-1, keepdims=True)
    r = jnp.exp(m2 - m1)
    w1 = g_w / (1.0 + r)
    w2 = g_w * r / (1.0 + r)
    route_ref[...] = jnp.where(lane == 0.0, i1 - N_GROUPS,
                               jnp.where(lane == 1.0, i2 - N_GROUPS,
                                         jnp.where(lane == 2.0, w1, jnp.where(lane == 3.0, w2, 0.0))))


def _outproj(x, oa, ob, oc, wa, wb, wc, ln, wr, br):
    T = x.shape[0]
    tm = TOK_TILE
    row = lambda w: pl.BlockSpec((tm, w), lambda i: (i, 0))
    return pl.pallas_call(
        _outproj_kernel,
        grid=(T // tm,),
        in_specs=[row(D_MODEL), row(A_WIDTH), row(B_WIDTH), row(C_WIDTH),
                  _full((A_WIDTH, D_MODEL)), _full((B_WIDTH, D_MODEL)), _full((C_WIDTH, D_MODEL)),
                  _full((1, D_MODEL)), _full((D_MODEL, LANE)), _full((1, LANE))],
        out_specs=[row(D_MODEL), row(D_MODEL), row(LANE)],
        out_shape=[jax.ShapeDtypeStruct((T, D_MODEL), F32), jax.ShapeDtypeStruct((T, D_MODEL), F32),
                   jax.ShapeDtypeStruct((T, LANE), F32)],
        compiler_params=_cparams(("parallel",)),
        name="outproj_router",
    )(x, oa, ob, oc, wa, wb, wc, ln, wr, br)


def _gather_rows(src_hbm, buf, sem, cur_ids, nxt_ids, n_rows):
    i = pl.program_id(0)
    slot = i % 2

    def row_copy(ids, r, s):
        return pltpu.make_async_copy(src_hbm.at[pl.ds(ids[0, 0, r], 1)], buf.at[s, pl.ds(r, 1)], sem.at[s])

    def start_all(ids, s):
        def body(r, carry):
            row_copy(ids, r, s).start()
            return carry
        lax.fori_loop(0, n_rows, body, 0, unroll=8)

    def wait_all(ids, s):
        def body(r, carry):
            row_copy(ids, r, s).wait()
            return carry
        lax.fori_loop(0, n_rows, body, 0, unroll=8)

    @pl.when(i == 0)
    def _():
        start_all(cur_ids, 0)

    start_all(nxt_ids, 1 - slot)
    wait_all(cur_ids, slot)

    @pl.when(i == pl.num_programs(0) - 1)
    def _():
        wait_all(nxt_ids, 1 - slot)

    return slot


def _moe_kernel(blk_e_ref, nact_ref, ids_ref, nxt_ref, h_hbm, w1_ref, w3_ref, w2_ref, y_ref, xb, sem):
    slot = _gather_rows(h_hbm, xb, sem, ids_ref, nxt_ref, MOE_ROWS)
    active = pl.program_id(0) < nact_ref[0]

    @pl.when(active)
    def _():
        x = xb[slot].astype(BF16)
        a = jnp.dot(x, w1_ref[...], preferred_element_type=F32)
        g = jnp.dot(x, w3_ref[...], preferred_element_type=F32)
        y_ref[...] = jnp.dot((_silu(a) * g).astype(BF16), w2_ref[...], preferred_element_type=F32)

    @pl.when(jnp.logical_not(active))
    def _():
        y_ref[...] = jnp.zeros_like(y_ref)


def _moe(blk_e, nact, ids, h, w1, w3, w2):
    nblk = ids.shape[0]
    P = nblk * MOE_ROWS
    grid_spec = pltpu.PrefetchScalarGridSpec(
        num_scalar_prefetch=2,
        grid=(nblk,),
        in_specs=[pl.BlockSpec((1, 1, MOE_ROWS), lambda i, be, na: (i, 0, 0), memory_space=pltpu.SMEM),
                  pl.BlockSpec((1, 1, MOE_ROWS), lambda i, be, na: (jnp.minimum(i + 1, nblk - 1), 0, 0),
                               memory_space=pltpu.SMEM),
                  pl.BlockSpec(memory_space=pl.ANY),
                  pl.BlockSpec((None, D_MODEL, D_EXPERT), lambda i, be, na: (be[i], 0, 0)),
                  pl.BlockSpec((None, D_MODEL, D_EXPERT), lambda i, be, na: (be[i], 0, 0)),
                  pl.BlockSpec((None, D_EXPERT, D_MODEL), lambda i, be, na: (be[i], 0, 0))],
        out_specs=pl.BlockSpec((MOE_ROWS, D_MODEL), lambda i, be, na: (i, 0)),
        scratch_shapes=[pltpu.VMEM((2, MOE_ROWS, D_MODEL), F32), pltpu.SemaphoreType.DMA((2,))],
    )
    return pl.pallas_call(
        _moe_kernel,
        grid_spec=grid_spec,
        out_shape=jax.ShapeDtypeStruct((P, D_MODEL), F32),
        compiler_params=_cparams(("arbitrary",)),
        name="moe_ffn",
    )(blk_e, nact, ids, ids, h, w1, w3, w2)


def _combine_kernel(pos_ref, nxt_ref, x_ref, route_ref, y_hbm, out_ref, yb, sem):
    tm = x_ref.shape[0]
    slot = _gather_rows(y_hbm, yb, sem, pos_ref, nxt_ref, TOP_K * tm)
    route = route_ref[...]
    out_ref[...] = x_ref[...] + yb[slot, 0:tm, :] * route[:, 2:3] + yb[slot, tm:2 * tm, :] * route[:, 3:4]


def _combine(pos, x1, route, y):
    T = x1.shape[0]
    tm = TOK_TILE
    nt = T // tm
    return pl.pallas_call(
        _combine_kernel,
        grid=(nt,),
        in_specs=[pl.BlockSpec((1, 1, TOP_K * tm), lambda i: (i, 0, 0), memory_space=pltpu.SMEM),
                  pl.BlockSpec((1, 1, TOP_K * tm), lambda i: (jnp.minimum(i + 1, nt - 1), 0, 0),
                               memory_space=pltpu.SMEM),
                  pl.BlockSpec((tm, D_MODEL), lambda i: (i, 0)),
                  pl.BlockSpec((tm, LANE), lambda i: (i, 0)),
                  pl.BlockSpec(memory_space=pl.ANY)],
        out_specs=pl.BlockSpec((tm, D_MODEL), lambda i: (i, 0)),
        out_shape=jax.ShapeDtypeStruct((T, D_MODEL), F32),
        scratch_shapes=[pltpu.VMEM((2, TOP_K * tm, D_MODEL), F32), pltpu.SemaphoreType.DMA((2,))],
        compiler_params=_cparams(("arbitrary",)),
        name="moe_combine",
    )(pos, pos, x1, route, y)


def _dispatch(route):
    T = route.shape[0]
    A = T * TOP_K
    e_flat = route[:, 0:TOP_K].astype(jnp.int32).reshape(A)
    tok_flat = jnp.repeat(jnp.arange(T, dtype=jnp.int32), TOP_K)
    order = jnp.argsort(e_flat)
    e_s, tok_s = e_flat[order], tok_flat[order]
    counts = jnp.zeros((N_EXPERTS,), jnp.int32).at[e_flat].add(1)
    starts = jnp.cumsum(counts) - counts
    padded = (counts + MOE_ROWS - 1) // MOE_ROWS * MOE_ROWS
    pends = jnp.cumsum(padded)
    pstarts = pends - padded
    dest = pstarts[e_s] + (jnp.arange(A, dtype=jnp.int32) - starts[e_s])
    P = A + N_EXPERTS * MOE_ROWS
    nblk = P // MOE_ROWS
    ids = jnp.zeros((P,), jnp.int32).at[dest].set(tok_s).reshape(nblk, 1, MOE_ROWS)
    blk_e = jnp.minimum(jnp.searchsorted(pends, jnp.arange(nblk, dtype=jnp.int32) * MOE_ROWS, side='right'),
                        N_EXPERTS - 1).astype(jnp.int32)
    nact = (pends[-1:] // MOE_ROWS).astype(jnp.int32)
    pos = jnp.zeros((A,), jnp.int32).at[order].set(dest).reshape(T // TOK_TILE, TOK_TILE, TOP_K)
    pos = jnp.transpose(pos, (0, 2, 1)).reshape(T // TOK_TILE, 1, TOP_K * TOK_TILE)
    return blk_e, nact, ids, pos


def _block_ind(width, seg):
    i = np.arange(width)
    return (i[:, None] // seg == i[None, :] // seg).astype(np.float32)


def _rope_tables(L):
    half = B_ROPE // 2
    inv = ROPE_THETA ** (-jnp.arange(0, B_ROPE, 2, dtype=F32) / B_ROPE)
    ang = jnp.arange(L, dtype=F32)[:, None] * inv[None, :]
    cos, sin = jnp.cos(ang), jnp.sin(ang)
    z = lambda n: jnp.zeros((L, n), F32)
    one = jnp.ones((L, B_NOPE), F32)
    rest = LANE - B_NOPE - B_ROPE
    cq = jnp.concatenate([one, cos, cos, z(rest)], axis=1)
    s1q = jnp.concatenate([z(B_NOPE), -sin, z(half), z(rest)], axis=1)
    s2q = jnp.concatenate([z(B_NOPE), z(half), sin, z(rest)], axis=1)
    ck = jnp.concatenate([cos, cos, z(LANE - B_ROPE)], axis=1)
    s1k = jnp.concatenate([-sin, z(LANE - half)], axis=1)
    s2k = jnp.concatenate([z(half), sin, z(LANE - B_ROPE)], axis=1)
    return jnp.concatenate([cq, s1q, s2q, ck, s1k, s2k], axis=1)


def _layer_params(l, ln1, w_in, conv_w, a_log, dt_bias, a_norm, q_a_norm, w_qb, kv_a_norm, w_kvb, qn_nope, qn_rope,
                  kn_nope, kn_rope, lbs, c_norm, w_out, ln2, w_group, b_group, w_router, b_router, w1, w3, w2):
    p = {}
    w = w_in[l]
    wp = jnp.zeros((D_MODEL, N_PAD), F32)
    wp = wp.at[:, 0:A_IN].set(w[:, 0:A_IN])
    wp = wp.at[:, A_PAD:A_PAD + B_IN].set(w[:, A_IN:A_IN + B_IN])
    wp = wp.at[:, A_PAD + B_PAD:].set(w[:, A_IN + B_IN:])
    p['w_in'] = wp.astype(BF16)
    p['ln1'] = ln1[l].reshape(1, D_MODEL)
    p['convw'] = jnp.zeros((8, 3 * A_WIDTH), F32).at[0:CONV_K].set(conv_w[l])
    gpad = lambda v: jnp.zeros((1, LANE), F32).at[0, 2 * A_HEADS:4 * A_HEADS].set(v.reshape(-1))
    p['alog'] = gpad(a_log[l])
    p['dtb'] = gpad(dt_bias[l])
    p['anorm'] = jnp.tile(a_norm[l], A_HEADS).reshape(1, A_WIDTH)
    half = B_ROPE // 2
    wq = w_qb[l].reshape(B_Q_RANK, B_HEADS, B_NOPE + B_ROPE)
    wq = jnp.pad(wq, ((0, 0), (0, 0), (0, HEAD_PAD - B_NOPE - B_ROPE)))
    p['wqb'] = wq.reshape(B_Q_RANK, B_HEADS * HEAD_PAD).astype(BF16)
    wkv = w_kvb[l].reshape(B_KV_RANK, B_HEADS, B_NOPE + B_V)
    wk = jnp.pad(wkv[:, :, :B_NOPE], ((0, 0), (0, 0), (0, HEAD_PAD - B_NOPE)))
    p['wk'] = wk.reshape(B_KV_RANK, B_HEADS * HEAD_PAD).astype(BF16)
    p['wv'] = wkv[:, :, B_NOPE:].reshape(B_KV_RANK, B_WIDTH).astype(BF16)
    zpad = jnp.zeros((HEAD_PAD - B_NOPE - B_ROPE,), F32)
    p['qw'] = jnp.tile(jnp.concatenate([qn_nope[l], qn_rope[l], zpad]), B_HEADS).reshape(1, -1)
    p['kw'] = jnp.tile(jnp.concatenate([kn_nope[l], jnp.zeros((HEAD_PAD - B_NOPE,), F32)]), B_HEADS).reshape(1, -1)
    p['knr'] = jnp.concatenate([kn_rope[l], jnp.zeros((LANE - B_ROPE,), F32)]).reshape(1, LANE)
    p['qan'] = q_a_norm[l].reshape(1, -1)
    p['kvan'] = kv_a_norm[l].reshape(1, -1)
    p['lb'] = lbs[l]
    p['cnorm'] = jnp.tile(c_norm[l], C_HEADS).reshape(1, C_WIDTH)
    wo = w_out[l].astype(BF16)
    p['wa'], p['wb'], p['wc'] = wo[:A_WIDTH], wo[A_WIDTH:A_WIDTH + B_WIDTH], wo[A_WIDTH + B_WIDTH:]
    p['ln2'] = ln2[l].reshape(1, D_MODEL)
    wr = jnp.zeros((D_MODEL, LANE), F32).at[:, :N_GROUPS].set(w_group[l])
    p['wr'] = wr.at[:, N_GROUPS:N_GROUPS + N_EXPERTS].set(w_router[l])
    br = jnp.zeros((1, LANE), F32).at[0, :N_GROUPS].set(b_group[l])
    p['br'] = br.at[0, N_GROUPS:N_GROUPS + N_EXPERTS].set(b_router[l])
    p['w1'], p['w3'], p['w2'] = w1[l].astype(BF16), w3[l].astype(BF16), w2[l].astype(BF16)
    return p


def _static_tables():
    t = {}
    t['ind_a'] = jnp.asarray(_block_ind(A_WIDTH, A_HEAD_DIM), BF16)
    t['ind_c'] = jnp.asarray(_block_ind(C_WIDTH, C_HEAD_DIM), BF16)
    expb = np.zeros((2, LANE, A_WIDTH), np.float32)
    expg = np.zeros((2, LANE, A_WIDTH), np.float32)
    for d in range(2):
        for h in range(A_HEADS):
            expb[d, A_HEADS * d + h, h * A_HEAD_DIM:(h + 1) * A_HEAD_DIM] = 1.0
            expg[d, 2 * A_HEADS + A_HEADS * d + h, h * A_HEAD_DIM:(h + 1) * A_HEAD_DIM] = 1.0
    t['expb'] = jnp.asarray(expb, BF16)
    t['expg'] = jnp.asarray(expg, BF16)
    t['pick'] = jnp.asarray(np.repeat(np.transpose(expg[:, :, ::A_HEAD_DIM], (0, 2, 1)), CHUNK, axis=1), BF16)
    HW = B_HEADS * HEAD_PAD
    lane = np.arange(HW)
    seg = np.where(lane % HEAD_PAD < B_NOPE, 0, np.where(lane % HEAD_PAD < B_NOPE + B_ROPE, 1, 2))
    head = lane // HEAD_PAD
    indq = (head[:, None] == head[None, :]) & (seg[:, None] == seg[None, :]) & (seg[:, None] < 2)
    t['indq'] = jnp.asarray(indq.astype(np.float32), BF16)
    t['invn'] = jnp.asarray(np.where(seg == 0, 1.0 / B_NOPE, 1.0 / B_ROPE).astype(np.float32)).reshape(1, HW)
    place = np.zeros((LANE, HW), np.float32)
    for h in range(B_HEADS):
        for r in range(B_ROPE):
            place[r, h * HEAD_PAD + B_NOPE + r] = 1.0
    t['place'] = jnp.asarray(place, BF16)
    mall, masks = _hgrn_tables()
    t['mall'] = jnp.asarray(mall, BF16)
    t['masks'] = jnp.asarray(masks, F32)
    return t


def _trunk(x, params, tabs, rope):
    B, L, D = x.shape
    T = B * L
    xf = x.reshape(T, D)
    for p in params:
        a_in, b_in, c_in = _proj(xf, p['ln1'], p['w_in'])
        o_a = _gdn(a_in.reshape(B, L, A_PAD), p['convw'], p['alog'], p['dtb'], p['anorm'], tabs['ind_a'],
                   tabs['expb'], tabs['expg'], tabs['pick'])
        q, k, v = _mla_prep(b_in.reshape(B, L, B_PAD), rope, p['qan'], p['kvan'], p['wqb'], p['wk'], p['wv'],
                            p['qw'], p['kw'], p['knr'], tabs['indq'], tabs['invn'], tabs['place'])
        o_b = _attn(q, k, v)
        o_c = _hgrn(c_in.reshape(B, L, C_PAD), p['lb'], p['cnorm'], tabs['mall'], tabs['masks'], tabs['ind_c'])
        x1, h, route = _outproj(xf, o_a.reshape(T, A_WIDTH), o_b.reshape(T, B_WIDTH), o_c.reshape(T, C_WIDTH),
                                p['wa'], p['wb'], p['wc'], p['ln2'], p['wr'], p['br'])
        blk_e, nact, ids, pos = _dispatch(route)
        y = _moe(blk_e, nact, ids, h, p['w1'], p['w3'], p['w2'])
        xf = _combine(pos, x1, route, y)
    return xf.reshape(B, L, D)


def kernel(x_prompt, x_sample, ln1, w_in, conv_w, a_log, dt_bias, a_norm, q_a_norm, w_qb, kv_a_norm, w_kvb, qn_nope,
           qn_rope, kn_nope, kn_rope, c_lower_bounds, c_norm, w_out, ln2, w_group, b_group, w_router, b_router,
           w1, w3, w2):
    lbs = jnp.cumsum(jax.nn.softmax(c_lower_bounds.astype(F32), axis=0), axis=0)
    lbs = lbs - lbs[0:1]
    params = [_layer_params(l, ln1, w_in, conv_w, a_log, dt_bias, a_norm, q_a_norm, w_qb, kv_a_norm, w_kvb, qn_nope,
                            qn_rope, kn_nope, kn_rope, lbs, c_norm, w_out, ln2, w_group, b_group, w_router, b_router,
                            w1, w3, w2) for l in range(DEPTH)]
    tabs = _static_tables()
    assert x_prompt.shape[1:] == x_sample.shape[1:]
    rope = _rope_tables(x_prompt.shape[1])
    nb = x_prompt.shape[0]
    y = _trunk(jnp.concatenate([x_prompt, x_sample], axis=0), params, tabs, rope)
    return (y[:nb], y[nb:])
```

```python
import functools
import math

import numpy as np
import jax
import jax.numpy as jnp
from jax import lax
from jax.experimental import pallas as pl
from jax.experimental.pallas import tpu as pltpu

F32 = jnp.float32
BF16 = jnp.bfloat16

D_MODEL = 1024
DEPTH = 2
EPS = 1e-6
A_HEADS, A_HEAD_DIM, CONV_K = 6, 64, 5
A_WIDTH = A_HEADS * A_HEAD_DIM
B_HEADS, B_Q_RANK, B_KV_RANK, B_NOPE, B_ROPE, B_V = 6, 256, 128, 64, 32, 64
B_WIDTH = B_HEADS * B_V
ROPE_THETA = 10000.0
C_HEADS, C_EXPAND, C_HEAD_DIM = 4, 128, 64
C_WIDTH = C_HEADS * C_HEAD_DIM
C_FDIM = C_HEADS * C_EXPAND
D_MIX = A_WIDTH + B_WIDTH + C_WIDTH
A_IN = 4 * A_WIDTH + 4 * A_HEADS
B_IN = B_Q_RANK + B_KV_RANK + B_ROPE
C_IN = 3 * C_FDIM + 2 * C_WIDTH
N_GROUPS, EXPERTS_PER_GROUP, TOP_K, D_EXPERT = 4, 8, 2, 512
N_EXPERTS = N_GROUPS * EXPERTS_PER_GROUP

LANE = 128
CHUNK = 64
A_PAD = 4 * A_WIDTH + LANE
B_PAD = 512
C_PAD = C_IN
N_PAD = A_PAD + B_PAD + C_PAD
HEAD_PAD = LANE
MOE_ROWS = 256
TOK_TILE = 256
VMEM_LIMIT = 56 * 1024 * 1024


def _cparams(sem):
    return pltpu.CompilerParams(dimension_semantics=sem, vmem_limit_bytes=VMEM_LIMIT)


def _dot(a, b):
    return jnp.dot(a.astype(BF16), b.astype(BF16), preferred_element_type=F32)


def _dot_nt(a, b):
    return lax.dot_general(a.astype(BF16), b.astype(BF16), (((1,), (1,)), ((), ())), preferred_element_type=F32)


def _dot_tn(a, b):
    return lax.dot_general(a.astype(BF16), b.astype(BF16), (((0,), (0,)), ((), ())), preferred_element_type=F32)


def _split(x):
    hi = x.astype(BF16)
    lo = (x - hi.astype(F32)).astype(BF16)
    return hi, lo


def _sel_dot(sel, x):
    hi, lo = _split(x)
    return jnp.dot(sel, hi, preferred_element_type=F32) + jnp.dot(sel, lo, preferred_element_type=F32)


def _sel2_dot(sel2, x):
    hi, lo = _split(x)
    return jnp.dot(sel2, jnp.concatenate([hi, lo], axis=0), preferred_element_type=F32)


def _dot_sel(x, sel):
    hi, lo = _split(x)
    return jnp.dot(hi, sel, preferred_element_type=F32) + jnp.dot(lo, sel, preferred_element_type=F32)


def _sigmoid(x):
    return 1.0 / (1.0 + jnp.exp(-x))


def _silu(x):
    return x * _sigmoid(x)


def _softplus(x):
    return jnp.maximum(x, 0.0) + jnp.log(1.0 + jnp.exp(-jnp.abs(x)))


def _rms(x, w):
    return x * lax.rsqrt(jnp.mean(x * x, axis=-1, keepdims=True) + EPS) * w


def _full(shape):
    return pl.BlockSpec(shape, lambda *_: (0,) * len(shape))


def _proj_kernel(x_ref, ln_ref, w_ref, a_ref, b_ref, c_ref):
    h = _rms(x_ref[...], ln_ref[...]).astype(BF16)
    a_ref[...] = jnp.dot(h, w_ref[:, 0:A_PAD], preferred_element_type=F32)
    b_ref[...] = jnp.dot(h, w_ref[:, A_PAD:A_PAD + B_PAD], preferred_element_type=F32)
    c_ref[...] = jnp.dot(h, w_ref[:, A_PAD + B_PAD:N_PAD], preferred_element_type=F32)


def _proj(x, ln, w_pad):
    T = x.shape[0]
    tm = TOK_TILE
    return pl.pallas_call(
        _proj_kernel,
        grid=(T // tm,),
        in_specs=[pl.BlockSpec((tm, D_MODEL), lambda i: (i, 0)), _full((1, D_MODEL)), _full((D_MODEL, N_PAD))],
        out_specs=[pl.BlockSpec((tm, A_PAD), lambda i: (i, 0)), pl.BlockSpec((tm, B_PAD), lambda i: (i, 0)),
                   pl.BlockSpec((tm, C_PAD), lambda i: (i, 0))],
        out_shape=[jax.ShapeDtypeStruct((T, A_PAD), F32), jax.ShapeDtypeStruct((T, B_PAD), F32),
                   jax.ShapeDtypeStruct((T, C_PAD), F32)],
        compiler_params=_cparams(("parallel",)),
        name="proj",
    )(x, ln, w_pad)


def _gdn_kernel(qkv_ref, gate_ref, gates_ref, convw_ref, alog_ref, dtb_ref, anorm_ref, ind_ref, expb_ref, expg_ref,
                pick_ref, out_ref, q_s, k_s, v_s, of_s, ob_s, s_s, *, L):
    nc = L // CHUNK
    W = A_WIDTH
    row = lax.broadcasted_iota(jnp.int32, (CHUNK, CHUNK), 0)
    col = lax.broadcasted_iota(jnp.int32, (CHUNK, CHUNK), 1)
    lane = lax.broadcasted_iota(jnp.int32, (CHUNK, LANE), 1)
    row2 = lax.broadcasted_iota(jnp.int32, (CHUNK, 2 * CHUNK), 0)
    col2 = lax.broadcasted_iota(jnp.int32, (CHUNK, 2 * CHUNK), 1) % CHUNK
    ind = ind_ref[...]

    def conv_chunk(c, carry):
        t0 = pl.multiple_of(c * CHUNK, CHUNK)
        tp = pl.multiple_of(jnp.maximum(t0 - 8, 0), 8)
        tn = pl.multiple_of(jnp.minimum(t0 + CHUNK, L - 8), 8)
        for g in range(3):
            cs = slice(g * W, (g + 1) * W)
            main = qkv_ref[pl.ds(t0, CHUNK), cs]
            prev = jnp.where(c > 0, qkv_ref[pl.ds(tp, 8), cs], 0.0)
            nxt = jnp.where(c < nc - 1, qkv_ref[pl.ds(tn, 8), cs], 0.0)
            win = jnp.concatenate([prev, main, nxt], axis=0)
            acc = win * convw_ref[2:3, cs]
            for j in (0, 1, 3, 4):
                acc = acc + pltpu.roll(win, (2 - j) % (CHUNK + 16), 0) * convw_ref[j:j + 1, cs]
            y = _silu(acc[8:8 + CHUNK])
            if g < 2:
                y = y * lax.rsqrt(_dot_sel(y * y, ind) + EPS)
            if g == 0:
                q_s[pl.ds(t0, CHUNK), :] = (y * (A_HEAD_DIM ** -0.5)).astype(BF16)
            elif g == 1:
                k_s[pl.ds(t0, CHUNK), :] = y.astype(BF16)
            else:
                v_s[pl.ds(t0, CHUNK), :] = y
        return carry

    lax.fori_loop(0, nc, conv_chunk, 0)

    coef = -jnp.exp(alog_ref[...])
    dtb = dtb_ref[...]
    g_lane = (lane >= 2 * A_HEADS) & (lane < 4 * A_HEADS)

    def wide(c, d):
        t0 = pl.multiple_of(c * CHUNK, CHUNK)
        qc = q_s[pl.ds(t0, CHUNK), :].astype(F32)
        kc = k_s[pl.ds(t0, CHUNK), :].astype(F32)
        vc = v_s[pl.ds(t0, CHUNK), :]
        gt = gates_ref[pl.ds(t0, CHUNK), :]
        beta = _sigmoid(gt)
        gl = jnp.where(g_lane, coef * _softplus(gt + dtb), 0.0)
        tri2 = ((col2 <= row2) if d == 0 else (col2 >= row2)).astype(BF16)
        gcum = _sel2_dot(tri2, gl)
        total = gcum[CHUNK - 1:CHUNK] if d == 0 else gcum[0:1]
        e_in = jnp.exp(gcum)
        e_out = jnp.exp(total - gcum)
        g_last = jnp.broadcast_to(jnp.exp(total), (8, LANE))
        expb = expb_ref[d]
        expg = expg_ref[d]
        beta_w = jnp.dot(beta.astype(BF16), expb, preferred_element_type=F32)
        ein_w = jnp.dot(e_in.astype(BF16), expg, preferred_element_type=F32)
        eout_w = jnp.dot(e_out.astype(BF16), expg, preferred_element_type=F32)
        chi, clo = _split(gcum)
        nt = (((1,), (1,)), ((), ()))
        pick = pick_ref[d]
        kb = kc * beta_w
        return dict(
            t0=t0, q=qc, k=kc, kb=kb, vb=vc * beta_w, kbe=kb * ein_w, qe=qc * ein_w, kd=kc * eout_w,
            glast=_dot_sel(g_last, expg),
            gcol=_dot_sel(gcum, expg),
            grow=(lax.dot_general(pick, chi, nt, preferred_element_type=F32)
                  + lax.dot_general(pick, clo, nt, preferred_element_type=F32)),
            incl=(col <= row) if d == 0 else (col >= row),
            strict=(col < row) if d == 0 else (col > row))

    def scan_pair(i, carry):
        wd = [wide(i, 0), wide(nc - 1 - i, 1)]
        ks = [(d, h) for d in (0, 1) for h in range(A_HEADS)]
        hs = lambda h: slice(h * A_HEAD_DIM, (h + 1) * A_HEAD_DIM)
        dec, n, qk, x = {}, {}, {}, {}
        for d, h in ks:
            w = wd[d]
            dec[d, h] = jnp.exp(jnp.minimum(w['gcol'][:, hs(h)] - w['grow'][h * CHUNK:(h + 1) * CHUNK], 0.0))
        for d, h in ks:
            w = wd[d]
            n[d, h] = jnp.where(w['strict'], _dot_nt(w['kb'][:, hs(h)], w['k'][:, hs(h)]) * dec[d, h], 0.0)
        for d, h in ks:
            w = wd[d]
            qk[d, h] = jnp.where(w['incl'], _dot_nt(w['q'][:, hs(h)], w['k'][:, hs(h)]) * dec[d, h], 0.0)
            x[d, h] = jnp.concatenate([w['vb'][:, hs(h)], w['kbe'][:, hs(h)]], axis=1)
        xw = 2 * A_HEAD_DIM
        z = {k: _dot(n[k], jnp.concatenate([x[k], n[k]], axis=1)) for k in ks}
        x = {k: x[k] - z[k][:, :xw] for k in ks}
        for _ in range(4):
            p = {k: z[k][:, xw:] for k in ks}
            z = {k: _dot(p[k], jnp.concatenate([x[k], p[k]], axis=1)) for k in ks}
            x = {k: x[k] + z[k][:, :xw] for k in ks}
        x = {k: x[k] + _dot(z[k][:, xw:], x[k]) for k in ks}
        s = {(d, h): s_s[d * A_HEADS + h] for d, h in ks}
        v_new = {k: x[k][:, :A_HEAD_DIM] - _dot(x[k][:, A_HEAD_DIM:], s[k]) for k in ks}
        o = {(d, h): _dot(wd[d]['qe'][:, hs(h)], s[d, h]) + _dot(qk[d, h], v_new[d, h]) for d, h in ks}
        for d, h in ks:
            s_s[d * A_HEADS + h] = (s[d, h] * wd[d]['glast'][0:1, hs(h)]
                                    + _dot_tn(wd[d]['kd'][:, hs(h)], v_new[d, h]))
        of_s[pl.ds(wd[0]['t0'], CHUNK), :] = jnp.concatenate([o[0, h] for h in range(A_HEADS)], axis=1)
        ob_s[pl.ds(wd[1]['t0'], CHUNK), :] = jnp.concatenate([o[1, h] for h in range(A_HEADS)], axis=1)
        return carry

    s_s[...] = jnp.zeros_like(s_s)
    lax.fori_loop(0, nc, scan_pair, 0)

    rows = 256

    def finish(i, carry):
        t0 = pl.multiple_of(i * rows, rows)
        o = of_s[pl.ds(t0, rows), :] + ob_s[pl.ds(t0, rows), :]
        ms = _dot_sel(o * o, ind) * (1.0 / A_HEAD_DIM)
        o = o * lax.rsqrt(ms + EPS) * anorm_ref[...]
        out_ref[pl.ds(t0, rows), :] = o * _silu(gate_ref[pl.ds(t0, rows), :])
        return carry

    lax.fori_loop(0, L // rows, finish, 0)


def _gdn(a_in, convw, alog, dtb, anorm, ind, expb, expg, pick):
    B, L, _ = a_in.shape
    W = A_WIDTH
    kern = functools.partial(_gdn_kernel, L=L)
    one = pl.Buffered(1)
    return pl.pallas_call(
        kern,
        grid=(B,),
        in_specs=[pl.BlockSpec((None, L, 3 * W), lambda b: (b, 0, 0), pipeline_mode=one),
                  pl.BlockSpec((None, L, W), lambda b: (b, 0, 3), pipeline_mode=one),
                  pl.BlockSpec((None, L, LANE), lambda b: (b, 0, 4 * W // LANE), pipeline_mode=one),
                  _full((8, 3 * W)), _full((1, LANE)), _full((1, LANE)), _full((1, W)), _full((W, W)),
                  _full((2, LANE, W)), _full((2, LANE, W)), _full((2, A_HEADS * CHUNK, LANE))],
        out_specs=pl.BlockSpec((None, L, W), lambda b: (b, 0, 0)),
        out_shape=jax.ShapeDtypeStruct((B, L, W), F32),
        scratch_shapes=[pltpu.VMEM((L, W), BF16), pltpu.VMEM((L, W), BF16), pltpu.VMEM((L, W), F32),
                        pltpu.VMEM((L, W), F32), pltpu.VMEM((L, W), F32),
                        pltpu.VMEM((2 * A_HEADS, A_HEAD_DIM, A_HEAD_DIM), F32)],
        compiler_params=_cparams(("parallel",)),
        name="gdn",
    )(a_in, a_in, a_in, convw, alog, dtb, anorm, ind, expb, expg, pick)


def _mla_prep_kernel(b_ref, tab_ref, qan_ref, kvan_ref, wqb_ref, wk_ref, wv_ref, qw_ref, kw_ref, knr_ref,
                     indq_ref, invn_ref, place_ref, q_out, k_out, v_out):
    b = b_ref[...]
    tab = tab_ref[...]
    cq = _rms(b[:, :B_Q_RANK], qan_ref[...])
    ckv = _rms(b[:, B_Q_RANK:B_Q_RANK + B_KV_RANK], kvan_ref[...])
    kr = b[:, B_Q_RANK + B_KV_RANK:]
    indq = indq_ref[...]
    invn = invn_ref[...]

    q = _dot(cq, wqb_ref[...])
    qn = q * lax.rsqrt(_dot_sel(q * q, indq) * invn + EPS) * qw_ref[...]
    scale = (B_NOPE + B_ROPE) ** -0.5
    pieces = []
    for h in range(B_HEADS):
        x = qn[:, h * HEAD_PAD:(h + 1) * HEAD_PAD]
        half = B_ROPE // 2
        r = (x * tab[:, 0:LANE] + pltpu.roll(x, LANE - half, 1) * tab[:, LANE:2 * LANE]
             + pltpu.roll(x, half, 1) * tab[:, 2 * LANE:3 * LANE])
        pieces.append(r * scale)
    q_out[...] = jnp.concatenate(pieces, axis=1).astype(BF16)

    kk = _dot(ckv, wk_ref[...])
    kn = kk * lax.rsqrt(_dot_sel(kk * kk, indq) * invn + EPS) * kw_ref[...]
    krn = kr * lax.rsqrt(jnp.sum(kr * kr, axis=-1, keepdims=True) * (1.0 / B_ROPE) + EPS) * knr_ref[...]
    half = B_ROPE // 2
    krr = (krn * tab[:, 3 * LANE:4 * LANE] + pltpu.roll(krn, LANE - half, 1) * tab[:, 4 * LANE:5 * LANE]
           + pltpu.roll(krn, half, 1) * tab[:, 5 * LANE:6 * LANE])
    k_out[...] = (kn + jnp.dot(krr.astype(BF16), place_ref[...], preferred_element_type=F32)).astype(BF16)
    v_out[...] = _dot(ckv, wv_ref[...]).astype(BF16)


def _mla_prep(b_in, tab, qan, kvan, wqb, wk, wv, qw, kw, knr, indq, invn, place):
    B, L, _ = b_in.shape
    tm = TOK_TILE
    HW = B_HEADS * HEAD_PAD
    return pl.pallas_call(
        _mla_prep_kernel,
        grid=(B, L // tm),
        in_specs=[pl.BlockSpec((None, tm, B_PAD), lambda b, j: (b, j, 0)),
                  pl.BlockSpec((tm, 6 * LANE), lambda b, j: (j, 0)),
                  _full((1, B_Q_RANK)), _full((1, B_KV_RANK)), _full((B_Q_RANK, HW)), _full((B_KV_RANK, HW)),
                  _full((B_KV_RANK, B_WIDTH)), _full((1, HW)), _full((1, HW)), _full((1, LANE)),
                  _full((HW, HW)), _full((1, HW)), _full((LANE, HW))],
        out_specs=[pl.BlockSpec((None, tm, HW), lambda b, j: (b, j, 0)),
                   pl.BlockSpec((None, tm, HW), lambda b, j: (b, j, 0)),
                   pl.BlockSpec((None, tm, B_WIDTH), lambda b, j: (b, j, 0))],
        out_shape=[jax.ShapeDtypeStruct((B, L, HW), BF16), jax.ShapeDtypeStruct((B, L, HW), BF16),
                   jax.ShapeDtypeStruct((B, L, B_WIDTH), BF16)],
        compiler_params=_cparams(("parallel", "parallel")),
        name="mla_prep",
    )(b_in, tab, qan, kvan, wqb, wk, wv, qw, kw, knr, indq, invn, place)


def _attn_kernel(q_ref, k_ref, v_ref, o_ref):
    v = v_ref[...]
    scores = []
    for hh in range(2):
        q = q_ref[:, hh * HEAD_PAD:(hh + 1) * HEAD_PAD]
        k = k_ref[:, hh * HEAD_PAD:(hh + 1) * HEAD_PAD]
        scores.append(lax.dot_general(q, k, (((1,), (1,)), ((), ())), preferred_element_type=F32))
    outs = []
    for s in scores:
        p = jnp.exp(s - jnp.max(s, axis=-1, keepdims=True))
        den = jnp.sum(p, axis=-1, keepdims=True)
        outs.append(jnp.dot(p.astype(BF16), v, preferred_element_type=F32) / den)
    lane = lax.broadcasted_iota(jnp.int32, outs[0].shape, 1)
    o_ref[...] = jnp.where(lane < B_V, outs[0], outs[1])


def _attn(q, k, v):
    B, L, _ = q.shape
    tq = 256
    return pl.pallas_call(
        _attn_kernel,
        grid=(B, B_HEADS // 2, L // tq),
        in_specs=[pl.BlockSpec((None, tq, 2 * HEAD_PAD), lambda b, p, j: (b, j, p)),
                  pl.BlockSpec((None, L, 2 * HEAD_PAD), lambda b, p, j: (b, 0, p)),
                  pl.BlockSpec((None, L, 2 * B_V), lambda b, p, j: (b, 0, p))],
        out_specs=pl.BlockSpec((None, tq, 2 * B_V), lambda b, p, j: (b, j, p)),
        out_shape=jax.ShapeDtypeStruct((B, L, B_WIDTH), F32),
        compiler_params=_cparams(("parallel", "parallel", "arbitrary")),
        name="mla_attn",
    )(q, k, v)


def _hgrn_tables():
    C = CHUNK
    m_all = np.zeros((8 * C, C), np.float32)
    masks = np.zeros((7, C, C), np.float32)
    for lvl in range(6):
        m = 1 << lvl
        for t in range(C):
            bd = (t // (2 * m)) * 2 * m + m - 1
            if t % (2 * m) >= m:
                m_all[lvl * C + t, bd + 1:t + 1] = 1.0
            else:
                m_all[lvl * C + t, t + 1:bd + 1] = 1.0
            for s in range(C):
                if t // (2 * m) == s // (2 * m) and t % (2 * m) >= m and s % (2 * m) < m:
                    masks[lvl, t, s] = 1.0
    for t in range(C):
        m_all[6 * C + t, :t + 1] = 1.0
        m_all[7 * C + t, t + 1:] = 1.0
    masks[6] = np.eye(C, dtype=np.float32)
    m_f = m_all.reshape(8, C, C)
    m_b = m_f[:, ::-1, ::-1]
    mk_b = masks[:, ::-1, ::-1]
    m2 = np.stack([m_f.reshape(8 * C, C), m_b.reshape(8 * C, C)])
    return (np.concatenate([m2, m2], axis=2), np.stack([masks, mk_b]))


def _hgrn_kernel(q_ref, ff_ref, fb_ref, i_ref, gate_ref, lb_ref, cnorm_ref, mall_ref, masks_ref, ind_ref,
                 out_ref, of_s, ob_s, s_s, *, L):
    nc = L // CHUNK
    C = CHUNK

    def wide(c, d):
        t0 = pl.multiple_of(c * C, C)
        q = _silu(q_ref[pl.ds(t0, C), :])
        fp = (ff_ref if d == 0 else fb_ref)[pl.ds(t0, C), :]
        lb = lb_ref[d:d + 1, :]
        lsig = jnp.minimum(fp, 0.0) - jnp.log(1.0 + jnp.exp(-jnp.abs(fp)))
        a = jnp.log(lb)
        bb = jnp.log(1.0 - lb) + lsig
        logf = jnp.maximum(a, bb) + jnp.log(1.0 + jnp.exp(-jnp.abs(a - bb)))
        kk = (1.0 - lb) * _sigmoid(-fp)
        e_all = jnp.exp(_sel2_dot(mall_ref[d], logf))
        return dict(t0=t0, q=q, k=kk, e=e_all, v=i_ref[pl.ds(t0, C), :])

    def scan_pair(i, carry):
        wd = [wide(i, 0), wide(nc - 1 - i, 1)]
        ks = [(d, h) for d in (0, 1) for h in range(C_HEADS)]
        ls = lambda h: slice(h * C_EXPAND, (h + 1) * C_EXPAND)
        vs = lambda h: slice(h * C_HEAD_DIM, (h + 1) * C_HEAD_DIM)
        qh = {(d, h): wd[d]['q'][:, ls(h)] for d, h in ks}
        kh = {(d, h): wd[d]['k'][:, ls(h)] for d, h in ks}
        vh = {(d, h): wd[d]['v'][:, vs(h)] for d, h in ks}
        lvl_e = lambda d, h, j: wd[d]['e'][j * C:(j + 1) * C, ls(h)]
        attn = {(d, h): _dot_nt(qh[d, h], kh[d, h]) * masks_ref[d, 6] for d, h in ks}
        for lvl in range(6):
            for d, h in ks:
                e = lvl_e(d, h, lvl)
                attn[d, h] = attn[d, h] + _dot_nt(qh[d, h] * e, kh[d, h] * e) * masks_ref[d, lvl]
        st = {(d, h): s_s[d * C_HEADS + h] for d, h in ks}
        o = {}
        for d, h in ks:
            o[d, h] = _dot(attn[d, h], vh[d, h]) + _dot_nt(qh[d, h] * lvl_e(d, h, 6), st[d, h])
        for d, h in ks:
            e_in = lvl_e(d, h, 6)
            f_last = e_in[C - 1:C] if d == 0 else e_in[0:1]
            s_s[d * C_HEADS + h] = st[d, h] * f_last + _dot_tn(vh[d, h], kh[d, h] * lvl_e(d, h, 7))
        of_s[pl.ds(wd[0]['t0'], C), :] = jnp.concatenate([o[0, h] for h in range(C_HEADS)], axis=1)
        ob_s[pl.ds(wd[1]['t0'], C), :] = jnp.concatenate([o[1, h] for h in range(C_HEADS)], axis=1)
        return carry

    s_s[...] = jnp.zeros_like(s_s)
    lax.fori_loop(0, nc, scan_pair, 0)

    rows = 256
    ind = ind_ref[...]

    def finish(i, carry):
        t0 = pl.multiple_of(i * rows, rows)
        o = of_s[pl.ds(t0, rows), :] + ob_s[pl.ds(t0, rows), :]
        ms = _dot_sel(o * o, ind) * (1.0 / C_HEAD_DIM)
        o = o * lax.rsqrt(ms + EPS) * cnorm_ref[...]
        out_ref[pl.ds(t0, rows), :] = o * _silu(gate_ref[pl.ds(t0, rows), :])
        return carry

    lax.fori_loop(0, L // rows, finish, 0)


def _hgrn(c_in, lb, cnorm, mall, masks, ind):
    B, L, _ = c_in.shape
    kern = functools.partial(_hgrn_kernel, L=L)
    one = pl.Buffered(1)
    F, W = C_FDIM, C_WIDTH
    return pl.pallas_call(
        kern,
        grid=(B,),
        in_specs=[pl.BlockSpec((None, L, F), lambda b: (b, 0, 0), pipeline_mode=one),
                  pl.BlockSpec((None, L, F), lambda b: (b, 0, 1), pipeline_mode=one),
                  pl.BlockSpec((None, L, F), lambda b: (b, 0, 2), pipeline_mode=one),
                  pl.BlockSpec((None, L, W), lambda b: (b, 0, 3 * F // W), pipeline_mode=one),
                  pl.BlockSpec((None, L, W), lambda b: (b, 0, 3 * F // W + 1), pipeline_mode=one),
                  _full((2, F)), _full((1, W)), _full((2, 8 * CHUNK, 2 * CHUNK)), _full((2, 7, CHUNK, CHUNK)),
                  _full((W, W))],
        out_specs=pl.BlockSpec((None, L, W), lambda b: (b, 0, 0)),
        out_shape=jax.ShapeDtypeStruct((B, L, W), F32),
        scratch_shapes=[pltpu.VMEM((L, W), F32), pltpu.VMEM((L, W), F32),
                        pltpu.VMEM((2 * C_HEADS, C_HEAD_DIM, C_EXPAND), F32)],
        compiler_params=_cparams(("parallel",)),
        name="hgrn2",
    )(c_in, c_in, c_in, c_in, c_in, lb, cnorm, mall, masks, ind)


def _outproj_kernel(x_ref, oa_ref, ob_ref, oc_ref, wa_ref, wb_ref, wc_ref, ln_ref, wr_ref, br_ref, tri_ref,
                    x1_ref, h_ref, route_ref, cnt_ref, cnt_s):
    x1 = x_ref[...] + _dot(oa_ref[...], wa_ref[...]) + _dot(ob_ref[...], wb_ref[...]) + _dot(oc_ref[...], wc_ref[...])
    x1_ref[...] = x1
    h = _rms(x1, ln_ref[...])
    h_ref[...] = h
    hh, hl = _split(h)
    wr = wr_ref[...]
    wh, wl = _split(wr)
    lg = (jnp.dot(hh, wh, preferred_element_type=F32) + jnp.dot(hh, wl, preferred_element_type=F32)
          + jnp.dot(hl, wh, preferred_element_type=F32)) + br_ref[...]
    lane = lax.broadcasted_iota(jnp.int32, lg.shape, 1).astype(F32)
    neg = -1e30
    far = 1e9
    gmask = lane < N_GROUPS
    gl = jnp.where(gmask, lg, neg)
    gmax = jnp.max(gl, axis=-1, keepdims=True)
    g_idx = jnp.min(jnp.where(gl == gmax, lane, far), axis=-1, keepdims=True)
    g_w = 1.0 / jnp.sum(jnp.where(gmask, jnp.exp(gl - gmax), 0.0), axis=-1, keepdims=True)
    e_lane = lane - N_GROUPS
    emask = (e_lane >= g_idx * EXPERTS_PER_GROUP) & (e_lane < (g_idx + 1.0) * EXPERTS_PER_GROUP)
    el = jnp.where(emask, lg, neg)
    m1 = jnp.max(el, axis=-1, keepdims=True)
    i1 = jnp.min(jnp.where(el == m1, lane, far), axis=-1, keepdims=True)
    el2 = jnp.where(lane == i1, neg, el)
    m2 = jnp.max(el2, axis=-1, keepdims=True)
    i2 = jnp.min(jnp.where((el2 == m2) & emask & (lane != i1), lane, far), axis=-1, keepdims=True)
    r = jnp.exp(m2 - m1)
    w1 = g_w / (1.0 + r)
    w2 = g_w * r / (1.0 + r)
    @pl.when(pl.program_id(0) == 0)
    def _():
        cnt_s[...] = jnp.zeros_like(cnt_s)

    oh1 = jnp.where(lane == i1, 1.0, 0.0)
    oh2 = jnp.where(lane == i2, 1.0, 0.0)
    tri = tri_ref[...]
    run = cnt_s[0:1, :]
    tot1 = jnp.sum(oh1, axis=0, keepdims=True)
    c1 = run + jnp.dot(tri, oh1.astype(BF16), preferred_element_type=F32)
    c2 = run + tot1 + jnp.dot(tri, oh2.astype(BF16), preferred_element_type=F32)
    r1 = jnp.sum(oh1 * c1, axis=-1, keepdims=True)
    r2 = jnp.sum(oh2 * c2, axis=-1, keepdims=True)
    cnt = jnp.broadcast_to(run + tot1 + jnp.sum(oh2, axis=0, keepdims=True), cnt_s.shape)
    cnt_s[...] = cnt
    cnt_ref[...] = cnt
    vals = (i1 - N_GROUPS, i2 - N_GROUPS, w1, w2, r1, r2)
    route = jnp.zeros_like(lg)
    for j, v in enumerate(vals):
        route = jnp.where(lane == float(j), v, route)
    route_ref[...] = route


def _outproj(x, oa, ob, oc, wa, wb, wc, ln, wr, br, tri):
    T = x.shape[0]
    tm = TOK_TILE
    row = lambda w: pl.BlockSpec((tm, w), lambda i: (i, 0))
    return pl.pallas_call(
        _outproj_kernel,
        grid=(T // tm,),
        in_specs=[row(D_MODEL), row(A_WIDTH), row(B_WIDTH), row(C_WIDTH),
                  _full((A_WIDTH, D_MODEL)), _full((B_WIDTH, D_MODEL)), _full((C_WIDTH, D_MODEL)),
                  _full((1, D_MODEL)), _full((D_MODEL, LANE)), _full((1, LANE)), _full((tm, tm))],
        out_specs=[row(D_MODEL), row(D_MODEL), row(LANE), _full((8, LANE))],
        out_shape=[jax.ShapeDtypeStruct((T, D_MODEL), F32), jax.ShapeDtypeStruct((T, D_MODEL), F32),
                   jax.ShapeDtypeStruct((T, LANE), F32), jax.ShapeDtypeStruct((8, LANE), F32)],
        scratch_shapes=[pltpu.VMEM((8, LANE), F32)],
        compiler_params=_cparams(("arbitrary",)),
        name="outproj_router",
    )(x, oa, ob, oc, wa, wb, wc, ln, wr, br, tri)


def _gather_rows(src_hbm, buf, sem, cur_ids, nxt_ids, n_rows):
    i = pl.program_id(0)
    slot = i % 2

    def row_copy(ids, r, s):
        return pltpu.make_async_copy(src_hbm.at[pl.ds(ids[0, 0, r], 1)], buf.at[s, pl.ds(r, 1)], sem.at[s])

    def start_all(ids, s):
        def body(r, carry):
            row_copy(ids, r, s).start()
            return carry
        lax.fori_loop(0, n_rows, body, 0, unroll=8)

    def wait_all(s):
        pltpu.make_async_copy(src_hbm.at[pl.ds(0, n_rows)], buf.at[s], sem.at[s]).wait()

    @pl.when(i == 0)
    def _():
        start_all(cur_ids, 0)

    start_all(nxt_ids, 1 - slot)
    wait_all(slot)

    @pl.when(i == pl.num_programs(0) - 1)
    def _():
        wait_all(1 - slot)

    return slot


def _scatter_kernel(pos_ref, h_hbm, xs_in, xs_hbm, sem):
    del xs_in
    tm = TOK_TILE
    base = pl.program_id(0) * tm

    def row_copy(r, k):
        return pltpu.make_async_copy(h_hbm.at[pl.ds(base + r, 1)], xs_hbm.at[pl.ds(pos_ref[0, 0, k * tm + r], 1)], sem)

    def issue(r, carry):
        for k in range(TOP_K):
            row_copy(r, k).start()
        return carry

    lax.fori_loop(0, tm, issue, 0, unroll=8)
    n = TOP_K * tm
    pltpu.make_async_copy(h_hbm.at[pl.ds(0, n)], xs_hbm.at[pl.ds(0, n)], sem).wait()


def _scatter(pos, h, xs_init):
    T = h.shape[0]
    tm = TOK_TILE
    return pl.pallas_call(
        _scatter_kernel,
        grid=(T // tm,),
        in_specs=[pl.BlockSpec((1, 1, TOP_K * tm), lambda i: (i, 0, 0), memory_space=pltpu.SMEM),
                  pl.BlockSpec(memory_space=pl.ANY), pl.BlockSpec(memory_space=pl.ANY)],
        out_specs=pl.BlockSpec(memory_space=pl.ANY),
        out_shape=jax.ShapeDtypeStruct(xs_init.shape, F32),
        scratch_shapes=[pltpu.SemaphoreType.DMA(())],
        input_output_aliases={2: 0},
        compiler_params=_cparams(("arbitrary",)),
        name="moe_scatter",
    )(pos, h, xs_init)


def _moe_kernel(blk_e_ref, nact_ref, x_ref, w1_ref, w3_ref, w2_ref, y_ref):
    active = pl.program_id(0) < nact_ref[0]

    @pl.when(active)
    def _():
        x = x_ref[...].astype(BF16)
        a = jnp.dot(x, w1_ref[...], preferred_element_type=F32)
        g = jnp.dot(x, w3_ref[...], preferred_element_type=F32)
        y_ref[...] = jnp.dot((_silu(a) * g).astype(BF16), w2_ref[...], preferred_element_type=F32)

    @pl.when(jnp.logical_not(active))
    def _():
        y_ref[...] = jnp.zeros_like(y_ref)


def _moe(blk_e, nact, xs, w1, w3, w2):
    P = xs.shape[0]
    nblk = P // MOE_ROWS
    grid_spec = pltpu.PrefetchScalarGridSpec(
        num_scalar_prefetch=2,
        grid=(nblk,),
        in_specs=[pl.BlockSpec((MOE_ROWS, D_MODEL), lambda i, be, na: (i, 0)),
                  pl.BlockSpec((None, D_MODEL, D_EXPERT), lambda i, be, na: (be[i], 0, 0)),
                  pl.BlockSpec((None, D_MODEL, D_EXPERT), lambda i, be, na: (be[i], 0, 0)),
                  pl.BlockSpec((None, D_EXPERT, D_MODEL), lambda i, be, na: (be[i], 0, 0))],
        out_specs=pl.BlockSpec((MOE_ROWS, D_MODEL), lambda i, be, na: (i, 0)),
    )
    return pl.pallas_call(
        _moe_kernel,
        grid_spec=grid_spec,
        out_shape=jax.ShapeDtypeStruct((P, D_MODEL), F32),
        compiler_params=_cparams(("arbitrary",)),
        name="moe_ffn",
    )(blk_e, nact, xs, w1, w3, w2)


def _combine_kernel(pos_ref, nxt_ref, x_ref, route_ref, y_hbm, out_ref, yb, sem):
    tm = x_ref.shape[0]
    slot = _gather_rows(y_hbm, yb, sem, pos_ref, nxt_ref, TOP_K * tm)
    route = route_ref[...]
    out_ref[...] = x_ref[...] + yb[slot, 0:tm, :] * route[:, 2:3] + yb[slot, tm:2 * tm, :] * route[:, 3:4]


def _combine(pos, x1, route, y):
    T = x1.shape[0]
    tm = TOK_TILE
    nt = T // tm
    return pl.pallas_call(
        _combine_kernel,
        grid=(nt,),
        in_specs=[pl.BlockSpec((1, 1, TOP_K * tm), lambda i: (i, 0, 0), memory_space=pltpu.SMEM),
                  pl.BlockSpec((1, 1, TOP_K * tm), lambda i: (jnp.minimum(i + 1, nt - 1), 0, 0),
                               memory_space=pltpu.SMEM),
                  pl.BlockSpec((tm, D_MODEL), lambda i: (i, 0)),
                  pl.BlockSpec((tm, LANE), lambda i: (i, 0)),
                  pl.BlockSpec(memory_space=pl.ANY)],
        out_specs=pl.BlockSpec((tm, D_MODEL), lambda i: (i, 0)),
        out_shape=jax.ShapeDtypeStruct((T, D_MODEL), F32),
        scratch_shapes=[pltpu.VMEM((2, TOP_K * tm, D_MODEL), F32), pltpu.SemaphoreType.DMA((2,))],
        compiler_params=_cparams(("arbitrary",)),
        name="moe_combine",
    )(pos, pos, x1, route, y)


def _dispatch(route, cnt):
    T = route.shape[0]
    nblk = (T * TOP_K) // MOE_ROWS + N_EXPERTS
    counts = cnt[0, N_GROUPS:N_GROUPS + N_EXPERTS].astype(jnp.int32)
    padded = (counts + MOE_ROWS - 1) // MOE_ROWS * MOE_ROWS
    pends = jnp.cumsum(padded)
    pstarts = pends - padded
    e = route[:, 0:TOP_K].astype(jnp.int32)
    rank = route[:, 4:4 + TOP_K].astype(jnp.int32)
    sel = e[:, :, None] == jnp.arange(N_EXPERTS, dtype=jnp.int32)
    pos = rank + jnp.sum(jnp.where(sel, pstarts, 0), axis=-1)
    pos = jnp.transpose(pos.reshape(T // TOK_TILE, TOK_TILE, TOP_K), (0, 2, 1))
    pos = pos.reshape(T // TOK_TILE, 1, TOP_K * TOK_TILE)
    blk_start = jnp.arange(nblk, dtype=jnp.int32) * MOE_ROWS
    blk_e = jnp.minimum(jnp.sum(blk_start[:, None] >= pends[None, :], axis=-1), N_EXPERTS - 1).astype(jnp.int32)
    nact = (pends[-1:] // MOE_ROWS).astype(jnp.int32)
    return blk_e, nact, pos


def _block_ind(width, seg):
    i = np.arange(width)
    return (i[:, None] // seg == i[None, :] // seg).astype(np.float32)


def _rope_tables(L):
    half = B_ROPE // 2
    inv = ROPE_THETA ** (-jnp.arange(0, B_ROPE, 2, dtype=F32) / B_ROPE)
    ang = jnp.arange(L, dtype=F32)[:, None] * inv[None, :]
    cos, sin = jnp.cos(ang), jnp.sin(ang)
    z = lambda n: jnp.zeros((L, n), F32)
    one = jnp.ones((L, B_NOPE), F32)
    rest = LANE - B_NOPE - B_ROPE
    cq = jnp.concatenate([one, cos, cos, z(rest)], axis=1)
    s1q = jnp.concatenate([z(B_NOPE), -sin, z(half), z(rest)], axis=1)
    s2q = jnp.concatenate([z(B_NOPE), z(half), sin, z(rest)], axis=1)
    ck = jnp.concatenate([cos, cos, z(LANE - B_ROPE)], axis=1)
    s1k = jnp.concatenate([-sin, z(LANE - half)], axis=1)
    s2k = jnp.concatenate([z(half), sin, z(LANE - B_ROPE)], axis=1)
    return jnp.concatenate([cq, s1q, s2q, ck, s1k, s2k], axis=1)


def _layer_params(l, ln1, w_in, conv_w, a_log, dt_bias, a_norm, q_a_norm, w_qb, kv_a_norm, w_kvb, qn_nope, qn_rope,
                  kn_nope, kn_rope, lbs, c_norm, w_out, ln2, w_group, b_group, w_router, b_router, w1, w3, w2):
    p = {}
    w = w_in[l]
    wp = jnp.zeros((D_MODEL, N_PAD), F32)
    wp = wp.at[:, 0:A_IN].set(w[:, 0:A_IN])
    wp = wp.at[:, A_PAD:A_PAD + B_IN].set(w[:, A_IN:A_IN + B_IN])
    wp = wp.at[:, A_PAD + B_PAD:].set(w[:, A_IN + B_IN:])
    p['w_in'] = wp.astype(BF16)
    p['ln1'] = ln1[l].reshape(1, D_MODEL)
    p['convw'] = jnp.zeros((8, 3 * A_WIDTH), F32).at[0:CONV_K].set(conv_w[l])
    gpad = lambda v: jnp.zeros((1, LANE), F32).at[0, 2 * A_HEADS:4 * A_HEADS].set(v.reshape(-1))
    p['alog'] = gpad(a_log[l])
    p['dtb'] = gpad(dt_bias[l])
    p['anorm'] = jnp.tile(a_norm[l], A_HEADS).reshape(1, A_WIDTH)
    half = B_ROPE // 2
    wq = w_qb[l].reshape(B_Q_RANK, B_HEADS, B_NOPE + B_ROPE)
    wq = jnp.pad(wq, ((0, 0), (0, 0), (0, HEAD_PAD - B_NOPE - B_ROPE)))
    p['wqb'] = wq.reshape(B_Q_RANK, B_HEADS * HEAD_PAD).astype(BF16)
    wkv = w_kvb[l].reshape(B_KV_RANK, B_HEADS, B_NOPE + B_V)
    wk = jnp.pad(wkv[:, :, :B_NOPE], ((0, 0), (0, 0), (0, HEAD_PAD - B_NOPE)))
    p['wk'] = wk.reshape(B_KV_RANK, B_HEADS * HEAD_PAD).astype(BF16)
    p['wv'] = wkv[:, :, B_NOPE:].reshape(B_KV_RANK, B_WIDTH).astype(BF16)
    zpad = jnp.zeros((HEAD_PAD - B_NOPE - B_ROPE,), F32)
    p['qw'] = jnp.tile(jnp.concatenate([qn_nope[l], qn_rope[l], zpad]), B_HEADS).reshape(1, -1)
    p['kw'] = jnp.tile(jnp.concatenate([kn_nope[l], jnp.zeros((HEAD_PAD - B_NOPE,), F32)]), B_HEADS).reshape(1, -1)
    p['knr'] = jnp.concatenate([kn_rope[l], jnp.zeros((LANE - B_ROPE,), F32)]).reshape(1, LANE)
    p['qan'] = q_a_norm[l].reshape(1, -1)
    p['kvan'] = kv_a_norm[l].reshape(1, -1)
    p['lb'] = lbs[l]
    p['cnorm'] = jnp.tile(c_norm[l], C_HEADS).reshape(1, C_WIDTH)
    wo = w_out[l].astype(BF16)
    p['wa'], p['wb'], p['wc'] = wo[:A_WIDTH], wo[A_WIDTH:A_WIDTH + B_WIDTH], wo[A_WIDTH + B_WIDTH:]
    p['ln2'] = ln2[l].reshape(1, D_MODEL)
    wr = jnp.zeros((D_MODEL, LANE), F32).at[:, :N_GROUPS].set(w_group[l])
    p['wr'] = wr.at[:, N_GROUPS:N_GROUPS + N_EXPERTS].set(w_router[l])
    br = jnp.zeros((1, LANE), F32).at[0, :N_GROUPS].set(b_group[l])
    p['br'] = br.at[0, N_GROUPS:N_GROUPS + N_EXPERTS].set(b_router[l])
    p['w1'], p['w3'], p['w2'] = w1[l].astype(BF16), w3[l].astype(BF16), w2[l].astype(BF16)
    return p


def _static_tables():
    t = {}
    t['ind_a'] = jnp.asarray(_block_ind(A_WIDTH, A_HEAD_DIM), BF16)
    t['ind_c'] = jnp.asarray(_block_ind(C_WIDTH, C_HEAD_DIM), BF16)
    expb = np.zeros((2, LANE, A_WIDTH), np.float32)
    expg = np.zeros((2, LANE, A_WIDTH), np.float32)
    for d in range(2):
        for h in range(A_HEADS):
            expb[d, A_HEADS * d + h, h * A_HEAD_DIM:(h + 1) * A_HEAD_DIM] = 1.0
            expg[d, 2 * A_HEADS + A_HEADS * d + h, h * A_HEAD_DIM:(h + 1) * A_HEAD_DIM] = 1.0
    t['expb'] = jnp.asarray(expb, BF16)
    t['expg'] = jnp.asarray(expg, BF16)
    t['pick'] = jnp.asarray(np.repeat(np.transpose(expg[:, :, ::A_HEAD_DIM], (0, 2, 1)), CHUNK, axis=1), BF16)
    HW = B_HEADS * HEAD_PAD
    lane = np.arange(HW)
    seg = np.where(lane % HEAD_PAD < B_NOPE, 0, np.where(lane % HEAD_PAD < B_NOPE + B_ROPE, 1, 2))
    head = lane // HEAD_PAD
    indq = (head[:, None] == head[None, :]) & (seg[:, None] == seg[None, :]) & (seg[:, None] < 2)
    t['indq'] = jnp.asarray(indq.astype(np.float32), BF16)
    t['invn'] = jnp.asarray(np.where(seg == 0, 1.0 / B_NOPE, 1.0 / B_ROPE).astype(np.float32)).reshape(1, HW)
    place = np.zeros((LANE, HW), np.float32)
    for h in range(B_HEADS):
        for r in range(B_ROPE):
            place[r, h * HEAD_PAD + B_NOPE + r] = 1.0
    t['place'] = jnp.asarray(place, BF16)
    t['tri'] = jnp.asarray(np.tril(np.ones((TOK_TILE, TOK_TILE), np.float32), -1), BF16)
    mall, masks = _hgrn_tables()
    t['mall'] = jnp.asarray(mall, BF16)
    t['masks'] = jnp.asarray(masks, F32)
    return t


def _trunk(x, params, tabs, rope):
    B, L, D = x.shape
    T = B * L
    xf = x.reshape(T, D)
    for p in params:
        a_in, b_in, c_in = _proj(xf, p['ln1'], p['w_in'])
        o_a = _gdn(a_in.reshape(B, L, A_PAD), p['convw'], p['alog'], p['dtb'], p['anorm'], tabs['ind_a'],
                   tabs['expb'], tabs['expg'], tabs['pick'])
        q, k, v = _mla_prep(b_in.reshape(B, L, B_PAD), rope, p['qan'], p['kvan'], p['wqb'], p['wk'], p['wv'],
                            p['qw'], p['kw'], p['knr'], tabs['indq'], tabs['invn'], tabs['place'])
        o_b = _attn(q, k, v)
        o_c = _hgrn(c_in.reshape(B, L, C_PAD), p['lb'], p['cnorm'], tabs['mall'], tabs['masks'], tabs['ind_c'])
        x1, h, route, cnt = _outproj(xf, o_a.reshape(T, A_WIDTH), o_b.reshape(T, B_WIDTH), o_c.reshape(T, C_WIDTH),
                                     p['wa'], p['wb'], p['wc'], p['ln2'], p['wr'], p['br'], tabs['tri'])
        blk_e, nact, pos = _dispatch(route, cnt)
        xs = _scatter(pos, h, jnp.zeros((T * TOP_K + N_EXPERTS * MOE_ROWS, D), F32))
        y = _moe(blk_e, nact, xs, p['w1'], p['w3'], p['w2'])
        xf = _combine(pos, x1, route, y)
    return xf.reshape(B, L, D)


def kernel(x_prompt, x_sample, ln1, w_in, conv_w, a_log, dt_bias, a_norm, q_a_norm, w_qb, kv_a_norm, w_kvb, qn_nope,
           qn_rope, kn_nope, kn_rope, c_lower_bounds, c_norm, w_out, ln2, w_group, b_group, w_router, b_router,
           w1, w3, w2):
    lbs = jnp.cumsum(jax.nn.softmax(c_lower_bounds.astype(F32), axis=0), axis=0)
    lbs = lbs - lbs[0:1]
    params = [_layer_params(l, ln1, w_in, conv_w, a_log, dt_bias, a_norm, q_a_norm, w_qb, kv_a_norm, w_kvb, qn_nope,
                            qn_rope, kn_nope, kn_rope, lbs, c_norm, w_out, ln2, w_group, b_group, w_router, b_router,
                            w1, w3, w2) for l in range(DEPTH)]
    tabs = _static_tables()
    assert x_prompt.shape[1:] == x_sample.shape[1:]
    rope = _rope_tables(x_prompt.shape[1])
    nb = x_prompt.shape[0]
    y = _trunk(jnp.concatenate([x_prompt, x_sample], axis=0), params, tabs, rope)
    return (y[:nb], y[nb:])
```

```python
import functools
import math

import numpy as np
import jax
import jax.numpy as jnp
from jax import lax
from jax.experimental import pallas as pl
from jax.experimental.pallas import tpu as pltpu

F32 = jnp.float32
BF16 = jnp.bfloat16

D_MODEL = 1024
DEPTH = 2
EPS = 1e-6
A_HEADS, A_HEAD_DIM, CONV_K = 6, 64, 5
A_WIDTH = A_HEADS * A_HEAD_DIM
B_HEADS, B_Q_RANK, B_KV_RANK, B_NOPE, B_ROPE, B_V = 6, 256, 128, 64, 32, 64
B_WIDTH = B_HEADS * B_V
ROPE_THETA = 10000.0
C_HEADS, C_EXPAND, C_HEAD_DIM = 4, 128, 64
C_WIDTH = C_HEADS * C_HEAD_DIM
C_FDIM = C_HEADS * C_EXPAND
D_MIX = A_WIDTH + B_WIDTH + C_WIDTH
A_IN = 4 * A_WIDTH + 4 * A_HEADS
B_IN = B_Q_RANK + B_KV_RANK + B_ROPE
C_IN = 3 * C_FDIM + 2 * C_WIDTH
N_GROUPS, EXPERTS_PER_GROUP, TOP_K, D_EXPERT = 4, 8, 2, 512
N_EXPERTS = N_GROUPS * EXPERTS_PER_GROUP

LANE = 128
CHUNK = 64
A_PAD = 4 * A_WIDTH + LANE
B_PAD = 512
C_PAD = C_IN
N_PAD = A_PAD + B_PAD + C_PAD
HEAD_PAD = LANE
MOE_ROWS = 256
TOK_TILE = 256
VMEM_LIMIT = 56 * 1024 * 1024


def _cparams(sem):
    return pltpu.CompilerParams(dimension_semantics=sem, vmem_limit_bytes=VMEM_LIMIT)


def _dot(a, b):
    return jnp.dot(a.astype(BF16), b.astype(BF16), preferred_element_type=F32)


def _dot_nt(a, b):
    return lax.dot_general(a.astype(BF16), b.astype(BF16), (((1,), (1,)), ((), ())), preferred_element_type=F32)


def _dot_tn(a, b):
    return lax.dot_general(a.astype(BF16), b.astype(BF16), (((0,), (0,)), ((), ())), preferred_element_type=F32)


def _split(x):
    hi = x.astype(BF16)
    lo = (x - hi.astype(F32)).astype(BF16)
    return hi, lo


def _sel_dot(sel, x):
    hi, lo = _split(x)
    return jnp.dot(sel, hi, preferred_element_type=F32) + jnp.dot(sel, lo, preferred_element_type=F32)


def _sel2_dot(sel2, x):
    hi, lo = _split(x)
    return jnp.dot(sel2, jnp.concatenate([hi, lo], axis=0), preferred_element_type=F32)


def _dot_sel(x, sel):
    hi, lo = _split(x)
    return jnp.dot(hi, sel, preferred_element_type=F32) + jnp.dot(lo, sel, preferred_element_type=F32)


def _sigmoid(x):
    return 1.0 / (1.0 + jnp.exp(-x))


def _silu(x):
    return x * _sigmoid(x)


def _softplus(x):
    return jnp.maximum(x, 0.0) + jnp.log(1.0 + jnp.exp(-jnp.abs(x)))


def _rms(x, w):
    return x * lax.rsqrt(jnp.mean(x * x, axis=-1, keepdims=True) + EPS) * w


def _full(shape):
    return pl.BlockSpec(shape, lambda *_: (0,) * len(shape))


def _proj_kernel(x_ref, ln_ref, w_ref, a_ref, b_ref, c_ref):
    h = _rms(x_ref[...], ln_ref[...]).astype(BF16)
    a_ref[...] = jnp.dot(h, w_ref[:, 0:A_PAD], preferred_element_type=F32)
    b_ref[...] = jnp.dot(h, w_ref[:, A_PAD:A_PAD + B_PAD], preferred_element_type=F32)
    c_ref[...] = jnp.dot(h, w_ref[:, A_PAD + B_PAD:N_PAD], preferred_element_type=F32)


def _proj(x, ln, w_pad):
    T = x.shape[0]
    tm = TOK_TILE
    return pl.pallas_call(
        _proj_kernel,
        grid=(T // tm,),
        in_specs=[pl.BlockSpec((tm, D_MODEL), lambda i: (i, 0)), _full((1, D_MODEL)), _full((D_MODEL, N_PAD))],
        out_specs=[pl.BlockSpec((tm, A_PAD), lambda i: (i, 0)), pl.BlockSpec((tm, B_PAD), lambda i: (i, 0)),
                   pl.BlockSpec((tm, C_PAD), lambda i: (i, 0))],
        out_shape=[jax.ShapeDtypeStruct((T, A_PAD), F32), jax.ShapeDtypeStruct((T, B_PAD), F32),
                   jax.ShapeDtypeStruct((T, C_PAD), F32)],
        compiler_params=_cparams(("parallel",)),
        name="proj",
    )(x, ln, w_pad)


def _gdn_kernel(qkv_ref, gate_ref, gates_ref, convw_ref, alog_ref, dtb_ref, anorm_ref, ind_ref, expb_ref, expg_ref,
                pick_ref, out_ref, q_s, k_s, v_s, of_s, ob_s, s_s, *, L):
    nc = L // CHUNK
    W = A_WIDTH
    row = lax.broadcasted_iota(jnp.int32, (CHUNK, CHUNK), 0)
    col = lax.broadcasted_iota(jnp.int32, (CHUNK, CHUNK), 1)
    lane = lax.broadcasted_iota(jnp.int32, (CHUNK, LANE), 1)
    row2 = lax.broadcasted_iota(jnp.int32, (CHUNK, 2 * CHUNK), 0)
    col2 = lax.broadcasted_iota(jnp.int32, (CHUNK, 2 * CHUNK), 1) % CHUNK
    ind = ind_ref[...]

    def conv_chunk(c, carry):
        t0 = pl.multiple_of(c * CHUNK, CHUNK)
        tp = pl.multiple_of(jnp.maximum(t0 - 8, 0), 8)
        tn = pl.multiple_of(jnp.minimum(t0 + CHUNK, L - 8), 8)
        for g in range(3):
            cs = slice(g * W, (g + 1) * W)
            main = qkv_ref[pl.ds(t0, CHUNK), cs]
            prev = jnp.where(c > 0, qkv_ref[pl.ds(tp, 8), cs], 0.0)
            nxt = jnp.where(c < nc - 1, qkv_ref[pl.ds(tn, 8), cs], 0.0)
            win = jnp.concatenate([prev, main, nxt], axis=0)
            acc = win * convw_ref[2:3, cs]
            for j in (0, 1, 3, 4):
                acc = acc + pltpu.roll(win, (2 - j) % (CHUNK + 16), 0) * convw_ref[j:j + 1, cs]
            y = _silu(acc[8:8 + CHUNK])
            if g < 2:
                y = y * lax.rsqrt(_dot_sel(y * y, ind) + EPS)
            if g == 0:
                q_s[pl.ds(t0, CHUNK), :] = (y * (A_HEAD_DIM ** -0.5)).astype(BF16)
            elif g == 1:
                k_s[pl.ds(t0, CHUNK), :] = y.astype(BF16)
            else:
                v_s[pl.ds(t0, CHUNK), :] = y
        return carry

    lax.fori_loop(0, nc, conv_chunk, 0)

    coef = -jnp.exp(alog_ref[...])
    dtb = dtb_ref[...]
    g_lane = (lane >= 2 * A_HEADS) & (lane < 4 * A_HEADS)

    def wide(c, d):
        t0 = pl.multiple_of(c * CHUNK, CHUNK)
        qc = q_s[pl.ds(t0, CHUNK), :].astype(F32)
        kc = k_s[pl.ds(t0, CHUNK), :].astype(F32)
        vc = v_s[pl.ds(t0, CHUNK), :]
        gt = gates_ref[pl.ds(t0, CHUNK), :]
        beta = _sigmoid(gt)
        gl = jnp.where(g_lane, coef * _softplus(gt + dtb), 0.0)
        tri2 = ((col2 <= row2) if d == 0 else (col2 >= row2)).astype(BF16)
        gcum = _sel2_dot(tri2, gl)
        total = gcum[CHUNK - 1:CHUNK] if d == 0 else gcum[0:1]
        e_in = jnp.exp(gcum)
        e_out = jnp.exp(total - gcum)
        g_last = jnp.broadcast_to(jnp.exp(total), (8, LANE))
        expb = expb_ref[d]
        expg = expg_ref[d]
        beta_w = jnp.dot(beta.astype(BF16), expb, preferred_element_type=F32)
        ein_w = jnp.dot(e_in.astype(BF16), expg, preferred_element_type=F32)
        eout_w = jnp.dot(e_out.astype(BF16), expg, preferred_element_type=F32)
        kb = kc * beta_w
        return dict(
            t0=t0, q=qc, k=kc, kb=kb, vb=vc * beta_w, kbe=kb * ein_w, qe=qc * ein_w, kd=kc * eout_w,
            glast=_dot_sel(g_last, expg),
            gcum=gcum, gcum_t=gcum.T,
            incl=(col <= row) if d == 0 else (col >= row),
            strict=(col < row) if d == 0 else (col > row))

    unroll = 2

    def scan_group(i, carry):
        wd = [[wide(unroll * i + u, 0), wide(nc - 1 - (unroll * i + u), 1)] for u in range(unroll)]
        dh = [(d, h) for d in (0, 1) for h in range(A_HEADS)]
        ks = [(u, d, h) for u in range(unroll) for d, h in dh]
        hs = lambda h: slice(h * A_HEAD_DIM, (h + 1) * A_HEAD_DIM)
        dec, n, qk, x = {}, {}, {}, {}
        for u, d, h in ks:
            w = wd[u][d]
            lg = 2 * A_HEADS + A_HEADS * d + h
            dec[u, d, h] = jnp.exp(jnp.minimum(w['gcum'][:, lg:lg + 1] - w['gcum_t'][lg:lg + 1, :], 0.0))
        for u, d, h in ks:
            w = wd[u][d]
            n[u, d, h] = jnp.where(w['strict'], _dot_nt(w['kb'][:, hs(h)], w['k'][:, hs(h)]) * dec[u, d, h], 0.0)
        for u, d, h in ks:
            w = wd[u][d]
            qk[u, d, h] = jnp.where(w['incl'], _dot_nt(w['q'][:, hs(h)], w['k'][:, hs(h)]) * dec[u, d, h], 0.0)
            x[u, d, h] = jnp.concatenate([w['vb'][:, hs(h)], w['kbe'][:, hs(h)]], axis=1)
        xw = 2 * A_HEAD_DIM
        z = {k: _dot(n[k], jnp.concatenate([x[k], n[k]], axis=1)) for k in ks}
        x = {k: x[k] - z[k][:, :xw] for k in ks}
        for _ in range(4):
            p = {k: z[k][:, xw:] for k in ks}
            z = {k: _dot(p[k], jnp.concatenate([x[k], p[k]], axis=1)) for k in ks}
            x = {k: x[k] + z[k][:, :xw] for k in ks}
        x = {k: x[k] + _dot(z[k][:, xw:], x[k]) for k in ks}
        s = {(d, h): s_s[d * A_HEADS + h] for d, h in dh}
        for u in range(unroll):
            v_new = {(d, h): x[u, d, h][:, :A_HEAD_DIM] - _dot(x[u, d, h][:, A_HEAD_DIM:], s[d, h]) for d, h in dh}
            o = {(d, h): _dot(wd[u][d]['qe'][:, hs(h)], s[d, h]) + _dot(qk[u, d, h], v_new[d, h]) for d, h in dh}
            s = {(d, h): (s[d, h] * wd[u][d]['glast'][0:1, hs(h)] + _dot_tn(wd[u][d]['kd'][:, hs(h)], v_new[d, h]))
                 for d, h in dh}
            of_s[pl.ds(wd[u][0]['t0'], CHUNK), :] = jnp.concatenate([o[0, h] for h in range(A_HEADS)], axis=1)
            ob_s[pl.ds(wd[u][1]['t0'], CHUNK), :] = jnp.concatenate([o[1, h] for h in range(A_HEADS)], axis=1)
        for d, h in dh:
            s_s[d * A_HEADS + h] = s[d, h]
        return carry

    s_s[...] = jnp.zeros_like(s_s)
    lax.fori_loop(0, nc // unroll, scan_group, 0)

    rows = 256

    def finish(i, carry):
        t0 = pl.multiple_of(i * rows, rows)
        o = of_s[pl.ds(t0, rows), :] + ob_s[pl.ds(t0, rows), :]
        ms = _dot_sel(o * o, ind) * (1.0 / A_HEAD_DIM)
        o = o * lax.rsqrt(ms + EPS) * anorm_ref[...]
        out_ref[pl.ds(t0, rows), :] = o * _silu(gate_ref[pl.ds(t0, rows), :])
        return carry

    lax.fori_loop(0, L // rows, finish, 0)


def _gdn(a_in, convw, alog, dtb, anorm, ind, expb, expg, pick):
    B, L, _ = a_in.shape
    W = A_WIDTH
    kern = functools.partial(_gdn_kernel, L=L)
    one = pl.Buffered(1)
    return pl.pallas_call(
        kern,
        grid=(B,),
        in_specs=[pl.BlockSpec((None, L, 3 * W), lambda b: (b, 0, 0), pipeline_mode=one),
                  pl.BlockSpec((None, L, W), lambda b: (b, 0, 3), pipeline_mode=one),
                  pl.BlockSpec((None, L, LANE), lambda b: (b, 0, 4 * W // LANE), pipeline_mode=one),
                  _full((8, 3 * W)), _full((1, LANE)), _full((1, LANE)), _full((1, W)), _full((W, W)),
                  _full((2, LANE, W)), _full((2, LANE, W)), _full((2, A_HEADS * CHUNK, LANE))],
        out_specs=pl.BlockSpec((None, L, W), lambda b: (b, 0, 0)),
        out_shape=jax.ShapeDtypeStruct((B, L, W), F32),
        scratch_shapes=[pltpu.VMEM((L, W), BF16), pltpu.VMEM((L, W), BF16), pltpu.VMEM((L, W), F32),
                        pltpu.VMEM((L, W), F32), pltpu.VMEM((L, W), F32),
                        pltpu.VMEM((2 * A_HEADS, A_HEAD_DIM, A_HEAD_DIM), F32)],
        compiler_params=_cparams(("parallel",)),
        name="gdn",
    )(a_in, a_in, a_in, convw, alog, dtb, anorm, ind, expb, expg, pick)


def _mla_prep_kernel(b_ref, tab_ref, qan_ref, kvan_ref, wqb_ref, wk_ref, wv_ref, qw_ref, kw_ref, knr_ref,
                     indq_ref, invn_ref, place_ref, q_out, k_out, v_out):
    b = b_ref[...]
    tab = tab_ref[...]
    cq = _rms(b[:, :B_Q_RANK], qan_ref[...])
    ckv = _rms(b[:, B_Q_RANK:B_Q_RANK + B_KV_RANK], kvan_ref[...])
    kr = b[:, B_Q_RANK + B_KV_RANK:]
    indq = indq_ref[...]
    invn = invn_ref[...]

    q = _dot(cq, wqb_ref[...])
    qn = q * lax.rsqrt(_dot_sel(q * q, indq) * invn + EPS) * qw_ref[...]
    scale = (B_NOPE + B_ROPE) ** -0.5
    pieces = []
    for h in range(B_HEADS):
        x = qn[:, h * HEAD_PAD:(h + 1) * HEAD_PAD]
        half = B_ROPE // 2
        r = (x * tab[:, 0:LANE] + pltpu.roll(x, LANE - half, 1) * tab[:, LANE:2 * LANE]
             + pltpu.roll(x, half, 1) * tab[:, 2 * LANE:3 * LANE])
        pieces.append(r * scale)
    q_out[...] = jnp.concatenate(pieces, axis=1).astype(BF16)

    kk = _dot(ckv, wk_ref[...])
    kn = kk * lax.rsqrt(_dot_sel(kk * kk, indq) * invn + EPS) * kw_ref[...]
    krn = kr * lax.rsqrt(jnp.sum(kr * kr, axis=-1, keepdims=True) * (1.0 / B_ROPE) + EPS) * knr_ref[...]
    half = B_ROPE // 2
    krr = (krn * tab[:, 3 * LANE:4 * LANE] + pltpu.roll(krn, LANE - half, 1) * tab[:, 4 * LANE:5 * LANE]
           + pltpu.roll(krn, half, 1) * tab[:, 5 * LANE:6 * LANE])
    k_out[...] = (kn + jnp.dot(krr.astype(BF16), place_ref[...], preferred_element_type=F32)).astype(BF16)
    v_out[...] = _dot(ckv, wv_ref[...]).astype(BF16)


def _mla_prep(b_in, tab, qan, kvan, wqb, wk, wv, qw, kw, knr, indq, invn, place):
    B, L, _ = b_in.shape
    tm = TOK_TILE
    HW = B_HEADS * HEAD_PAD
    return pl.pallas_call(
        _mla_prep_kernel,
        grid=(B, L // tm),
        in_specs=[pl.BlockSpec((None, tm, B_PAD), lambda b, j: (b, j, 0)),
                  pl.BlockSpec((tm, 6 * LANE), lambda b, j: (j, 0)),
                  _full((1, B_Q_RANK)), _full((1, B_KV_RANK)), _full((B_Q_RANK, HW)), _full((B_KV_RANK, HW)),
                  _full((B_KV_RANK, B_WIDTH)), _full((1, HW)), _full((1, HW)), _full((1, LANE)),
                  _full((HW, HW)), _full((1, HW)), _full((LANE, HW))],
        out_specs=[pl.BlockSpec((None, tm, HW), lambda b, j: (b, j, 0)),
                   pl.BlockSpec((None, tm, HW), lambda b, j: (b, j, 0)),
                   pl.BlockSpec((None, tm, B_WIDTH), lambda b, j: (b, j, 0))],
        out_shape=[jax.ShapeDtypeStruct((B, L, HW), BF16), jax.ShapeDtypeStruct((B, L, HW), BF16),
                   jax.ShapeDtypeStruct((B, L, B_WIDTH), BF16)],
        compiler_params=_cparams(("parallel", "parallel")),
        name="mla_prep",
    )(b_in, tab, qan, kvan, wqb, wk, wv, qw, kw, knr, indq, invn, place)


def _attn_kernel(q_ref, k_ref, v_ref, o_ref):
    v = v_ref[...]
    scores = []
    for hh in range(2):
        q = q_ref[:, hh * HEAD_PAD:(hh + 1) * HEAD_PAD]
        k = k_ref[:, hh * HEAD_PAD:(hh + 1) * HEAD_PAD]
        scores.append(lax.dot_general(q, k, (((1,), (1,)), ((), ())), preferred_element_type=F32))
    outs = []
    for s in scores:
        p = jnp.exp(s - jnp.max(s, axis=-1, keepdims=True))
        den = jnp.sum(p, axis=-1, keepdims=True)
        outs.append(jnp.dot(p.astype(BF16), v, preferred_element_type=F32) / den)
    lane = lax.broadcasted_iota(jnp.int32, outs[0].shape, 1)
    o_ref[...] = jnp.where(lane < B_V, outs[0], outs[1])


def _attn(q, k, v):
    B, L, _ = q.shape
    tq = 256
    return pl.pallas_call(
        _attn_kernel,
        grid=(B, B_HEADS // 2, L // tq),
        in_specs=[pl.BlockSpec((None, tq, 2 * HEAD_PAD), lambda b, p, j: (b, j, p)),
                  pl.BlockSpec((None, L, 2 * HEAD_PAD), lambda b, p, j: (b, 0, p)),
                  pl.BlockSpec((None, L, 2 * B_V), lambda b, p, j: (b, 0, p))],
        out_specs=pl.BlockSpec((None, tq, 2 * B_V), lambda b, p, j: (b, j, p)),
        out_shape=jax.ShapeDtypeStruct((B, L, B_WIDTH), F32),
        compiler_params=_cparams(("parallel", "parallel", "arbitrary")),
        name="mla_attn",
    )(q, k, v)


def _hgrn_tables():
    C = CHUNK
    m_all = np.zeros((8 * C, C), np.float32)
    masks = np.zeros((7, C, C), np.float32)
    for lvl in range(6):
        m = 1 << lvl
        for t in range(C):
            bd = (t // (2 * m)) * 2 * m + m - 1
            if t % (2 * m) >= m:
                m_all[lvl * C + t, bd + 1:t + 1] = 1.0
            else:
                m_all[lvl * C + t, t + 1:bd + 1] = 1.0
            for s in range(C):
                if t // (2 * m) == s // (2 * m) and t % (2 * m) >= m and s % (2 * m) < m:
                    masks[lvl, t, s] = 1.0
    for t in range(C):
        m_all[6 * C + t, :t + 1] = 1.0
        m_all[7 * C + t, t + 1:] = 1.0
    masks[6] = np.eye(C, dtype=np.float32)
    m_f = m_all.reshape(8, C, C)
    m_b = m_f[:, ::-1, ::-1]
    mk_b = masks[:, ::-1, ::-1]
    m2 = np.stack([m_f.reshape(8 * C, C), m_b.reshape(8 * C, C)])
    return (np.concatenate([m2, m2], axis=2), np.stack([masks, mk_b]))


def _hgrn_kernel(q_ref, ff_ref, fb_ref, i_ref, gate_ref, lb_ref, cnorm_ref, mall_ref, masks_ref, ind_ref,
                 out_ref, of_s, ob_s, s_s, *, L):
    nc = L // CHUNK
    C = CHUNK

    def wide(c, d):
        t0 = pl.multiple_of(c * C, C)
        q = _silu(q_ref[pl.ds(t0, C), :])
        fp = (ff_ref if d == 0 else fb_ref)[pl.ds(t0, C), :]
        lb = lb_ref[d:d + 1, :]
        lsig = jnp.minimum(fp, 0.0) - jnp.log(1.0 + jnp.exp(-jnp.abs(fp)))
        a = jnp.log(lb)
        bb = jnp.log(1.0 - lb) + lsig
        logf = jnp.maximum(a, bb) + jnp.log(1.0 + jnp.exp(-jnp.abs(a - bb)))
        kk = (1.0 - lb) * _sigmoid(-fp)
        e_all = jnp.exp(_sel2_dot(mall_ref[d], logf))
        return dict(t0=t0, q=q, k=kk, e=e_all, v=i_ref[pl.ds(t0, C), :])

    def scan_pair(i, carry):
        wd = [wide(i, 0), wide(nc - 1 - i, 1)]
        ks = [(d, h) for d in (0, 1) for h in range(C_HEADS)]
        ls = lambda h: slice(h * C_EXPAND, (h + 1) * C_EXPAND)
        vs = lambda h: slice(h * C_HEAD_DIM, (h + 1) * C_HEAD_DIM)
        qh = {(d, h): wd[d]['q'][:, ls(h)] for d, h in ks}
        kh = {(d, h): wd[d]['k'][:, ls(h)] for d, h in ks}
        vh = {(d, h): wd[d]['v'][:, vs(h)] for d, h in ks}
        lvl_e = lambda d, h, j: wd[d]['e'][j * C:(j + 1) * C, ls(h)]
        qb = {k: qh[k].astype(BF16) for k in ks}
        kb = {k: kh[k].astype(BF16) for k in ks}
        attn = {(d, h): _dot_nt(qb[d, h], kb[d, h]) * masks_ref[d, 6] for d, h in ks}
        for lvl in range(6):
            for d, h in ks:
                e = lvl_e(d, h, lvl).astype(BF16)
                attn[d, h] = attn[d, h] + _dot_nt(qb[d, h] * e, kb[d, h] * e) * masks_ref[d, lvl]
        st = {(d, h): s_s[d * C_HEADS + h] for d, h in ks}
        o = {}
        for d, h in ks:
            o[d, h] = _dot(attn[d, h], vh[d, h]) + _dot_nt(qb[d, h] * lvl_e(d, h, 6).astype(BF16), st[d, h])
        for d, h in ks:
            e_in = lvl_e(d, h, 6)
            f_last = e_in[C - 1:C] if d == 0 else e_in[0:1]
            s_s[d * C_HEADS + h] = st[d, h] * f_last + _dot_tn(vh[d, h], kb[d, h] * lvl_e(d, h, 7).astype(BF16))
        of_s[pl.ds(wd[0]['t0'], C), :] = jnp.concatenate([o[0, h] for h in range(C_HEADS)], axis=1)
        ob_s[pl.ds(wd[1]['t0'], C), :] = jnp.concatenate([o[1, h] for h in range(C_HEADS)], axis=1)
        return carry

    s_s[...] = jnp.zeros_like(s_s)
    lax.fori_loop(0, nc, scan_pair, 0)

    rows = 256
    ind = ind_ref[...]

    def finish(i, carry):
        t0 = pl.multiple_of(i * rows, rows)
        o = of_s[pl.ds(t0, rows), :] + ob_s[pl.ds(t0, rows), :]
        ms = _dot_sel(o * o, ind) * (1.0 / C_HEAD_DIM)
        o = o * lax.rsqrt(ms + EPS) * cnorm_ref[...]
        out_ref[pl.ds(t0, rows), :] = o * _silu(gate_ref[pl.ds(t0, rows), :])
        return carry

    lax.fori_loop(0, L // rows, finish, 0)


def _hgrn(c_in, lb, cnorm, mall, masks, ind):
    B, L, _ = c_in.shape
    kern = functools.partial(_hgrn_kernel, L=L)
    one = pl.Buffered(1)
    F, W = C_FDIM, C_WIDTH
    return pl.pallas_call(
        kern,
        grid=(B,),
        in_specs=[pl.BlockSpec((None, L, F), lambda b: (b, 0, 0), pipeline_mode=one),
                  pl.BlockSpec((None, L, F), lambda b: (b, 0, 1), pipeline_mode=one),
                  pl.BlockSpec((None, L, F), lambda b: (b, 0, 2), pipeline_mode=one),
                  pl.BlockSpec((None, L, W), lambda b: (b, 0, 3 * F // W), pipeline_mode=one),
                  pl.BlockSpec((None, L, W), lambda b: (b, 0, 3 * F // W + 1), pipeline_mode=one),
                  _full((2, F)), _full((1, W)), _full((2, 8 * CHUNK, 2 * CHUNK)), _full((2, 7, CHUNK, CHUNK)),
                  _full((W, W))],
        out_specs=pl.BlockSpec((None, L, W), lambda b: (b, 0, 0)),
        out_shape=jax.ShapeDtypeStruct((B, L, W), F32),
        scratch_shapes=[pltpu.VMEM((L, W), F32), pltpu.VMEM((L, W), F32),
                        pltpu.VMEM((2 * C_HEADS, C_HEAD_DIM, C_EXPAND), F32)],
        compiler_params=_cparams(("parallel",)),
        name="hgrn2",
    )(c_in, c_in, c_in, c_in, c_in, lb, cnorm, mall, masks, ind)


def _outproj_kernel(x_ref, oa_ref, ob_ref, oc_ref, wa_ref, wb_ref, wc_ref, ln_ref, wr_ref, br_ref, tri_ref,
                    x1_ref, h_ref, route_ref, cnt_ref, cnt_s):
    x1 = x_ref[...] + _dot(oa_ref[...], wa_ref[...]) + _dot(ob_ref[...], wb_ref[...]) + _dot(oc_ref[...], wc_ref[...])
    x1_ref[...] = x1
    h = _rms(x1, ln_ref[...])
    h_ref[...] = h
    hh, hl = _split(h)
    wr = wr_ref[...]
    wh, wl = _split(wr)
    lg = (jnp.dot(hh, wh, preferred_element_type=F32) + jnp.dot(hh, wl, preferred_element_type=F32)
          + jnp.dot(hl, wh, preferred_element_type=F32)) + br_ref[...]
    lane = lax.broadcasted_iota(jnp.int32, lg.shape, 1).astype(F32)
    neg = -1e30
    far = 1e9
    gmask = lane < N_GROUPS
    gl = jnp.where(gmask, lg, neg)
    gmax = jnp.max(gl, axis=-1, keepdims=True)
    g_idx = jnp.min(jnp.where(gl == gmax, lane, far), axis=-1, keepdims=True)
    g_w = 1.0 / jnp.sum(jnp.where(gmask, jnp.exp(gl - gmax), 0.0), axis=-1, keepdims=True)
    e_lane = lane - N_GROUPS
    emask = (e_lane >= g_idx * EXPERTS_PER_GROUP) & (e_lane < (g_idx + 1.0) * EXPERTS_PER_GROUP)
    el = jnp.where(emask, lg, neg)
    m1 = jnp.max(el, axis=-1, keepdims=True)
    i1 = jnp.min(jnp.where(el == m1, lane, far), axis=-1, keepdims=True)
    el2 = jnp.where(lane == i1, neg, el)
    m2 = jnp.max(el2, axis=-1, keepdims=True)
    i2 = jnp.min(jnp.where((el2 == m2) & emask & (lane != i1), lane, far), axis=-1, keepdims=True)
    r = jnp.exp(m2 - m1)
    w1 = g_w / (1.0 + r)
    w2 = g_w * r / (1.0 + r)
    @pl.when(pl.program_id(0) == 0)
    def _():
        cnt_s[...] = jnp.zeros_like(cnt_s)

    oh1 = jnp.where(lane == i1, 1.0, 0.0)
    oh2 = jnp.where(lane == i2, 1.0, 0.0)
    tri = tri_ref[...]
    run = cnt_s[0:1, :]
    tot1 = jnp.sum(oh1, axis=0, keepdims=True)
    c1 = run + jnp.dot(tri, oh1.astype(BF16), preferred_element_type=F32)
    c2 = run + tot1 + jnp.dot(tri, oh2.astype(BF16), preferred_element_type=F32)
    r1 = jnp.sum(oh1 * c1, axis=-1, keepdims=True)
    r2 = jnp.sum(oh2 * c2, axis=-1, keepdims=True)
    cnt = jnp.broadcast_to(run + tot1 + jnp.sum(oh2, axis=0, keepdims=True), cnt_s.shape)
    cnt_s[...] = cnt
    cnt_ref[...] = cnt
    vals = (i1 - N_GROUPS, i2 - N_GROUPS, w1, w2, r1, r2)
    route = jnp.zeros_like(lg)
    for j, v in enumerate(vals):
        route = jnp.where(lane == float(j), v, route)
    route_ref[...] = route


def _outproj(x, oa, ob, oc, wa, wb, wc, ln, wr, br, tri):
    T = x.shape[0]
    tm = TOK_TILE
    row = lambda w: pl.BlockSpec((tm, w), lambda i: (i, 0))
    return pl.pallas_call(
        _outproj_kernel,
        grid=(T // tm,),
        in_specs=[row(D_MODEL), row(A_WIDTH), row(B_WIDTH), row(C_WIDTH),
                  _full((A_WIDTH, D_MODEL)), _full((B_WIDTH, D_MODEL)), _full((C_WIDTH, D_MODEL)),
                  _full((1, D_MODEL)), _full((D_MODEL, LANE)), _full((1, LANE)), _full((tm, tm))],
        out_specs=[row(D_MODEL), row(D_MODEL), row(LANE), _full((8, LANE))],
        out_shape=[jax.ShapeDtypeStruct((T, D_MODEL), F32), jax.ShapeDtypeStruct((T, D_MODEL), F32),
                   jax.ShapeDtypeStruct((T, LANE), F32), jax.ShapeDtypeStruct((8, LANE), F32)],
        scratch_shapes=[pltpu.VMEM((8, LANE), F32)],
        compiler_params=_cparams(("arbitrary",)),
        name="outproj_router",
    )(x, oa, ob, oc, wa, wb, wc, ln, wr, br, tri)


def _gather_rows(src_hbm, buf, sem, cur_ids, nxt_ids, n_rows):
    i = pl.program_id(0)
    slot = i % 2

    def row_copy(ids, r, s):
        return pltpu.make_async_copy(src_hbm.at[pl.ds(ids[0, 0, r], 1)], buf.at[s, pl.ds(r, 1)], sem.at[s])

    def start_all(ids, s):
        def body(r, carry):
            row_copy(ids, r, s).start()
            return carry
        lax.fori_loop(0, n_rows, body, 0, unroll=8)

    def wait_all(s):
        pltpu.make_async_copy(src_hbm.at[pl.ds(0, n_rows)], buf.at[s], sem.at[s]).wait()

    @pl.when(i == 0)
    def _():
        start_all(cur_ids, 0)

    start_all(nxt_ids, 1 - slot)
    wait_all(slot)

    @pl.when(i == pl.num_programs(0) - 1)
    def _():
        wait_all(1 - slot)

    return slot


def _scatter_kernel(pos_ref, h_ref, xs_in, xs_hbm, sem):
    del xs_in
    tm = TOK_TILE

    def row_copy(r, k):
        return pltpu.make_async_copy(h_ref.at[pl.ds(r, 1)], xs_hbm.at[pl.ds(pos_ref[0, 0, k * tm + r], 1)], sem)

    def issue(r, carry):
        for k in range(TOP_K):
            row_copy(r, k).start()
        return carry

    lax.fori_loop(0, tm, issue, 0, unroll=8)
    for k in range(TOP_K):
        pltpu.make_async_copy(h_ref, xs_hbm.at[pl.ds(0, tm)], sem).wait()


def _scatter(pos, h, xs_init):
    T = h.shape[0]
    tm = TOK_TILE
    return pl.pallas_call(
        _scatter_kernel,
        grid=(T // tm,),
        in_specs=[pl.BlockSpec((1, 1, TOP_K * tm), lambda i: (i, 0, 0), memory_space=pltpu.SMEM),
                  pl.BlockSpec((tm, D_MODEL), lambda i: (i, 0)), pl.BlockSpec(memory_space=pl.ANY)],
        out_specs=pl.BlockSpec(memory_space=pl.ANY),
        out_shape=jax.ShapeDtypeStruct(xs_init.shape, F32),
        scratch_shapes=[pltpu.SemaphoreType.DMA(())],
        input_output_aliases={2: 0},
        compiler_params=_cparams(("arbitrary",)),
        name="moe_scatter",
    )(pos, h, xs_init)


def _moe_kernel(blk_e_ref, nact_ref, x_ref, w1_ref, w3_ref, w2_ref, y_ref):
    active = pl.program_id(0) < nact_ref[0]

    @pl.when(active)
    def _():
        x = x_ref[...].astype(BF16)
        a = jnp.dot(x, w1_ref[...], preferred_element_type=F32)
        g = jnp.dot(x, w3_ref[...], preferred_element_type=F32)
        y_ref[...] = jnp.dot((_silu(a) * g).astype(BF16), w2_ref[...], preferred_element_type=F32)

    @pl.when(jnp.logical_not(active))
    def _():
        y_ref[...] = jnp.zeros_like(y_ref)


def _moe(blk_e, nact, xs, w1, w3, w2):
    P = xs.shape[0]
    nblk = P // MOE_ROWS
    grid_spec = pltpu.PrefetchScalarGridSpec(
        num_scalar_prefetch=2,
        grid=(nblk,),
        in_specs=[pl.BlockSpec((MOE_ROWS, D_MODEL), lambda i, be, na: (i, 0)),
                  pl.BlockSpec((None, D_MODEL, D_EXPERT), lambda i, be, na: (be[i], 0, 0)),
                  pl.BlockSpec((None, D_MODEL, D_EXPERT), lambda i, be, na: (be[i], 0, 0)),
                  pl.BlockSpec((None, D_EXPERT, D_MODEL), lambda i, be, na: (be[i], 0, 0))],
        out_specs=pl.BlockSpec((MOE_ROWS, D_MODEL), lambda i, be, na: (i, 0)),
    )
    return pl.pallas_call(
        _moe_kernel,
        grid_spec=grid_spec,
        out_shape=jax.ShapeDtypeStruct((P, D_MODEL), F32),
        compiler_params=_cparams(("arbitrary",)),
        name="moe_ffn",
    )(blk_e, nact, xs, w1, w3, w2)


def _combine_kernel(pos_ref, nxt_ref, x_ref, route_ref, y_hbm, out_ref, yb, sem):
    tm = x_ref.shape[0]
    slot = _gather_rows(y_hbm, yb, sem, pos_ref, nxt_ref, TOP_K * tm)
    route = route_ref[...]
    out_ref[...] = x_ref[...] + yb[slot, 0:tm, :] * route[:, 2:3] + yb[slot, tm:2 * tm, :] * route[:, 3:4]


def _combine(pos, x1, route, y):
    T = x1.shape[0]
    tm = TOK_TILE
    nt = T // tm
    return pl.pallas_call(
        _combine_kernel,
        grid=(nt,),
        in_specs=[pl.BlockSpec((1, 1, TOP_K * tm), lambda i: (i, 0, 0), memory_space=pltpu.SMEM),
                  pl.BlockSpec((1, 1, TOP_K * tm), lambda i: (jnp.minimum(i + 1, nt - 1), 0, 0),
                               memory_space=pltpu.SMEM),
                  pl.BlockSpec((tm, D_MODEL), lambda i: (i, 0)),
                  pl.BlockSpec((tm, LANE), lambda i: (i, 0)),
                  pl.BlockSpec(memory_space=pl.ANY)],
        out_specs=pl.BlockSpec((tm, D_MODEL), lambda i: (i, 0)),
        out_shape=jax.ShapeDtypeStruct((T, D_MODEL), F32),
        scratch_shapes=[pltpu.VMEM((2, TOP_K * tm, D_MODEL), F32), pltpu.SemaphoreType.DMA((2,))],
        compiler_params=_cparams(("arbitrary",)),
        name="moe_combine",
    )(pos, pos, x1, route, y)


def _dispatch(route, cnt):
    T = route.shape[0]
    nblk = (T * TOP_K) // MOE_ROWS + N_EXPERTS
    counts = cnt[0, N_GROUPS:N_GROUPS + N_EXPERTS].astype(jnp.int32)
    padded = (counts + MOE_ROWS - 1) // MOE_ROWS * MOE_ROWS
    pends = jnp.cumsum(padded)
    pstarts = pends - padded
    e = route[:, 0:TOP_K].astype(jnp.int32)
    rank = route[:, 4:4 + TOP_K].astype(jnp.int32)
    sel = e[:, :, None] == jnp.arange(N_EXPERTS, dtype=jnp.int32)
    pos = rank + jnp.sum(jnp.where(sel, pstarts, 0), axis=-1)
    pos = jnp.transpose(pos.reshape(T // TOK_TILE, TOK_TILE, TOP_K), (0, 2, 1))
    pos = pos.reshape(T // TOK_TILE, 1, TOP_K * TOK_TILE)
    blk_start = jnp.arange(nblk, dtype=jnp.int32) * MOE_ROWS
    blk_e = jnp.minimum(jnp.sum(blk_start[:, None] >= pends[None, :], axis=-1), N_EXPERTS - 1).astype(jnp.int32)
    nact = (pends[-1:] // MOE_ROWS).astype(jnp.int32)
    return blk_e, nact, pos


def _block_ind(width, seg):
    i = np.arange(width)
    return (i[:, None] // seg == i[None, :] // seg).astype(np.float32)


def _rope_tables(L):
    half = B_ROPE // 2
    inv = ROPE_THETA ** (-jnp.arange(0, B_ROPE, 2, dtype=F32) / B_ROPE)
    ang = jnp.arange(L, dtype=F32)[:, None] * inv[None, :]
    cos, sin = jnp.cos(ang), jnp.sin(ang)
    z = lambda n: jnp.zeros((L, n), F32)
    one = jnp.ones((L, B_NOPE), F32)
    rest = LANE - B_NOPE - B_ROPE
    cq = jnp.concatenate([one, cos, cos, z(rest)], axis=1)
    s1q = jnp.concatenate([z(B_NOPE), -sin, z(half), z(rest)], axis=1)
    s2q = jnp.concatenate([z(B_NOPE), z(half), sin, z(rest)], axis=1)
    ck = jnp.concatenate([cos, cos, z(LANE - B_ROPE)], axis=1)
    s1k = jnp.concatenate([-sin, z(LANE - half)], axis=1)
    s2k = jnp.concatenate([z(half), sin, z(LANE - B_ROPE)], axis=1)
    return jnp.concatenate([cq, s1q, s2q, ck, s1k, s2k], axis=1)


def _layer_params(l, ln1, w_in, conv_w, a_log, dt_bias, a_norm, q_a_norm, w_qb, kv_a_norm, w_kvb, qn_nope, qn_rope,
                  kn_nope, kn_rope, lbs, c_norm, w_out, ln2, w_group, b_group, w_router, b_router, w1, w3, w2):
    p = {}
    w = w_in[l]
    wp = jnp.zeros((D_MODEL, N_PAD), F32)
    wp = wp.at[:, 0:A_IN].set(w[:, 0:A_IN])
    wp = wp.at[:, A_PAD:A_PAD + B_IN].set(w[:, A_IN:A_IN + B_IN])
    wp = wp.at[:, A_PAD + B_PAD:].set(w[:, A_IN + B_IN:])
    p['w_in'] = wp.astype(BF16)
    p['ln1'] = ln1[l].reshape(1, D_MODEL)
    p['convw'] = jnp.zeros((8, 3 * A_WIDTH), F32).at[0:CONV_K].set(conv_w[l])
    gpad = lambda v: jnp.zeros((1, LANE), F32).at[0, 2 * A_HEADS:4 * A_HEADS].set(v.reshape(-1))
    p['alog'] = gpad(a_log[l])
    p['dtb'] = gpad(dt_bias[l])
    p['anorm'] = jnp.tile(a_norm[l], A_HEADS).reshape(1, A_WIDTH)
    half = B_ROPE // 2
    wq = w_qb[l].reshape(B_Q_RANK, B_HEADS, B_NOPE + B_ROPE)
    wq = jnp.pad(wq, ((0, 0), (0, 0), (0, HEAD_PAD - B_NOPE - B_ROPE)))
    p['wqb'] = wq.reshape(B_Q_RANK, B_HEADS * HEAD_PAD).astype(BF16)
    wkv = w_kvb[l].reshape(B_KV_RANK, B_HEADS, B_NOPE + B_V)
    wk = jnp.pad(wkv[:, :, :B_NOPE], ((0, 0), (0, 0), (0, HEAD_PAD - B_NOPE)))
    p['wk'] = wk.reshape(B_KV_RANK, B_HEADS * HEAD_PAD).astype(BF16)
    p['wv'] = wkv[:, :, B_NOPE:].reshape(B_KV_RANK, B_WIDTH).astype(BF16)
    zpad = jnp.zeros((HEAD_PAD - B_NOPE - B_ROPE,), F32)
    p['qw'] = jnp.tile(jnp.concatenate([qn_nope[l], qn_rope[l], zpad]), B_HEADS).reshape(1, -1)
    p['kw'] = jnp.tile(jnp.concatenate([kn_nope[l], jnp.zeros((HEAD_PAD - B_NOPE,), F32)]), B_HEADS).reshape(1, -1)
    p['knr'] = jnp.concatenate([kn_rope[l], jnp.zeros((LANE - B_ROPE,), F32)]).reshape(1, LANE)
    p['qan'] = q_a_norm[l].reshape(1, -1)
    p['kvan'] = kv_a_norm[l].reshape(1, -1)
    p['lb'] = lbs[l]
    p['cnorm'] = jnp.tile(c_norm[l], C_HEADS).reshape(1, C_WIDTH)
    wo = w_out[l].astype(BF16)
    p['wa'], p['wb'], p['wc'] = wo[:A_WIDTH], wo[A_WIDTH:A_WIDTH + B_WIDTH], wo[A_WIDTH + B_WIDTH:]
    p['ln2'] = ln2[l].reshape(1, D_MODEL)
    wr = jnp.zeros((D_MODEL, LANE), F32).at[:, :N_GROUPS].set(w_group[l])
    p['wr'] = wr.at[:, N_GROUPS:N_GROUPS + N_EXPERTS].set(w_router[l])
    br = jnp.zeros((1, LANE), F32).at[0, :N_GROUPS].set(b_group[l])
    p['br'] = br.at[0, N_GROUPS:N_GROUPS + N_EXPERTS].set(b_router[l])
    p['w1'], p['w3'], p['w2'] = w1[l].astype(BF16), w3[l].astype(BF16), w2[l].astype(BF16)
    return p


def _static_tables():
    t = {}
    t['ind_a'] = jnp.asarray(_block_ind(A_WIDTH, A_HEAD_DIM), BF16)
    t['ind_c'] = jnp.asarray(_block_ind(C_WIDTH, C_HEAD_DIM), BF16)
    expb = np.zeros((2, LANE, A_WIDTH), np.float32)
    expg = np.zeros((2, LANE, A_WIDTH), np.float32)
    for d in range(2):
        for h in range(A_HEADS):
            expb[d, A_HEADS * d + h, h * A_HEAD_DIM:(h + 1) * A_HEAD_DIM] = 1.0
            expg[d, 2 * A_HEADS + A_HEADS * d + h, h * A_HEAD_DIM:(h + 1) * A_HEAD_DIM] = 1.0
    t['expb'] = jnp.asarray(expb, BF16)
    t['expg'] = jnp.asarray(expg, BF16)
    t['pick'] = jnp.asarray(np.repeat(np.transpose(expg[:, :, ::A_HEAD_DIM], (0, 2, 1)), CHUNK, axis=1), BF16)
    HW = B_HEADS * HEAD_PAD
    lane = np.arange(HW)
    seg = np.where(lane % HEAD_PAD < B_NOPE, 0, np.where(lane % HEAD_PAD < B_NOPE + B_ROPE, 1, 2))
    head = lane // HEAD_PAD
    indq = (head[:, None] == head[None, :]) & (seg[:, None] == seg[None, :]) & (seg[:, None] < 2)
    t['indq'] = jnp.asarray(indq.astype(np.float32), BF16)
    t['invn'] = jnp.asarray(np.where(seg == 0, 1.0 / B_NOPE, 1.0 / B_ROPE).astype(np.float32)).reshape(1, HW)
    place = np.zeros((LANE, HW), np.float32)
    for h in range(B_HEADS):
        for r in range(B_ROPE):
            place[r, h * HEAD_PAD + B_NOPE + r] = 1.0
    t['place'] = jnp.asarray(place, BF16)
    t['tri'] = jnp.asarray(np.tril(np.ones((TOK_TILE, TOK_TILE), np.float32), -1), BF16)
    mall, masks = _hgrn_tables()
    t['mall'] = jnp.asarray(mall, BF16)
    t['masks'] = jnp.asarray(masks, F32)
    return t


def _trunk(x, params, tabs, rope):
    B, L, D = x.shape
    T = B * L
    xf = x.reshape(T, D)
    for p in params:
        a_in, b_in, c_in = _proj(xf, p['ln1'], p['w_in'])
        o_a = _gdn(a_in.reshape(B, L, A_PAD), p['convw'], p['alog'], p['dtb'], p['anorm'], tabs['ind_a'],
                   tabs['expb'], tabs['expg'], tabs['pick'])
        q, k, v = _mla_prep(b_in.reshape(B, L, B_PAD), rope, p['qan'], p['kvan'], p['wqb'], p['wk'], p['wv'],
                            p['qw'], p['kw'], p['knr'], tabs['indq'], tabs['invn'], tabs['place'])
        o_b = _attn(q, k, v)
        o_c = _hgrn(c_in.reshape(B, L, C_PAD), p['lb'], p['cnorm'], tabs['mall'], tabs['masks'], tabs['ind_c'])
        x1, h, route, cnt = _outproj(xf, o_a.reshape(T, A_WIDTH), o_b.reshape(T, B_WIDTH), o_c.reshape(T, C_WIDTH),
                                     p['wa'], p['wb'], p['wc'], p['ln2'], p['wr'], p['br'], tabs['tri'])
        blk_e, nact, pos = _dispatch(route, cnt)
        xs = _scatter(pos, h, jnp.zeros((T * TOP_K + N_EXPERTS * MOE_ROWS, D), F32))
        y = _moe(blk_e, nact, xs, p['w1'], p['w3'], p['w2'])
        xf = _combine(pos, x1, route, y)
    return xf.reshape(B, L, D)


def kernel(x_prompt, x_sample, ln1, w_in, conv_w, a_log, dt_bias, a_norm, q_a_norm, w_qb, kv_a_norm, w_kvb, qn_nope,
           qn_rope, kn_nope, kn_rope, c_lower_bounds, c_norm, w_out, ln2, w_group, b_group, w_router, b_router,
           w1, w3, w2):
    lbs = jnp.cumsum(jax.nn.softmax(c_lower_bounds.astype(F32), axis=0), axis=0)
    lbs = lbs - lbs[0:1]
    params = [_layer_params(l, ln1, w_in, conv_w, a_log, dt_bias, a_norm, q_a_norm, w_qb, kv_a_norm, w_kvb, qn_nope,
                            qn_rope, kn_nope, kn_rope, lbs, c_norm, w_out, ln2, w_group, b_group, w_router, b_router,
                            w1, w3, w2) for l in range(DEPTH)]
    tabs = _static_tables()
    assert x_prompt.shape[1:] == x_sample.shape[1:]
    rope = _rope_tables(x_prompt.shape[1])
    nb = x_prompt.shape[0]
    y = _trunk(jnp.concatenate([x_prompt, x_sample], axis=0), params, tabs, rope)
    return (y[:nb], y[nb:])
```

```python
import functools
import math

import numpy as np
import jax
import jax.numpy as jnp
from jax import lax
from jax.experimental import pallas as pl
from jax.experimental.pallas import tpu as pltpu

F32 = jnp.float32
BF16 = jnp.bfloat16

D_MODEL = 1024
DEPTH = 2
EPS = 1e-6
A_HEADS, A_HEAD_DIM, CONV_K = 6, 64, 5
A_WIDTH = A_HEADS * A_HEAD_DIM
B_HEADS, B_Q_RANK, B_KV_RANK, B_NOPE, B_ROPE, B_V = 6, 256, 128, 64, 32, 64
B_WIDTH = B_HEADS * B_V
ROPE_THETA = 10000.0
C_HEADS, C_EXPAND, C_HEAD_DIM = 4, 128, 64
C_WIDTH = C_HEADS * C_HEAD_DIM
C_FDIM = C_HEADS * C_EXPAND
D_MIX = A_WIDTH + B_WIDTH + C_WIDTH
A_IN = 4 * A_WIDTH + 4 * A_HEADS
B_IN = B_Q_RANK + B_KV_RANK + B_ROPE
C_IN = 3 * C_FDIM + 2 * C_WIDTH
N_GROUPS, EXPERTS_PER_GROUP, TOP_K, D_EXPERT = 4, 8, 2, 512
N_EXPERTS = N_GROUPS * EXPERTS_PER_GROUP

LANE = 128
CHUNK = 64
A_PAD = 4 * A_WIDTH + LANE
B_PAD = 512
C_PAD = C_IN
N_PAD = A_PAD + B_PAD + C_PAD
HEAD_PAD = LANE
MOE_ROWS = 256
TOK_TILE = 256
VMEM_LIMIT = 56 * 1024 * 1024


def _cparams(sem):
    return pltpu.CompilerParams(dimension_semantics=sem, vmem_limit_bytes=VMEM_LIMIT)


def _dot(a, b):
    return jnp.dot(a.astype(BF16), b.astype(BF16), preferred_element_type=F32)


def _dot_nt(a, b):
    return lax.dot_general(a.astype(BF16), b.astype(BF16), (((1,), (1,)), ((), ())), preferred_element_type=F32)


def _dot_tn(a, b):
    return lax.dot_general(a.astype(BF16), b.astype(BF16), (((0,), (0,)), ((), ())), preferred_element_type=F32)


def _split(x):
    hi = x.astype(BF16)
    lo = (x - hi.astype(F32)).astype(BF16)
    return hi, lo


def _sel_dot(sel, x):
    hi, lo = _split(x)
    return jnp.dot(sel, hi, preferred_element_type=F32) + jnp.dot(sel, lo, preferred_element_type=F32)


def _sel2_dot(sel2, x):
    hi, lo = _split(x)
    return jnp.dot(sel2, jnp.concatenate([hi, lo], axis=0), preferred_element_type=F32)


def _dot_sel(x, sel):
    hi, lo = _split(x)
    return jnp.dot(hi, sel, preferred_element_type=F32) + jnp.dot(lo, sel, preferred_element_type=F32)


def _sigmoid(x):
    return 1.0 / (1.0 + jnp.exp(-x))


def _silu(x):
    return x * _sigmoid(x)


def _softplus(x):
    return jnp.maximum(x, 0.0) + jnp.log(1.0 + jnp.exp(-jnp.abs(x)))


def _rms(x, w):
    return x * lax.rsqrt(jnp.mean(x * x, axis=-1, keepdims=True) + EPS) * w


def _full(shape):
    return pl.BlockSpec(shape, lambda *_: (0,) * len(shape))


def _proj_kernel(x_ref, ln_ref, w_ref, lb_ref, a_ref, b_ref, cq_ref, clf_ref, ck_ref, cv_ref, cg_ref):
    h = _rms(x_ref[...], ln_ref[...]).astype(BF16)
    a_ref[...] = jnp.dot(h, w_ref[:, 0:A_PAD], preferred_element_type=F32)
    b_ref[...] = jnp.dot(h, w_ref[:, A_PAD:A_PAD + B_PAD], preferred_element_type=F32)
    c0 = A_PAD + B_PAD
    F = C_FDIM
    cq_ref[...] = _silu(jnp.dot(h, w_ref[:, c0:c0 + F], preferred_element_type=F32)).astype(BF16)
    for d in range(2):
        fp = jnp.dot(h, w_ref[:, c0 + (1 + d) * F:c0 + (2 + d) * F], preferred_element_type=F32)
        lb = lb_ref[d:d + 1, :]
        e = jnp.exp(-jnp.abs(fp))
        r = 1.0 / (1.0 + e)
        lsig = jnp.minimum(fp, 0.0) - jnp.log(1.0 + e)
        a = jnp.log(lb)
        bb = jnp.log(1.0 - lb) + lsig
        clf_ref[:, d * F:(d + 1) * F] = jnp.maximum(a, bb) + jnp.log(1.0 + jnp.exp(-jnp.abs(a - bb)))
        ck_ref[:, d * F:(d + 1) * F] = ((1.0 - lb) * jnp.where(fp >= 0.0, e, 1.0) * r).astype(BF16)
    tail = jnp.dot(h, w_ref[:, c0 + 3 * F:N_PAD], preferred_element_type=F32)
    cv_ref[...] = tail[:, :C_WIDTH].astype(BF16)
    cg_ref[...] = tail[:, C_WIDTH:]


def _proj(x, ln, w_pad, lb):
    T = x.shape[0]
    tm = TOK_TILE
    row = lambda w: pl.BlockSpec((tm, w), lambda i: (i, 0))
    sds = jax.ShapeDtypeStruct
    return pl.pallas_call(
        _proj_kernel,
        grid=(T // tm,),
        in_specs=[row(D_MODEL), _full((1, D_MODEL)), _full((D_MODEL, N_PAD)), _full((2, C_FDIM))],
        out_specs=[row(A_PAD), row(B_PAD), row(C_FDIM), row(2 * C_FDIM), row(2 * C_FDIM), row(C_WIDTH), row(C_WIDTH)],
        out_shape=[sds((T, A_PAD), F32), sds((T, B_PAD), F32), sds((T, C_FDIM), BF16), sds((T, 2 * C_FDIM), F32),
                   sds((T, 2 * C_FDIM), BF16), sds((T, C_WIDTH), BF16), sds((T, C_WIDTH), F32)],
        compiler_params=_cparams(("parallel",)),
        name="proj",
    )(x, ln, w_pad, lb)


def _gdn_kernel(qkv_ref, gate_ref, gates_ref, convw_ref, alog_ref, dtb_ref, anorm_ref, ind_ref, expb_ref, expg_ref,
                pick_ref, out_ref, q_s, k_s, v_s, of_s, ob_s, s_s, *, L):
    nc = L // CHUNK
    W = A_WIDTH
    row = lax.broadcasted_iota(jnp.int32, (CHUNK, CHUNK), 0)
    col = lax.broadcasted_iota(jnp.int32, (CHUNK, CHUNK), 1)
    lane = lax.broadcasted_iota(jnp.int32, (CHUNK, LANE), 1)
    row2 = lax.broadcasted_iota(jnp.int32, (CHUNK, 2 * CHUNK), 0)
    col2 = lax.broadcasted_iota(jnp.int32, (CHUNK, 2 * CHUNK), 1) % CHUNK
    ind = ind_ref[...]

    def conv_chunk(c, carry):
        t0 = pl.multiple_of(c * CHUNK, CHUNK)
        tp = pl.multiple_of(jnp.maximum(t0 - 8, 0), 8)
        tn = pl.multiple_of(jnp.minimum(t0 + CHUNK, L - 8), 8)
        for g in range(3):
            cs = slice(g * W, (g + 1) * W)
            main = qkv_ref[pl.ds(t0, CHUNK), cs]
            prev = jnp.where(c > 0, qkv_ref[pl.ds(tp, 8), cs], 0.0)
            nxt = jnp.where(c < nc - 1, qkv_ref[pl.ds(tn, 8), cs], 0.0)
            win = jnp.concatenate([prev, main, nxt], axis=0)
            acc = win * convw_ref[2:3, cs]
            for j in (0, 1, 3, 4):
                acc = acc + pltpu.roll(win, (2 - j) % (CHUNK + 16), 0) * convw_ref[j:j + 1, cs]
            y = _silu(acc[8:8 + CHUNK])
            if g < 2:
                y = y * lax.rsqrt(_dot_sel(y * y, ind) + EPS)
            if g == 0:
                q_s[pl.ds(t0, CHUNK), :] = (y * (A_HEAD_DIM ** -0.5)).astype(BF16)
            elif g == 1:
                k_s[pl.ds(t0, CHUNK), :] = y.astype(BF16)
            else:
                v_s[pl.ds(t0, CHUNK), :] = y
        return carry

    lax.fori_loop(0, nc, conv_chunk, 0)

    coef = -jnp.exp(alog_ref[...])
    dtb = dtb_ref[...]
    g_lane = (lane >= 2 * A_HEADS) & (lane < 4 * A_HEADS)

    def wide(c, d):
        t0 = pl.multiple_of(c * CHUNK, CHUNK)
        qc = q_s[pl.ds(t0, CHUNK), :].astype(F32)
        kc = k_s[pl.ds(t0, CHUNK), :].astype(F32)
        vc = v_s[pl.ds(t0, CHUNK), :]
        gt = gates_ref[pl.ds(t0, CHUNK), :]
        beta = _sigmoid(gt)
        gl = jnp.where(g_lane, coef * _softplus(gt + dtb), 0.0)
        tri2 = ((col2 <= row2) if d == 0 else (col2 >= row2)).astype(BF16)
        gcum = _sel2_dot(tri2, gl)
        total = gcum[CHUNK - 1:CHUNK] if d == 0 else gcum[0:1]
        e_in = jnp.exp(gcum)
        e_out = jnp.exp(total - gcum)
        g_last = jnp.broadcast_to(jnp.exp(total), (8, LANE))
        expb = expb_ref[d]
        expg = expg_ref[d]
        beta_w = jnp.dot(beta.astype(BF16), expb, preferred_element_type=F32)
        ein_w = jnp.dot(e_in.astype(BF16), expg, preferred_element_type=F32)
        eout_w = jnp.dot(e_out.astype(BF16), expg, preferred_element_type=F32)
        kb = kc * beta_w
        return dict(
            t0=t0, q=qc, k=kc, kb=kb, vb=vc * beta_w, kbe=kb * ein_w, qe=qc * ein_w, kd=kc * eout_w,
            glast=_dot_sel(g_last, expg),
            gcum=gcum, gcum_t=gcum.T,
            incl=(col <= row) if d == 0 else (col >= row),
            strict=(col < row) if d == 0 else (col > row))

    unroll = 2

    def scan_group(i, carry):
        wd = [[wide(unroll * i + u, 0), wide(nc - 1 - (unroll * i + u), 1)] for u in range(unroll)]
        dh = [(d, h) for d in (0, 1) for h in range(A_HEADS)]
        ks = [(u, d, h) for u in range(unroll) for d, h in dh]
        hs = lambda h: slice(h * A_HEAD_DIM, (h + 1) * A_HEAD_DIM)
        dec, n, qk, x = {}, {}, {}, {}
        for u, d, h in ks:
            w = wd[u][d]
            lg = 2 * A_HEADS + A_HEADS * d + h
            dec[u, d, h] = jnp.exp(jnp.minimum(w['gcum'][:, lg:lg + 1] - w['gcum_t'][lg:lg + 1, :], 0.0))
        for u, d, h in ks:
            w = wd[u][d]
            n[u, d, h] = jnp.where(w['strict'], _dot_nt(w['kb'][:, hs(h)], w['k'][:, hs(h)]) * dec[u, d, h], 0.0)
        for u, d, h in ks:
            w = wd[u][d]
            qk[u, d, h] = jnp.where(w['incl'], _dot_nt(w['q'][:, hs(h)], w['k'][:, hs(h)]) * dec[u, d, h], 0.0)
            x[u, d, h] = jnp.concatenate([w['vb'][:, hs(h)], w['kbe'][:, hs(h)]], axis=1)
        xw = 2 * A_HEAD_DIM
        z = {k: _dot(n[k], jnp.concatenate([x[k], n[k]], axis=1)) for k in ks}
        x = {k: x[k] - z[k][:, :xw] for k in ks}
        for _ in range(4):
            p = {k: z[k][:, xw:] for k in ks}
            z = {k: _dot(p[k], jnp.concatenate([x[k], p[k]], axis=1)) for k in ks}
            x = {k: x[k] + z[k][:, :xw] for k in ks}
        x = {k: x[k] + _dot(z[k][:, xw:], x[k]) for k in ks}
        s = {(d, h): s_s[d * A_HEADS + h] for d, h in dh}
        for u in range(unroll):
            v_new = {(d, h): x[u, d, h][:, :A_HEAD_DIM] - _dot(x[u, d, h][:, A_HEAD_DIM:], s[d, h]) for d, h in dh}
            o = {(d, h): _dot(wd[u][d]['qe'][:, hs(h)], s[d, h]) + _dot(qk[u, d, h], v_new[d, h]) for d, h in dh}
            s = {(d, h): (s[d, h] * wd[u][d]['glast'][0:1, hs(h)] + _dot_tn(wd[u][d]['kd'][:, hs(h)], v_new[d, h]))
                 for d, h in dh}
            of_s[pl.ds(wd[u][0]['t0'], CHUNK), :] = jnp.concatenate([o[0, h] for h in range(A_HEADS)], axis=1)
            ob_s[pl.ds(wd[u][1]['t0'], CHUNK), :] = jnp.concatenate([o[1, h] for h in range(A_HEADS)], axis=1)
        for d, h in dh:
            s_s[d * A_HEADS + h] = s[d, h]
        return carry

    s_s[...] = jnp.zeros_like(s_s)
    lax.fori_loop(0, nc // unroll, scan_group, 0)

    rows = 256

    def finish(i, carry):
        t0 = pl.multiple_of(i * rows, rows)
        o = of_s[pl.ds(t0, rows), :] + ob_s[pl.ds(t0, rows), :]
        ms = _dot_sel(o * o, ind) * (1.0 / A_HEAD_DIM)
        o = o * lax.rsqrt(ms + EPS) * anorm_ref[...]
        out_ref[pl.ds(t0, rows), :] = o * _silu(gate_ref[pl.ds(t0, rows), :])
        return carry

    lax.fori_loop(0, L // rows, finish, 0)


def _gdn(a_in, convw, alog, dtb, anorm, ind, expb, expg, pick):
    B, L, _ = a_in.shape
    W = A_WIDTH
    kern = functools.partial(_gdn_kernel, L=L)
    one = pl.Buffered(1)
    return pl.pallas_call(
        kern,
        grid=(B,),
        in_specs=[pl.BlockSpec((None, L, 3 * W), lambda b: (b, 0, 0), pipeline_mode=one),
                  pl.BlockSpec((None, L, W), lambda b: (b, 0, 3), pipeline_mode=one),
                  pl.BlockSpec((None, L, LANE), lambda b: (b, 0, 4 * W // LANE), pipeline_mode=one),
                  _full((8, 3 * W)), _full((1, LANE)), _full((1, LANE)), _full((1, W)), _full((W, W)),
                  _full((2, LANE, W)), _full((2, LANE, W)), _full((2, A_HEADS * CHUNK, LANE))],
        out_specs=pl.BlockSpec((None, L, W), lambda b: (b, 0, 0)),
        out_shape=jax.ShapeDtypeStruct((B, L, W), F32),
        scratch_shapes=[pltpu.VMEM((L, W), BF16), pltpu.VMEM((L, W), BF16), pltpu.VMEM((L, W), F32),
                        pltpu.VMEM((L, W), F32), pltpu.VMEM((L, W), F32),
                        pltpu.VMEM((2 * A_HEADS, A_HEAD_DIM, A_HEAD_DIM), F32)],
        compiler_params=_cparams(("parallel",)),
        name="gdn",
    )(a_in, a_in, a_in, convw, alog, dtb, anorm, ind, expb, expg, pick)


def _mla_prep_kernel(b_ref, tab_ref, qan_ref, kvan_ref, wqb_ref, wk_ref, wv_ref, qw_ref, kw_ref, knr_ref,
                     indq_ref, invn_ref, place_ref, q_out, k_out, v_out):
    b = b_ref[...]
    tab = tab_ref[...]
    cq = _rms(b[:, :B_Q_RANK], qan_ref[...])
    ckv = _rms(b[:, B_Q_RANK:B_Q_RANK + B_KV_RANK], kvan_ref[...])
    kr = b[:, B_Q_RANK + B_KV_RANK:]
    indq = indq_ref[...]
    invn = invn_ref[...]

    q = _dot(cq, wqb_ref[...])
    qn = q * lax.rsqrt(_dot_sel(q * q, indq) * invn + EPS) * qw_ref[...]
    scale = (B_NOPE + B_ROPE) ** -0.5
    pieces = []
    for h in range(B_HEADS):
        x = qn[:, h * HEAD_PAD:(h + 1) * HEAD_PAD]
        half = B_ROPE // 2
        r = (x * tab[:, 0:LANE] + pltpu.roll(x, LANE - half, 1) * tab[:, LANE:2 * LANE]
             + pltpu.roll(x, half, 1) * tab[:, 2 * LANE:3 * LANE])
        pieces.append(r * scale)
    q_out[...] = jnp.concatenate(pieces, axis=1).astype(BF16)

    kk = _dot(ckv, wk_ref[...])
    kn = kk * lax.rsqrt(_dot_sel(kk * kk, indq) * invn + EPS) * kw_ref[...]
    krn = kr * lax.rsqrt(jnp.sum(kr * kr, axis=-1, keepdims=True) * (1.0 / B_ROPE) + EPS) * knr_ref[...]
    half = B_ROPE // 2
    krr = (krn * tab[:, 3 * LANE:4 * LANE] + pltpu.roll(krn, LANE - half, 1) * tab[:, 4 * LANE:5 * LANE]
           + pltpu.roll(krn, half, 1) * tab[:, 5 * LANE:6 * LANE])
    k_out[...] = (kn + jnp.dot(krr.astype(BF16), place_ref[...], preferred_element_type=F32)).astype(BF16)
    v_out[...] = _dot(ckv, wv_ref[...]).astype(BF16)


def _mla_prep(b_in, tab, qan, kvan, wqb, wk, wv, qw, kw, knr, indq, invn, place):
    B, L, _ = b_in.shape
    tm = TOK_TILE
    HW = B_HEADS * HEAD_PAD
    return pl.pallas_call(
        _mla_prep_kernel,
        grid=(B, L // tm),
        in_specs=[pl.BlockSpec((None, tm, B_PAD), lambda b, j: (b, j, 0)),
                  pl.BlockSpec((tm, 6 * LANE), lambda b, j: (j, 0)),
                  _full((1, B_Q_RANK)), _full((1, B_KV_RANK)), _full((B_Q_RANK, HW)), _full((B_KV_RANK, HW)),
                  _full((B_KV_RANK, B_WIDTH)), _full((1, HW)), _full((1, HW)), _full((1, LANE)),
                  _full((HW, HW)), _full((1, HW)), _full((LANE, HW))],
        out_specs=[pl.BlockSpec((None, tm, HW), lambda b, j: (b, j, 0)),
                   pl.BlockSpec((None, tm, HW), lambda b, j: (b, j, 0)),
                   pl.BlockSpec((None, tm, B_WIDTH), lambda b, j: (b, j, 0))],
        out_shape=[jax.ShapeDtypeStruct((B, L, HW), BF16), jax.ShapeDtypeStruct((B, L, HW), BF16),
                   jax.ShapeDtypeStruct((B, L, B_WIDTH), BF16)],
        compiler_params=_cparams(("parallel", "parallel")),
        name="mla_prep",
    )(b_in, tab, qan, kvan, wqb, wk, wv, qw, kw, knr, indq, invn, place)


def _attn_kernel(q_ref, k_ref, v_ref, o_ref, v_aug):
    @pl.when(pl.program_id(2) == 0)
    def _():
        lane = lax.broadcasted_iota(jnp.int32, (v_aug.shape[0], LANE), 1)
        v_aug[:, 0:2 * B_V] = v_ref[...]
        v_aug[:, 2 * B_V:] = jnp.where(lane == 0, 1.0, 0.0).astype(BF16)

    scores = []
    for hh in range(2):
        q = q_ref[:, hh * HEAD_PAD:(hh + 1) * HEAD_PAD]
        k = k_ref[:, hh * HEAD_PAD:(hh + 1) * HEAD_PAD]
        scores.append(lax.dot_general(q, k, (((1,), (1,)), ((), ())), preferred_element_type=F32))
    outs = []
    for s in scores:
        p = jnp.exp((s - jnp.max(s, axis=-1, keepdims=True)).astype(BF16))
        pv = jnp.dot(p, v_aug[...], preferred_element_type=F32)
        outs.append(pv[:, 0:2 * B_V] / pv[:, 2 * B_V:2 * B_V + 1])
    lane = lax.broadcasted_iota(jnp.int32, outs[0].shape, 1)
    o_ref[...] = jnp.where(lane < B_V, outs[0], outs[1])


def _attn(q, k, v):
    B, L, _ = q.shape
    tq = min(1024, L)
    return pl.pallas_call(
        _attn_kernel,
        grid=(B, B_HEADS // 2, L // tq),
        in_specs=[pl.BlockSpec((None, tq, 2 * HEAD_PAD), lambda b, p, j: (b, j, p)),
                  pl.BlockSpec((None, L, 2 * HEAD_PAD), lambda b, p, j: (b, 0, p)),
                  pl.BlockSpec((None, L, 2 * B_V), lambda b, p, j: (b, 0, p))],
        out_specs=pl.BlockSpec((None, tq, 2 * B_V), lambda b, p, j: (b, j, p)),
        out_shape=jax.ShapeDtypeStruct((B, L, B_WIDTH), F32),
        scratch_shapes=[pltpu.VMEM((L, 2 * B_V + LANE), BF16)],
        compiler_params=_cparams(("parallel", "parallel", "arbitrary")),
        name="mla_attn",
    )(q, k, v)


def _hgrn_tables():
    C = CHUNK
    m_all = np.zeros((8 * C, C), np.float32)
    masks = np.zeros((7, C, C), np.float32)
    for lvl in range(6):
        m = 1 << lvl
        for t in range(C):
            bd = (t // (2 * m)) * 2 * m + m - 1
            if t % (2 * m) >= m:
                m_all[lvl * C + t, bd + 1:t + 1] = 1.0
            else:
                m_all[lvl * C + t, t + 1:bd + 1] = 1.0
            for s in range(C):
                if t // (2 * m) == s // (2 * m) and t % (2 * m) >= m and s % (2 * m) < m:
                    masks[lvl, t, s] = 1.0
    for t in range(C):
        m_all[6 * C + t, :t + 1] = 1.0
        m_all[7 * C + t, t + 1:] = 1.0
    masks[6] = np.eye(C, dtype=np.float32)
    m_f = m_all.reshape(8, C, C)
    m_b = m_f[:, ::-1, ::-1]
    mk_b = masks[:, ::-1, ::-1]
    m2 = np.stack([m_f.reshape(8 * C, C), m_b.reshape(8 * C, C)])
    return (np.concatenate([m2, m2], axis=2), np.stack([masks, mk_b]))


def _hgrn_kernel(q_ref, lff_ref, lfb_ref, kf_ref, kb_ref, i_ref, gate_ref, cnorm_ref, mall_ref, masks_ref, ind_ref,
                 out_ref, of_s, ob_s, s_s, *, L):
    nc = L // CHUNK
    C = CHUNK

    def wide(c, d):
        t0 = pl.multiple_of(c * C, C)
        logf = (lff_ref if d == 0 else lfb_ref)[pl.ds(t0, C), :]
        e_all = jnp.exp(_sel2_dot(mall_ref[d], logf))
        return dict(t0=t0, q=q_ref[pl.ds(t0, C), :], k=(kf_ref if d == 0 else kb_ref)[pl.ds(t0, C), :],
                    e=e_all, v=i_ref[pl.ds(t0, C), :])

    unroll = 2

    def scan_group(i, carry):
        wd = [[wide(unroll * i + u, 0), wide(nc - 1 - (unroll * i + u), 1)] for u in range(unroll)]
        dh = [(d, h) for d in (0, 1) for h in range(C_HEADS)]
        ks = [(u, d, h) for u in range(unroll) for d, h in dh]
        ls = lambda h: slice(h * C_EXPAND, (h + 1) * C_EXPAND)
        vs = lambda h: slice(h * C_HEAD_DIM, (h + 1) * C_HEAD_DIM)
        qb = {(u, d, h): wd[u][d]['q'][:, ls(h)] for u, d, h in ks}
        kb = {(u, d, h): wd[u][d]['k'][:, ls(h)] for u, d, h in ks}
        vh = {(u, d, h): wd[u][d]['v'][:, vs(h)] for u, d, h in ks}
        lvl_e = lambda u, d, h, j: wd[u][d]['e'][j * C:(j + 1) * C, ls(h)]
        attn = {(u, d, h): _dot_nt(qb[u, d, h], kb[u, d, h]) * masks_ref[d, 6] for u, d, h in ks}
        for lvl in range(6):
            for u, d, h in ks:
                e = lvl_e(u, d, h, lvl).astype(BF16)
                attn[u, d, h] = attn[u, d, h] + _dot_nt(qb[u, d, h] * e, kb[u, d, h] * e) * masks_ref[d, lvl]
        intra = {k: _dot(attn[k], vh[k]) for k in ks}
        qin = {(u, d, h): qb[u, d, h] * lvl_e(u, d, h, 6).astype(BF16) for u, d, h in ks}
        kout = {(u, d, h): _dot_tn(vh[u, d, h], kb[u, d, h] * lvl_e(u, d, h, 7).astype(BF16)) for u, d, h in ks}
        st = {(d, h): s_s[d * C_HEADS + h] for d, h in dh}
        for u in range(unroll):
            o = {(d, h): intra[u, d, h] + _dot_nt(qin[u, d, h], st[d, h]) for d, h in dh}
            for d, h in dh:
                e_in = lvl_e(u, d, h, 6)
                st[d, h] = st[d, h] * (e_in[C - 1:C] if d == 0 else e_in[0:1]) + kout[u, d, h]
            of_s[pl.ds(wd[u][0]['t0'], C), :] = jnp.concatenate([o[0, h] for h in range(C_HEADS)], axis=1)
            ob_s[pl.ds(wd[u][1]['t0'], C), :] = jnp.concatenate([o[1, h] for h in range(C_HEADS)], axis=1)
        for d, h in dh:
            s_s[d * C_HEADS + h] = st[d, h]
        return carry

    s_s[...] = jnp.zeros_like(s_s)
    lax.fori_loop(0, nc // unroll, scan_group, 0)

    rows = 256
    ind = ind_ref[...]

    def finish(i, carry):
        t0 = pl.multiple_of(i * rows, rows)
        o = of_s[pl.ds(t0, rows), :] + ob_s[pl.ds(t0, rows), :]
        ms = _dot_sel(o * o, ind) * (1.0 / C_HEAD_DIM)
        o = o * lax.rsqrt(ms + EPS) * cnorm_ref[...]
        out_ref[pl.ds(t0, rows), :] = o * _silu(gate_ref[pl.ds(t0, rows), :])
        return carry

    lax.fori_loop(0, L // rows, finish, 0)


def _hgrn(q, logf, kk, v, gate, cnorm, mall, masks, ind):
    B, L, _ = q.shape
    kern = functools.partial(_hgrn_kernel, L=L)
    one = pl.Buffered(1)
    F, W = C_FDIM, C_WIDTH
    seq = lambda w, j: pl.BlockSpec((None, L, w), lambda b: (b, 0, j), pipeline_mode=one)
    return pl.pallas_call(
        kern,
        grid=(B,),
        in_specs=[seq(F, 0), seq(F, 0), seq(F, 1), seq(F, 0), seq(F, 1), seq(W, 0), seq(W, 0),
                  _full((1, W)), _full((2, 8 * CHUNK, 2 * CHUNK)), _full((2, 7, CHUNK, CHUNK)), _full((W, W))],
        out_specs=pl.BlockSpec((None, L, W), lambda b: (b, 0, 0)),
        out_shape=jax.ShapeDtypeStruct((B, L, W), F32),
        scratch_shapes=[pltpu.VMEM((L, W), F32), pltpu.VMEM((L, W), F32),
                        pltpu.VMEM((2 * C_HEADS, C_HEAD_DIM, C_EXPAND), F32)],
        compiler_params=_cparams(("parallel",)),
        name="hgrn2",
    )(q, logf, logf, kk, kk, v, gate, cnorm, mall, masks, ind)


def _outproj_kernel(x_ref, oa_ref, ob_ref, oc_ref, wa_ref, wb_ref, wc_ref, ln_ref, wr_ref, br_ref, tri_ref,
                    x1_ref, h_ref, route_ref, cnt_ref, cnt_s):
    x1 = x_ref[...] + _dot(oa_ref[...], wa_ref[...]) + _dot(ob_ref[...], wb_ref[...]) + _dot(oc_ref[...], wc_ref[...])
    x1_ref[...] = x1
    h = _rms(x1, ln_ref[...])
    h_ref[...] = h
    hh, hl = _split(h)
    wr = wr_ref[...]
    wh, wl = _split(wr)
    lg = (jnp.dot(hh, wh, preferred_element_type=F32) + jnp.dot(hh, wl, preferred_element_type=F32)
          + jnp.dot(hl, wh, preferred_element_type=F32)) + br_ref[...]
    lane = lax.broadcasted_iota(jnp.int32, lg.shape, 1).astype(F32)
    neg = -1e30
    far = 1e9
    gmask = lane < N_GROUPS
    gl = jnp.where(gmask, lg, neg)
    gmax = jnp.max(gl, axis=-1, keepdims=True)
    g_idx = jnp.min(jnp.where(gl == gmax, lane, far), axis=-1, keepdims=True)
    g_w = 1.0 / jnp.sum(jnp.where(gmask, jnp.exp(gl - gmax), 0.0), axis=-1, keepdims=True)
    e_lane = lane - N_GROUPS
    emask = (e_lane >= g_idx * EXPERTS_PER_GROUP) & (e_lane < (g_idx + 1.0) * EXPERTS_PER_GROUP)
    el = jnp.where(emask, lg, neg)
    m1 = jnp.max(el, axis=-1, keepdims=True)
    i1 = jnp.min(jnp.where(el == m1, lane, far), axis=-1, keepdims=True)
    el2 = jnp.where(lane == i1, neg, el)
    m2 = jnp.max(el2, axis=-1, keepdims=True)
    i2 = jnp.min(jnp.where((el2 == m2) & emask & (lane != i1), lane, far), axis=-1, keepdims=True)
    r = jnp.exp(m2 - m1)
    w1 = g_w / (1.0 + r)
    w2 = g_w * r / (1.0 + r)
    @pl.when(pl.program_id(0) == 0)
    def _():
        cnt_s[...] = jnp.zeros_like(cnt_s)

    oh1 = jnp.where(lane == i1, 1.0, 0.0)
    oh2 = jnp.where(lane == i2, 1.0, 0.0)
    tri = tri_ref[...]
    run = cnt_s[0:1, :]
    tot1 = jnp.sum(oh1, axis=0, keepdims=True)
    c1 = run + jnp.dot(tri, oh1.astype(BF16), preferred_element_type=F32)
    c2 = run + tot1 + jnp.dot(tri, oh2.astype(BF16), preferred_element_type=F32)
    r1 = jnp.sum(oh1 * c1, axis=-1, keepdims=True)
    r2 = jnp.sum(oh2 * c2, axis=-1, keepdims=True)
    cnt = jnp.broadcast_to(run + tot1 + jnp.sum(oh2, axis=0, keepdims=True), cnt_s.shape)
    cnt_s[...] = cnt
    cnt_ref[...] = cnt
    vals = (i1 - N_GROUPS, i2 - N_GROUPS, w1, w2, r1, r2)
    route = jnp.zeros_like(lg)
    for j, v in enumerate(vals):
        route = jnp.where(lane == float(j), v, route)
    route_ref[...] = route


def _outproj(x, oa, ob, oc, wa, wb, wc, ln, wr, br, tri):
    T = x.shape[0]
    tm = TOK_TILE
    row = lambda w: pl.BlockSpec((tm, w), lambda i: (i, 0))
    return pl.pallas_call(
        _outproj_kernel,
        grid=(T // tm,),
        in_specs=[row(D_MODEL), row(A_WIDTH), row(B_WIDTH), row(C_WIDTH),
                  _full((A_WIDTH, D_MODEL)), _full((B_WIDTH, D_MODEL)), _full((C_WIDTH, D_MODEL)),
                  _full((1, D_MODEL)), _full((D_MODEL, LANE)), _full((1, LANE)), _full((tm, tm))],
        out_specs=[row(D_MODEL), row(D_MODEL), row(LANE), _full((8, LANE))],
        out_shape=[jax.ShapeDtypeStruct((T, D_MODEL), F32), jax.ShapeDtypeStruct((T, D_MODEL), F32),
                   jax.ShapeDtypeStruct((T, LANE), F32), jax.ShapeDtypeStruct((8, LANE), F32)],
        scratch_shapes=[pltpu.VMEM((8, LANE), F32)],
        compiler_params=_cparams(("arbitrary",)),
        name="outproj_router",
    )(x, oa, ob, oc, wa, wb, wc, ln, wr, br, tri)


def _gather_rows(src_hbm, buf, sem, cur_ids, nxt_ids, n_rows):
    i = pl.program_id(0)
    slot = i % 2

    def row_copy(ids, r, s):
        return pltpu.make_async_copy(src_hbm.at[pl.ds(ids[0, 0, r], 1)], buf.at[s, pl.ds(r, 1)], sem.at[s])

    def start_all(ids, s):
        def body(r, carry):
            row_copy(ids, r, s).start()
            return carry
        lax.fori_loop(0, n_rows, body, 0, unroll=8)

    def wait_all(s):
        pltpu.make_async_copy(src_hbm.at[pl.ds(0, n_rows)], buf.at[s], sem.at[s]).wait()

    @pl.when(i == 0)
    def _():
        start_all(cur_ids, 0)

    start_all(nxt_ids, 1 - slot)
    wait_all(slot)

    @pl.when(i == pl.num_programs(0) - 1)
    def _():
        wait_all(1 - slot)

    return slot


def _scatter_kernel(pends_ref, pos_ref, h_ref, xs_hbm, zero_s, sem):
    tm = TOK_TILE

    @pl.when(pl.program_id(0) == 0)
    def _():
        zero_s[...] = jnp.zeros_like(zero_s)

        def fill(e):
            start = pl.multiple_of(jnp.maximum(pends_ref[e] - MOE_ROWS, 0), MOE_ROWS)
            return pltpu.make_async_copy(zero_s, xs_hbm.at[pl.ds(start, MOE_ROWS)], sem)

        for e in range(N_EXPERTS):
            fill(e).start()
            fill(e).wait()

        def fill_tail(b, carry):
            cp = pltpu.make_async_copy(zero_s, xs_hbm.at[pl.ds(pl.multiple_of(b * MOE_ROWS, MOE_ROWS), MOE_ROWS)], sem)
            cp.start()
            cp.wait()
            return carry

        lax.fori_loop(pends_ref[N_EXPERTS - 1] // MOE_ROWS, xs_hbm.shape[0] // MOE_ROWS, fill_tail, 0)

    def row_copy(r, k):
        return pltpu.make_async_copy(h_ref.at[pl.ds(r, 1)], xs_hbm.at[pl.ds(pos_ref[0, 0, k * tm + r], 1)], sem)

    def issue(r, carry):
        for k in range(TOP_K):
            row_copy(r, k).start()
        return carry

    lax.fori_loop(0, tm, issue, 0, unroll=8)
    for k in range(TOP_K):
        pltpu.make_async_copy(h_ref, xs_hbm.at[pl.ds(0, tm)], sem).wait()


def _scatter(pends, pos, h):
    T = h.shape[0]
    tm = TOK_TILE
    grid_spec = pltpu.PrefetchScalarGridSpec(
        num_scalar_prefetch=1,
        grid=(T // tm,),
        in_specs=[pl.BlockSpec((1, 1, TOP_K * tm), lambda i, pe: (i, 0, 0), memory_space=pltpu.SMEM),
                  pl.BlockSpec((tm, D_MODEL), lambda i, pe: (i, 0))],
        out_specs=pl.BlockSpec(memory_space=pl.ANY),
        scratch_shapes=[pltpu.VMEM((MOE_ROWS, D_MODEL), F32), pltpu.SemaphoreType.DMA(())],
    )
    return pl.pallas_call(
        _scatter_kernel,
        grid_spec=grid_spec,
        out_shape=jax.ShapeDtypeStruct((T * TOP_K + N_EXPERTS * MOE_ROWS, D_MODEL), F32),
        compiler_params=_cparams(("arbitrary",)),
        name="moe_scatter",
    )(pends, pos, h)


def _moe_kernel(blk_e_ref, nact_ref, x_ref, w1_ref, w3_ref, w2_ref, y_ref):
    active = pl.program_id(0) < nact_ref[0]

    @pl.when(active)
    def _():
        x = x_ref[...].astype(BF16)
        a = jnp.dot(x, w1_ref[...], preferred_element_type=F32)
        g = jnp.dot(x, w3_ref[...], preferred_element_type=F32)
        y_ref[...] = jnp.dot((_silu(a) * g).astype(BF16), w2_ref[...], preferred_element_type=F32)

    @pl.when(jnp.logical_not(active))
    def _():
        y_ref[...] = jnp.zeros_like(y_ref)


def _moe(blk_e, nact, xs, w1, w3, w2):
    P = xs.shape[0]
    nblk = P // MOE_ROWS
    grid_spec = pltpu.PrefetchScalarGridSpec(
        num_scalar_prefetch=2,
        grid=(nblk,),
        in_specs=[pl.BlockSpec((MOE_ROWS, D_MODEL), lambda i, be, na: (jnp.minimum(i, na[0] - 1), 0)),
                  pl.BlockSpec((None, D_MODEL, D_EXPERT), lambda i, be, na: (be[i], 0, 0)),
                  pl.BlockSpec((None, D_MODEL, D_EXPERT), lambda i, be, na: (be[i], 0, 0)),
                  pl.BlockSpec((None, D_EXPERT, D_MODEL), lambda i, be, na: (be[i], 0, 0))],
        out_specs=pl.BlockSpec((MOE_ROWS, D_MODEL), lambda i, be, na: (i, 0)),
    )
    return pl.pallas_call(
        _moe_kernel,
        grid_spec=grid_spec,
        out_shape=jax.ShapeDtypeStruct((P, D_MODEL), F32),
        compiler_params=_cparams(("arbitrary",)),
        name="moe_ffn",
    )(blk_e, nact, xs, w1, w3, w2)


def _combine_kernel(pos_ref, nxt_ref, x_ref, route_ref, y_hbm, out_ref, yb, sem):
    tm = x_ref.shape[0]
    slot = _gather_rows(y_hbm, yb, sem, pos_ref, nxt_ref, TOP_K * tm)
    route = route_ref[...]
    out_ref[...] = x_ref[...] + yb[slot, 0:tm, :] * route[:, 2:3] + yb[slot, tm:2 * tm, :] * route[:, 3:4]


def _combine(pos, x1, route, y):
    T = x1.shape[0]
    tm = TOK_TILE
    nt = T // tm
    return pl.pallas_call(
        _combine_kernel,
        grid=(nt,),
        in_specs=[pl.BlockSpec((1, 1, TOP_K * tm), lambda i: (i, 0, 0), memory_space=pltpu.SMEM),
                  pl.BlockSpec((1, 1, TOP_K * tm), lambda i: (jnp.minimum(i + 1, nt - 1), 0, 0),
                               memory_space=pltpu.SMEM),
                  pl.BlockSpec((tm, D_MODEL), lambda i: (i, 0)),
                  pl.BlockSpec((tm, LANE), lambda i: (i, 0)),
                  pl.BlockSpec(memory_space=pl.ANY)],
        out_specs=pl.BlockSpec((tm, D_MODEL), lambda i: (i, 0)),
        out_shape=jax.ShapeDtypeStruct((T, D_MODEL), F32),
        scratch_shapes=[pltpu.VMEM((2, TOP_K * tm, D_MODEL), F32), pltpu.SemaphoreType.DMA((2,))],
        compiler_params=_cparams(("arbitrary",)),
        name="moe_combine",
    )(pos, pos, x1, route, y)


def _dispatch(route, cnt):
    T = route.shape[0]
    nblk = (T * TOP_K) // MOE_ROWS + N_EXPERTS
    counts = cnt[0, N_GROUPS:N_GROUPS + N_EXPERTS].astype(jnp.int32)
    padded = (counts + MOE_ROWS - 1) // MOE_ROWS * MOE_ROWS
    pends = jnp.cumsum(padded)
    pstarts = pends - padded
    e = route[:, 0:TOP_K].astype(jnp.int32)
    rank = route[:, 4:4 + TOP_K].astype(jnp.int32)
    sel = e[:, :, None] == jnp.arange(N_EXPERTS, dtype=jnp.int32)
    pos = rank + jnp.sum(jnp.where(sel, pstarts, 0), axis=-1)
    pos = jnp.transpose(pos.reshape(T // TOK_TILE, TOK_TILE, TOP_K), (0, 2, 1))
    pos = pos.reshape(T // TOK_TILE, 1, TOP_K * TOK_TILE)
    blk_start = jnp.arange(nblk, dtype=jnp.int32) * MOE_ROWS
    blk_e = jnp.minimum(jnp.sum(blk_start[:, None] >= pends[None, :], axis=-1), N_EXPERTS - 1).astype(jnp.int32)
    nact = (pends[-1:] // MOE_ROWS).astype(jnp.int32)
    return blk_e, nact, pos, pends.astype(jnp.int32)


def _block_ind(width, seg):
    i = np.arange(width)
    return (i[:, None] // seg == i[None, :] // seg).astype(np.float32)


def _rope_tables(L):
    half = B_ROPE // 2
    inv = ROPE_THETA ** (-jnp.arange(0, B_ROPE, 2, dtype=F32) / B_ROPE)
    ang = jnp.arange(L, dtype=F32)[:, None] * inv[None, :]
    cos, sin = jnp.cos(ang), jnp.sin(ang)
    z = lambda n: jnp.zeros((L, n), F32)
    one = jnp.ones((L, B_NOPE), F32)
    rest = LANE - B_NOPE - B_ROPE
    cq = jnp.concatenate([one, cos, cos, z(rest)], axis=1)
    s1q = jnp.concatenate([z(B_NOPE), -sin, z(half), z(rest)], axis=1)
    s2q = jnp.concatenate([z(B_NOPE), z(half), sin, z(rest)], axis=1)
    ck = jnp.concatenate([cos, cos, z(LANE - B_ROPE)], axis=1)
    s1k = jnp.concatenate([-sin, z(LANE - half)], axis=1)
    s2k = jnp.concatenate([z(half), sin, z(LANE - B_ROPE)], axis=1)
    return jnp.concatenate([cq, s1q, s2q, ck, s1k, s2k], axis=1)


def _layer_params(l, ln1, w_in, conv_w, a_log, dt_bias, a_norm, q_a_norm, w_qb, kv_a_norm, w_kvb, qn_nope, qn_rope,
                  kn_nope, kn_rope, lbs, c_norm, w_out, ln2, w_group, b_group, w_router, b_router, w1, w3, w2):
    p = {}
    w = w_in[l]
    wp = jnp.zeros((D_MODEL, N_PAD), F32)
    wp = wp.at[:, 0:A_IN].set(w[:, 0:A_IN])
    wp = wp.at[:, A_PAD:A_PAD + B_IN].set(w[:, A_IN:A_IN + B_IN])
    wp = wp.at[:, A_PAD + B_PAD:].set(w[:, A_IN + B_IN:])
    p['w_in'] = wp.astype(BF16)
    p['ln1'] = ln1[l].reshape(1, D_MODEL)
    p['convw'] = jnp.zeros((8, 3 * A_WIDTH), F32).at[0:CONV_K].set(conv_w[l])
    gpad = lambda v: jnp.zeros((1, LANE), F32).at[0, 2 * A_HEADS:4 * A_HEADS].set(v.reshape(-1))
    p['alog'] = gpad(a_log[l])
    p['dtb'] = gpad(dt_bias[l])
    p['anorm'] = jnp.tile(a_norm[l], A_HEADS).reshape(1, A_WIDTH)
    half = B_ROPE // 2
    wq = w_qb[l].reshape(B_Q_RANK, B_HEADS, B_NOPE + B_ROPE)
    wq = jnp.pad(wq, ((0, 0), (0, 0), (0, HEAD_PAD - B_NOPE - B_ROPE)))
    p['wqb'] = wq.reshape(B_Q_RANK, B_HEADS * HEAD_PAD).astype(BF16)
    wkv = w_kvb[l].reshape(B_KV_RANK, B_HEADS, B_NOPE + B_V)
    wk = jnp.pad(wkv[:, :, :B_NOPE], ((0, 0), (0, 0), (0, HEAD_PAD - B_NOPE)))
    p['wk'] = wk.reshape(B_KV_RANK, B_HEADS * HEAD_PAD).astype(BF16)
    p['wv'] = wkv[:, :, B_NOPE:].reshape(B_KV_RANK, B_WIDTH).astype(BF16)
    zpad = jnp.zeros((HEAD_PAD - B_NOPE - B_ROPE,), F32)
    p['qw'] = jnp.tile(jnp.concatenate([qn_nope[l], qn_rope[l], zpad]), B_HEADS).reshape(1, -1)
    p['kw'] = jnp.tile(jnp.concatenate([kn_nope[l], jnp.zeros((HEAD_PAD - B_NOPE,), F32)]), B_HEADS).reshape(1, -1)
    p['knr'] = jnp.concatenate([kn_rope[l], jnp.zeros((LANE - B_ROPE,), F32)]).reshape(1, LANE)
    p['qan'] = q_a_norm[l].reshape(1, -1)
    p['kvan'] = kv_a_norm[l].reshape(1, -1)
    p['lb'] = lbs[l]
    p['cnorm'] = jnp.tile(c_norm[l], C_HEADS).reshape(1, C_WIDTH)
    wo = w_out[l].astype(BF16)
    p['wa'], p['wb'], p['wc'] = wo[:A_WIDTH], wo[A_WIDTH:A_WIDTH + B_WIDTH], wo[A_WIDTH + B_WIDTH:]
    p['ln2'] = ln2[l].reshape(1, D_MODEL)
    wr = jnp.zeros((D_MODEL, LANE), F32).at[:, :N_GROUPS].set(w_group[l])
    p['wr'] = wr.at[:, N_GROUPS:N_GROUPS + N_EXPERTS].set(w_router[l])
    br = jnp.zeros((1, LANE), F32).at[0, :N_GROUPS].set(b_group[l])
    p['br'] = br.at[0, N_GROUPS:N_GROUPS + N_EXPERTS].set(b_router[l])
    p['w1'], p['w3'], p['w2'] = w1[l].astype(BF16), w3[l].astype(BF16), w2[l].astype(BF16)
    return p


def _static_tables():
    t = {}
    t['ind_a'] = jnp.asarray(_block_ind(A_WIDTH, A_HEAD_DIM), BF16)
    t['ind_c'] = jnp.asarray(_block_ind(C_WIDTH, C_HEAD_DIM), BF16)
    expb = np.zeros((2, LANE, A_WIDTH), np.float32)
    expg = np.zeros((2, LANE, A_WIDTH), np.float32)
    for d in range(2):
        for h in range(A_HEADS):
            expb[d, A_HEADS * d + h, h * A_HEAD_DIM:(h + 1) * A_HEAD_DIM] = 1.0
            expg[d, 2 * A_HEADS + A_HEADS * d + h, h * A_HEAD_DIM:(h + 1) * A_HEAD_DIM] = 1.0
    t['expb'] = jnp.asarray(expb, BF16)
    t['expg'] = jnp.asarray(expg, BF16)
    t['pick'] = jnp.asarray(np.repeat(np.transpose(expg[:, :, ::A_HEAD_DIM], (0, 2, 1)), CHUNK, axis=1), BF16)
    HW = B_HEADS * HEAD_PAD
    lane = np.arange(HW)
    seg = np.where(lane % HEAD_PAD < B_NOPE, 0, np.where(lane % HEAD_PAD < B_NOPE + B_ROPE, 1, 2))
    head = lane // HEAD_PAD
    indq = (head[:, None] == head[None, :]) & (seg[:, None] == seg[None, :]) & (seg[:, None] < 2)
    t['indq'] = jnp.asarray(indq.astype(np.float32), BF16)
    t['invn'] = jnp.asarray(np.where(seg == 0, 1.0 / B_NOPE, 1.0 / B_ROPE).astype(np.float32)).reshape(1, HW)
    place = np.zeros((LANE, HW), np.float32)
    for h in range(B_HEADS):
        for r in range(B_ROPE):
            place[r, h * HEAD_PAD + B_NOPE + r] = 1.0
    t['place'] = jnp.asarray(place, BF16)
    t['tri'] = jnp.asarray(np.tril(np.ones((TOK_TILE, TOK_TILE), np.float32), -1), BF16)
    mall, masks = _hgrn_tables()
    t['mall'] = jnp.asarray(mall, BF16)
    t['masks'] = jnp.asarray(masks, F32)
    return t


def _trunk(x, params, tabs, rope):
    B, L, D = x.shape
    T = B * L
    xf = x.reshape(T, D)
    for p in params:
        a_in, b_in, c_q, c_logf, c_k, c_v, c_gate = _proj(xf, p['ln1'], p['w_in'], p['lb'])
        o_a = _gdn(a_in.reshape(B, L, A_PAD), p['convw'], p['alog'], p['dtb'], p['anorm'], tabs['ind_a'],
                   tabs['expb'], tabs['expg'], tabs['pick'])
        q, k, v = _mla_prep(b_in.reshape(B, L, B_PAD), rope, p['qan'], p['kvan'], p['wqb'], p['wk'], p['wv'],
                            p['qw'], p['kw'], p['knr'], tabs['indq'], tabs['invn'], tabs['place'])
        o_b = _attn(q, k, v)
        seq = lambda t: t.reshape(B, L, t.shape[-1])
        o_c = _hgrn(seq(c_q), seq(c_logf), seq(c_k), seq(c_v), seq(c_gate), p['cnorm'], tabs['mall'],
                    tabs['masks'], tabs['ind_c'])
        x1, h, route, cnt = _outproj(xf, o_a.reshape(T, A_WIDTH), o_b.reshape(T, B_WIDTH), o_c.reshape(T, C_WIDTH),
                                     p['wa'], p['wb'], p['wc'], p['ln2'], p['wr'], p['br'], tabs['tri'])
        blk_e, nact, pos, pends = _dispatch(route, cnt)
        xs = _scatter(pends, pos, h)
        y = _moe(blk_e, nact, xs, p['w1'], p['w3'], p['w2'])
        xf = _combine(pos, x1, route, y)
    return xf.reshape(B, L, D)


def kernel(x_prompt, x_sample, ln1, w_in, conv_w, a_log, dt_bias, a_norm, q_a_norm, w_qb, kv_a_norm, w_kvb, qn_nope,
           qn_rope, kn_nope, kn_rope, c_lower_bounds, c_norm, w_out, ln2, w_group, b_group, w_router, b_router,
           w1, w3, w2):
    lbs = jnp.cumsum(jax.nn.softmax(c_lower_bounds.astype(F32), axis=0), axis=0)
    lbs = lbs - lbs[0:1]
    params = [_layer_params(l, ln1, w_in, conv_w, a_log, dt_bias, a_norm, q_a_norm, w_qb, kv_a_norm, w_kvb, qn_nope,
                            qn_rope, kn_nope, kn_rope, lbs, c_norm, w_out, ln2, w_group, b_group, w_router, b_router,
                            w1, w3, w2) for l in range(DEPTH)]
    tabs = _static_tables()
    assert x_prompt.shape[1:] == x_sample.shape[1:]
    rope = _rope_tables(x_prompt.shape[1])
    nb = x_prompt.shape[0]
    y = _trunk(jnp.concatenate([x_prompt, x_sample], axis=0), params, tabs, rope)
    return (y[:nb], y[nb:])
```

```python
import functools
import math

import numpy as np
import jax
import jax.numpy as jnp
from jax import lax
from jax.experimental import pallas as pl
from jax.experimental.pallas import tpu as pltpu

F32 = jnp.float32
BF16 = jnp.bfloat16

D_MODEL = 1024
DEPTH = 2
EPS = 1e-6
A_HEADS, A_HEAD_DIM, CONV_K = 6, 64, 5
A_WIDTH = A_HEADS * A_HEAD_DIM
B_HEADS, B_Q_RANK, B_KV_RANK, B_NOPE, B_ROPE, B_V = 6, 256, 128, 64, 32, 64
B_WIDTH = B_HEADS * B_V
ROPE_THETA = 10000.0
C_HEADS, C_EXPAND, C_HEAD_DIM = 4, 128, 64
C_WIDTH = C_HEADS * C_HEAD_DIM
C_FDIM = C_HEADS * C_EXPAND
D_MIX = A_WIDTH + B_WIDTH + C_WIDTH
A_IN = 4 * A_WIDTH + 4 * A_HEADS
B_IN = B_Q_RANK + B_KV_RANK + B_ROPE
C_IN = 3 * C_FDIM + 2 * C_WIDTH
N_GROUPS, EXPERTS_PER_GROUP, TOP_K, D_EXPERT = 4, 8, 2, 512
N_EXPERTS = N_GROUPS * EXPERTS_PER_GROUP

LANE = 128
CHUNK = 64
A_PAD = 4 * A_WIDTH + LANE
B_PAD = 512
C_PAD = C_IN
N_PAD = A_PAD + B_PAD + C_PAD
HEAD_PAD = LANE
MOE_ROWS = 256
TOK_TILE = 512
VMEM_LIMIT = 56 * 1024 * 1024


def _cparams(sem):
    return pltpu.CompilerParams(dimension_semantics=sem, vmem_limit_bytes=VMEM_LIMIT)


def _dot(a, b):
    return jnp.dot(a.astype(BF16), b.astype(BF16), preferred_element_type=F32)


def _dot_nt(a, b):
    return lax.dot_general(a.astype(BF16), b.astype(BF16), (((1,), (1,)), ((), ())), preferred_element_type=F32)


def _dot_tn(a, b):
    return lax.dot_general(a.astype(BF16), b.astype(BF16), (((0,), (0,)), ((), ())), preferred_element_type=F32)


def _split(x):
    hi = x.astype(BF16)
    lo = (x - hi.astype(F32)).astype(BF16)
    return hi, lo


def _sel_dot(sel, x):
    hi, lo = _split(x)
    return jnp.dot(sel, hi, preferred_element_type=F32) + jnp.dot(sel, lo, preferred_element_type=F32)


def _sel2_dot(sel2, x):
    hi, lo = _split(x)
    return jnp.dot(sel2, jnp.concatenate([hi, lo], axis=0), preferred_element_type=F32)


def _dot_sel(x, sel):
    hi, lo = _split(x)
    return jnp.dot(hi, sel, preferred_element_type=F32) + jnp.dot(lo, sel, preferred_element_type=F32)


def _sigmoid(x):
    return 1.0 / (1.0 + jnp.exp(-x))


def _silu(x):
    return x * _sigmoid(x)


def _softplus(x):
    return jnp.maximum(x, 0.0) + jnp.log(1.0 + jnp.exp(-jnp.abs(x)))


def _rms(x, w):
    return x * lax.rsqrt(jnp.mean(x * x, axis=-1, keepdims=True) + EPS) * w


def _full(shape):
    return pl.BlockSpec(shape, lambda *_: (0,) * len(shape))


def _pair_specs(n_first, tm):
    return [pl.BlockSpec((tm, D_MODEL), lambda i, *_: (jnp.minimum(i, n_first - 1), 0)),
            pl.BlockSpec((tm, D_MODEL), lambda i, *_: (jnp.maximum(i - n_first, 0), 0))]


def _pair_load(xa_ref, xb_ref, n_first):
    return jnp.where(pl.program_id(0) < n_first, xa_ref[...], xb_ref[...])


def _proj_kernel(xa_ref, xb_ref, ln_ref, w_ref, lb_ref, a_ref, b_ref, cq_ref, clf_ref, ck_ref, cv_ref, cg_ref, *,
                 n_first):
    h = _rms(_pair_load(xa_ref, xb_ref, n_first), ln_ref[...]).astype(BF16)
    a_ref[...] = jnp.dot(h, w_ref[:, 0:A_PAD], preferred_element_type=F32)
    b_ref[...] = jnp.dot(h, w_ref[:, A_PAD:A_PAD + B_PAD], preferred_element_type=F32)
    c0 = A_PAD + B_PAD
    F = C_FDIM
    cq_ref[...] = _silu(jnp.dot(h, w_ref[:, c0:c0 + F], preferred_element_type=F32)).astype(BF16)
    for d in range(2):
        fp = jnp.dot(h, w_ref[:, c0 + (1 + d) * F:c0 + (2 + d) * F], preferred_element_type=F32)
        lb = lb_ref[d:d + 1, :]
        e = jnp.exp(-jnp.abs(fp))
        r = 1.0 / (1.0 + e)
        lsig = jnp.minimum(fp, 0.0) - jnp.log(1.0 + e)
        a = jnp.log(lb)
        bb = jnp.log(1.0 - lb) + lsig
        clf_ref[:, d * F:(d + 1) * F] = jnp.maximum(a, bb) + jnp.log(1.0 + jnp.exp(-jnp.abs(a - bb)))
        ck_ref[:, d * F:(d + 1) * F] = ((1.0 - lb) * jnp.where(fp >= 0.0, e, 1.0) * r).astype(BF16)
    tail = jnp.dot(h, w_ref[:, c0 + 3 * F:N_PAD], preferred_element_type=F32)
    cv_ref[...] = tail[:, :C_WIDTH].astype(BF16)
    cg_ref[...] = tail[:, C_WIDTH:]


def _proj(xa, xb, ln, w_pad, lb):
    tm = TOK_TILE
    n_first = xa.shape[0] // tm
    T = xa.shape[0] + xb.shape[0]
    row = lambda w: pl.BlockSpec((tm, w), lambda i: (i, 0))
    sds = jax.ShapeDtypeStruct
    return pl.pallas_call(
        functools.partial(_proj_kernel, n_first=n_first),
        grid=(T // tm,),
        in_specs=_pair_specs(n_first, tm) + [_full((1, D_MODEL)), _full((D_MODEL, N_PAD)), _full((2, C_FDIM))],
        out_specs=[row(A_PAD), row(B_PAD), row(C_FDIM), row(2 * C_FDIM), row(2 * C_FDIM), row(C_WIDTH), row(C_WIDTH)],
        out_shape=[sds((T, A_PAD), F32), sds((T, B_PAD), F32), sds((T, C_FDIM), BF16), sds((T, 2 * C_FDIM), F32),
                   sds((T, 2 * C_FDIM), BF16), sds((T, C_WIDTH), BF16), sds((T, C_WIDTH), F32)],
        compiler_params=_cparams(("parallel",)),
        name="proj",
    )(xa, xb, ln, w_pad, lb)


def _gdn_kernel(qkv_ref, gate_ref, gates_ref, convw_ref, alog_ref, dtb_ref, anorm_ref, ind_ref, expb_ref, expg_ref,
                pick_ref, out_ref, q_s, k_s, v_s, of_s, ob_s, s_s, *, L):
    nc = L // CHUNK
    W = A_WIDTH
    row = lax.broadcasted_iota(jnp.int32, (CHUNK, CHUNK), 0)
    col = lax.broadcasted_iota(jnp.int32, (CHUNK, CHUNK), 1)
    lane = lax.broadcasted_iota(jnp.int32, (CHUNK, LANE), 1)
    row2 = lax.broadcasted_iota(jnp.int32, (CHUNK, 2 * CHUNK), 0)
    col2 = lax.broadcasted_iota(jnp.int32, (CHUNK, 2 * CHUNK), 1) % CHUNK
    ind = ind_ref[...]

    def conv_chunk(c, carry):
        t0 = pl.multiple_of(c * CHUNK, CHUNK)
        tp = pl.multiple_of(jnp.maximum(t0 - 8, 0), 8)
        tn = pl.multiple_of(jnp.minimum(t0 + CHUNK, L - 8), 8)
        for g in range(3):
            cs = slice(g * W, (g + 1) * W)
            main = qkv_ref[pl.ds(t0, CHUNK), cs]
            prev = jnp.where(c > 0, qkv_ref[pl.ds(tp, 8), cs], 0.0)
            nxt = jnp.where(c < nc - 1, qkv_ref[pl.ds(tn, 8), cs], 0.0)
            win = jnp.concatenate([prev, main, nxt], axis=0)
            acc = win * convw_ref[2:3, cs]
            for j in (0, 1, 3, 4):
                acc = acc + pltpu.roll(win, (2 - j) % (CHUNK + 16), 0) * convw_ref[j:j + 1, cs]
            y = _silu(acc[8:8 + CHUNK])
            if g < 2:
                y = y * lax.rsqrt(_dot_sel(y * y, ind) + EPS)
            if g == 0:
                q_s[pl.ds(t0, CHUNK), :] = (y * (A_HEAD_DIM ** -0.5)).astype(BF16)
            elif g == 1:
                k_s[pl.ds(t0, CHUNK), :] = y.astype(BF16)
            else:
                v_s[pl.ds(t0, CHUNK), :] = y
        return carry

    lax.fori_loop(0, nc, conv_chunk, 0)

    coef = -jnp.exp(alog_ref[...])
    dtb = dtb_ref[...]
    g_lane = (lane >= 2 * A_HEADS) & (lane < 4 * A_HEADS)

    def wide(c, d):
        t0 = pl.multiple_of(c * CHUNK, CHUNK)
        qc = q_s[pl.ds(t0, CHUNK), :].astype(F32)
        kc = k_s[pl.ds(t0, CHUNK), :].astype(F32)
        vc = v_s[pl.ds(t0, CHUNK), :]
        gt = gates_ref[pl.ds(t0, CHUNK), :]
        beta = _sigmoid(gt)
        gl = jnp.where(g_lane, coef * _softplus(gt + dtb), 0.0)
        tri2 = ((col2 <= row2) if d == 0 else (col2 >= row2)).astype(BF16)
        gcum = _sel2_dot(tri2, gl)
        total = gcum[CHUNK - 1:CHUNK] if d == 0 else gcum[0:1]
        e_in = jnp.exp(gcum)
        e_out = jnp.exp(total - gcum)
        g_last = jnp.broadcast_to(jnp.exp(total), (8, LANE))
        expb = expb_ref[d]
        expg = expg_ref[d]
        beta_w = jnp.dot(beta.astype(BF16), expb, preferred_element_type=F32)
        ein_w = jnp.dot(e_in.astype(BF16), expg, preferred_element_type=F32)
        eout_w = jnp.dot(e_out.astype(BF16), expg, preferred_element_type=F32)
        kb = kc * beta_w
        return dict(
            t0=t0, q=qc, k=kc, kb=kb, vb=vc * beta_w, kbe=kb * ein_w, qe=qc * ein_w, kd=kc * eout_w,
            glast=_dot_sel(g_last, expg),
            gcum=gcum, gcum_t=gcum.T,
            incl=(col <= row) if d == 0 else (col >= row),
            strict=(col < row) if d == 0 else (col > row))

    unroll = 2

    def scan_group(i, carry):
        wd = [[wide(unroll * i + u, 0), wide(nc - 1 - (unroll * i + u), 1)] for u in range(unroll)]
        dh = [(d, h) for d in (0, 1) for h in range(A_HEADS)]
        ks = [(u, d, h) for u in range(unroll) for d, h in dh]
        hs = lambda h: slice(h * A_HEAD_DIM, (h + 1) * A_HEAD_DIM)
        dec, n, qk, x = {}, {}, {}, {}
        for u, d, h in ks:
            w = wd[u][d]
            lg = 2 * A_HEADS + A_HEADS * d + h
            dec[u, d, h] = jnp.exp(jnp.minimum(w['gcum'][:, lg:lg + 1] - w['gcum_t'][lg:lg + 1, :], 0.0))
        for u, d, h in ks:
            w = wd[u][d]
            n[u, d, h] = jnp.where(w['strict'], _dot_nt(w['kb'][:, hs(h)], w['k'][:, hs(h)]) * dec[u, d, h], 0.0)
        for u, d, h in ks:
            w = wd[u][d]
            qk[u, d, h] = jnp.where(w['incl'], _dot_nt(w['q'][:, hs(h)], w['k'][:, hs(h)]) * dec[u, d, h], 0.0)
            x[u, d, h] = jnp.concatenate([w['vb'][:, hs(h)], w['kbe'][:, hs(h)]], axis=1)
        xw = 2 * A_HEAD_DIM
        z = {k: _dot(n[k], jnp.concatenate([x[k], n[k]], axis=1)) for k in ks}
        x = {k: x[k] - z[k][:, :xw] for k in ks}
        for _ in range(4):
            p = {k: z[k][:, xw:] for k in ks}
            z = {k: _dot(p[k], jnp.concatenate([x[k], p[k]], axis=1)) for k in ks}
            x = {k: x[k] + z[k][:, :xw] for k in ks}
        x = {k: x[k] + _dot(z[k][:, xw:], x[k]) for k in ks}
        s = {(d, h): s_s[d * A_HEADS + h] for d, h in dh}
        for u in range(unroll):
            v_new = {(d, h): x[u, d, h][:, :A_HEAD_DIM] - _dot(x[u, d, h][:, A_HEAD_DIM:], s[d, h]) for d, h in dh}
            o = {(d, h): _dot(wd[u][d]['qe'][:, hs(h)], s[d, h]) + _dot(qk[u, d, h], v_new[d, h]) for d, h in dh}
            s = {(d, h): (s[d, h] * wd[u][d]['glast'][0:1, hs(h)] + _dot_tn(wd[u][d]['kd'][:, hs(h)], v_new[d, h]))
                 for d, h in dh}
            of_s[pl.ds(wd[u][0]['t0'], CHUNK), :] = jnp.concatenate([o[0, h] for h in range(A_HEADS)], axis=1)
            ob_s[pl.ds(wd[u][1]['t0'], CHUNK), :] = jnp.concatenate([o[1, h] for h in range(A_HEADS)], axis=1)
        for d, h in dh:
            s_s[d * A_HEADS + h] = s[d, h]
        return carry

    s_s[...] = jnp.zeros_like(s_s)
    lax.fori_loop(0, nc // unroll, scan_group, 0)

    rows = 256

    def finish(i, carry):
        t0 = pl.multiple_of(i * rows, rows)
        o = of_s[pl.ds(t0, rows), :] + ob_s[pl.ds(t0, rows), :]
        ms = _dot_sel(o * o, ind) * (1.0 / A_HEAD_DIM)
        o = o * lax.rsqrt(ms + EPS) * anorm_ref[...]
        out_ref[pl.ds(t0, rows), :] = o * _silu(gate_ref[pl.ds(t0, rows), :])
        return carry

    lax.fori_loop(0, L // rows, finish, 0)


def _gdn(a_in, convw, alog, dtb, anorm, ind, expb, expg, pick):
    B, L, _ = a_in.shape
    W = A_WIDTH
    kern = functools.partial(_gdn_kernel, L=L)
    one = None
    return pl.pallas_call(
        kern,
        grid=(B,),
        in_specs=[pl.BlockSpec((None, L, 3 * W), lambda b: (b, 0, 0), pipeline_mode=one),
                  pl.BlockSpec((None, L, W), lambda b: (b, 0, 3), pipeline_mode=one),
                  pl.BlockSpec((None, L, LANE), lambda b: (b, 0, 4 * W // LANE), pipeline_mode=one),
                  _full((8, 3 * W)), _full((1, LANE)), _full((1, LANE)), _full((1, W)), _full((W, W)),
                  _full((2, LANE, W)), _full((2, LANE, W)), _full((2, A_HEADS * CHUNK, LANE))],
        out_specs=pl.BlockSpec((None, L, W), lambda b: (b, 0, 0)),
        out_shape=jax.ShapeDtypeStruct((B, L, W), F32),
        scratch_shapes=[pltpu.VMEM((L, W), BF16), pltpu.VMEM((L, W), BF16), pltpu.VMEM((L, W), F32),
                        pltpu.VMEM((L, W), F32), pltpu.VMEM((L, W), F32),
                        pltpu.VMEM((2 * A_HEADS, A_HEAD_DIM, A_HEAD_DIM), F32)],
        compiler_params=_cparams(("parallel",)),
        name="gdn",
    )(a_in, a_in, a_in, convw, alog, dtb, anorm, ind, expb, expg, pick)


def _mla_prep_kernel(b_ref, tab_ref, qan_ref, kvan_ref, wqb_ref, wk_ref, wv_ref, qw_ref, kw_ref, knr_ref,
                     indq_ref, invn_ref, place_ref, q_out, k_out, v_out):
    b = b_ref[...]
    tab = tab_ref[...]
    cq = _rms(b[:, :B_Q_RANK], qan_ref[...])
    ckv = _rms(b[:, B_Q_RANK:B_Q_RANK + B_KV_RANK], kvan_ref[...])
    kr = b[:, B_Q_RANK + B_KV_RANK:]
    ind2 = indq_ref[...]
    invn = invn_ref[...]

    def seg_sums(x2):
        hi, lo = _split(x2)
        hp = lambda t, h: t[:, h * HEAD_PAD:(h + 1) * HEAD_PAD]
        return jnp.concatenate(
            [jnp.dot(jnp.concatenate([hp(hi, h), hp(lo, h)], axis=1), ind2, preferred_element_type=F32)
             for h in range(B_HEADS)], axis=1)

    q = _dot(cq, wqb_ref[...])
    qn = q * lax.rsqrt(seg_sums(q * q) * invn + EPS) * qw_ref[...]
    scale = (B_NOPE + B_ROPE) ** -0.5
    pieces = []
    for h in range(B_HEADS):
        x = qn[:, h * HEAD_PAD:(h + 1) * HEAD_PAD]
        half = B_ROPE // 2
        r = (x * tab[:, 0:LANE] + pltpu.roll(x, LANE - half, 1) * tab[:, LANE:2 * LANE]
             + pltpu.roll(x, half, 1) * tab[:, 2 * LANE:3 * LANE])
        pieces.append(r * scale)
    q_out[...] = jnp.concatenate(pieces, axis=1).astype(BF16)

    kk = _dot(ckv, wk_ref[...])
    kn = kk * lax.rsqrt(seg_sums(kk * kk) * invn + EPS) * kw_ref[...]
    krn = kr * lax.rsqrt(jnp.sum(kr * kr, axis=-1, keepdims=True) * (1.0 / B_ROPE) + EPS) * knr_ref[...]
    half = B_ROPE // 2
    krr = (krn * tab[:, 3 * LANE:4 * LANE] + pltpu.roll(krn, LANE - half, 1) * tab[:, 4 * LANE:5 * LANE]
           + pltpu.roll(krn, half, 1) * tab[:, 5 * LANE:6 * LANE])
    k_out[...] = (kn + jnp.dot(krr.astype(BF16), place_ref[...], preferred_element_type=F32)).astype(BF16)
    v_out[...] = _dot(ckv, wv_ref[...]).astype(BF16)


def _mla_prep(b_in, tab, qan, kvan, wqb, wk, wv, qw, kw, knr, indq, invn, place):
    B, L, _ = b_in.shape
    tm = TOK_TILE
    HW = B_HEADS * HEAD_PAD
    return pl.pallas_call(
        _mla_prep_kernel,
        grid=(B, L // tm),
        in_specs=[pl.BlockSpec((None, tm, B_PAD), lambda b, j: (b, j, 0)),
                  pl.BlockSpec((tm, 6 * LANE), lambda b, j: (j, 0)),
                  _full((1, B_Q_RANK)), _full((1, B_KV_RANK)), _full((B_Q_RANK, HW)), _full((B_KV_RANK, HW)),
                  _full((B_KV_RANK, B_WIDTH)), _full((1, HW)), _full((1, HW)), _full((1, LANE)),
                  _full((2 * HEAD_PAD, HEAD_PAD)), _full((1, HW)), _full((LANE, HW))],
        out_specs=[pl.BlockSpec((None, tm, HW), lambda b, j: (b, j, 0)),
                   pl.BlockSpec((None, tm, HW), lambda b, j: (b, j, 0)),
                   pl.BlockSpec((None, tm, B_WIDTH), lambda b, j: (b, j, 0))],
        out_shape=[jax.ShapeDtypeStruct((B, L, HW), BF16), jax.ShapeDtypeStruct((B, L, HW), BF16),
                   jax.ShapeDtypeStruct((B, L, B_WIDTH), BF16)],
        compiler_params=_cparams(("parallel", "parallel")),
        name="mla_prep",
    )(b_in, tab, qan, kvan, wqb, wk, wv, qw, kw, knr, indq, invn, place)


def _attn_kernel(q_ref, k_ref, v_ref, o_ref, v_aug):
    @pl.when(pl.program_id(2) == 0)
    def _():
        lane = lax.broadcasted_iota(jnp.int32, (v_aug.shape[0], LANE), 1)
        v_aug[:, 0:2 * B_V] = v_ref[...]
        v_aug[:, 2 * B_V:] = jnp.where(lane == 0, 1.0, 0.0).astype(BF16)

    scores = []
    for hh in range(2):
        q = q_ref[:, hh * HEAD_PAD:(hh + 1) * HEAD_PAD]
        k = k_ref[:, hh * HEAD_PAD:(hh + 1) * HEAD_PAD]
        scores.append(lax.dot_general(q, k, (((1,), (1,)), ((), ())), preferred_element_type=F32))
    outs = []
    for s in scores:
        p = jnp.exp((s - jnp.max(s, axis=-1, keepdims=True)).astype(BF16))
        pv = jnp.dot(p, v_aug[...], preferred_element_type=F32)
        outs.append(pv[:, 0:2 * B_V] / pv[:, 2 * B_V:2 * B_V + 1])
    lane = lax.broadcasted_iota(jnp.int32, outs[0].shape, 1)
    o_ref[...] = jnp.where(lane < B_V, outs[0], outs[1])


def _attn(q, k, v):
    B, L, _ = q.shape
    tq = min(1024, L)
    return pl.pallas_call(
        _attn_kernel,
        grid=(B, B_HEADS // 2, L // tq),
        in_specs=[pl.BlockSpec((None, tq, 2 * HEAD_PAD), lambda b, p, j: (b, j, p)),
                  pl.BlockSpec((None, L, 2 * HEAD_PAD), lambda b, p, j: (b, 0, p)),
                  pl.BlockSpec((None, L, 2 * B_V), lambda b, p, j: (b, 0, p))],
        out_specs=pl.BlockSpec((None, tq, 2 * B_V), lambda b, p, j: (b, j, p)),
        out_shape=jax.ShapeDtypeStruct((B, L, B_WIDTH), F32),
        scratch_shapes=[pltpu.VMEM((L, 2 * B_V + LANE), BF16)],
        compiler_params=_cparams(("parallel", "parallel", "arbitrary")),
        name="mla_attn",
    )(q, k, v)


def _hgrn_tables():
    C = CHUNK
    m_all = np.zeros((8 * C, C), np.float32)
    masks = np.zeros((7, C, C), np.float32)
    for lvl in range(6):
        m = 1 << lvl
        for t in range(C):
            bd = (t // (2 * m)) * 2 * m + m - 1
            if t % (2 * m) >= m:
                m_all[lvl * C + t, bd + 1:t + 1] = 1.0
            else:
                m_all[lvl * C + t, t + 1:bd + 1] = 1.0
            for s in range(C):
                if t // (2 * m) == s // (2 * m) and t % (2 * m) >= m and s % (2 * m) < m:
                    masks[lvl, t, s] = 1.0
    for t in range(C):
        m_all[6 * C + t, :t + 1] = 1.0
        m_all[7 * C + t, t + 1:] = 1.0
    masks[6] = np.eye(C, dtype=np.float32)
    m_f = m_all.reshape(8, C, C)
    m_b = m_f[:, ::-1, ::-1]
    mk_b = masks[:, ::-1, ::-1]
    m2 = np.stack([m_f.reshape(8 * C, C), m_b.reshape(8 * C, C)])
    return (np.concatenate([m2, m2], axis=2), np.stack([masks, mk_b]))


def _hgrn_kernel(q_ref, lff_ref, lfb_ref, kf_ref, kb_ref, i_ref, gate_ref, cnorm_ref, mall_ref, masks_ref, ind_ref,
                 out_ref, of_s, ob_s, s_s, *, L):
    nc = L // CHUNK
    C = CHUNK

    def wide(c, d):
        t0 = pl.multiple_of(c * C, C)
        logf = (lff_ref if d == 0 else lfb_ref)[pl.ds(t0, C), :]
        e_all = jnp.exp(_sel2_dot(mall_ref[d], logf))
        return dict(t0=t0, q=q_ref[pl.ds(t0, C), :], k=(kf_ref if d == 0 else kb_ref)[pl.ds(t0, C), :],
                    e=e_all, v=i_ref[pl.ds(t0, C), :])

    unroll = 2

    def scan_group(i, carry):
        wd = [[wide(unroll * i + u, 0), wide(nc - 1 - (unroll * i + u), 1)] for u in range(unroll)]
        dh = [(d, h) for d in (0, 1) for h in range(C_HEADS)]
        ks = [(u, d, h) for u in range(unroll) for d, h in dh]
        ls = lambda h: slice(h * C_EXPAND, (h + 1) * C_EXPAND)
        vs = lambda h: slice(h * C_HEAD_DIM, (h + 1) * C_HEAD_DIM)
        qb = {(u, d, h): wd[u][d]['q'][:, ls(h)] for u, d, h in ks}
        kb = {(u, d, h): wd[u][d]['k'][:, ls(h)] for u, d, h in ks}
        vh = {(u, d, h): wd[u][d]['v'][:, vs(h)] for u, d, h in ks}
        lvl_e = lambda u, d, h, j: wd[u][d]['e'][j * C:(j + 1) * C, ls(h)]
        attn = {(u, d, h): _dot_nt(qb[u, d, h], kb[u, d, h]) * masks_ref[d, 6] for u, d, h in ks}
        for lvl in range(6):
            for u, d, h in ks:
                e = lvl_e(u, d, h, lvl).astype(BF16)
                attn[u, d, h] = attn[u, d, h] + _dot_nt(qb[u, d, h] * e, kb[u, d, h] * e) * masks_ref[d, lvl]
        intra = {k: _dot(attn[k], vh[k]) for k in ks}
        qin = {(u, d, h): qb[u, d, h] * lvl_e(u, d, h, 6).astype(BF16) for u, d, h in ks}
        kout = {(u, d, h): _dot_tn(vh[u, d, h], kb[u, d, h] * lvl_e(u, d, h, 7).astype(BF16)) for u, d, h in ks}
        st = {(d, h): s_s[d * C_HEADS + h] for d, h in dh}
        for u in range(unroll):
            o = {(d, h): intra[u, d, h] + _dot_nt(qin[u, d, h], st[d, h]) for d, h in dh}
            for d, h in dh:
                e_in = lvl_e(u, d, h, 6)
                st[d, h] = st[d, h] * (e_in[C - 1:C] if d == 0 else e_in[0:1]) + kout[u, d, h]
            of_s[pl.ds(wd[u][0]['t0'], C), :] = jnp.concatenate([o[0, h] for h in range(C_HEADS)], axis=1)
            ob_s[pl.ds(wd[u][1]['t0'], C), :] = jnp.concatenate([o[1, h] for h in range(C_HEADS)], axis=1)
        for d, h in dh:
            s_s[d * C_HEADS + h] = st[d, h]
        return carry

    s_s[...] = jnp.zeros_like(s_s)
    lax.fori_loop(0, nc // unroll, scan_group, 0)

    rows = 256
    ind = ind_ref[...]

    def finish(i, carry):
        t0 = pl.multiple_of(i * rows, rows)
        o = of_s[pl.ds(t0, rows), :] + ob_s[pl.ds(t0, rows), :]
        ms = _dot_sel(o * o, ind) * (1.0 / C_HEAD_DIM)
        o = o * lax.rsqrt(ms + EPS) * cnorm_ref[...]
        out_ref[pl.ds(t0, rows), :] = o * _silu(gate_ref[pl.ds(t0, rows), :])
        return carry

    lax.fori_loop(0, L // rows, finish, 0)


def _hgrn(q, logf, kk, v, gate, cnorm, mall, masks, ind):
    B, L, _ = q.shape
    kern = functools.partial(_hgrn_kernel, L=L)
    one = None
    F, W = C_FDIM, C_WIDTH
    seq = lambda w, j: pl.BlockSpec((None, L, w), lambda b: (b, 0, j), pipeline_mode=one)
    return pl.pallas_call(
        kern,
        grid=(B,),
        in_specs=[seq(F, 0), seq(F, 0), seq(F, 1), seq(F, 0), seq(F, 1), seq(W, 0), seq(W, 0),
                  _full((1, W)), _full((2, 8 * CHUNK, 2 * CHUNK)), _full((2, 7, CHUNK, CHUNK)), _full((W, W))],
        out_specs=pl.BlockSpec((None, L, W), lambda b: (b, 0, 0)),
        out_shape=jax.ShapeDtypeStruct((B, L, W), F32),
        scratch_shapes=[pltpu.VMEM((L, W), F32), pltpu.VMEM((L, W), F32),
                        pltpu.VMEM((2 * C_HEADS, C_HEAD_DIM, C_EXPAND), F32)],
        compiler_params=_cparams(("parallel",)),
        name="hgrn2",
    )(q, logf, logf, kk, kk, v, gate, cnorm, mall, masks, ind)


def _outproj_kernel(xa_ref, xb_ref, oa_ref, ob_ref, oc_ref, wa_ref, wb_ref, wc_ref, ln_ref, wr_ref, br_ref, tri_ref,
                    x1_ref, h_ref, route_ref, cnt_ref, cnt_s, *, n_first):
    x1 = (_pair_load(xa_ref, xb_ref, n_first) + _dot(oa_ref[...], wa_ref[...]) + _dot(ob_ref[...], wb_ref[...])
          + _dot(oc_ref[...], wc_ref[...]))
    x1_ref[...] = x1
    h = _rms(x1, ln_ref[...])
    h_ref[...] = h
    hh, hl = _split(h)
    wr = wr_ref[...]
    wh, wl = _split(wr)
    lg = (jnp.dot(hh, wh, preferred_element_type=F32) + jnp.dot(hh, wl, preferred_element_type=F32)
          + jnp.dot(hl, wh, preferred_element_type=F32)) + br_ref[...]
    lane = lax.broadcasted_iota(jnp.int32, lg.shape, 1).astype(F32)
    neg = -1e30
    far = 1e9
    gmask = lane < N_GROUPS
    gl = jnp.where(gmask, lg, neg)
    gmax = jnp.max(gl, axis=-1, keepdims=True)
    g_idx = jnp.min(jnp.where(gl == gmax, lane, far), axis=-1, keepdims=True)
    g_w = 1.0 / jnp.sum(jnp.where(gmask, jnp.exp(gl - gmax), 0.0), axis=-1, keepdims=True)
    e_lane = lane - N_GROUPS
    emask = (e_lane >= g_idx * EXPERTS_PER_GROUP) & (e_lane < (g_idx + 1.0) * EXPERTS_PER_GROUP)
    el = jnp.where(emask, lg, neg)
    m1 = jnp.max(el, axis=-1, keepdims=True)
    i1 = jnp.min(jnp.where(el == m1, lane, far), axis=-1, keepdims=True)
    el2 = jnp.where(lane == i1, neg, el)
    m2 = jnp.max(el2, axis=-1, keepdims=True)
    i2 = jnp.min(jnp.where((el2 == m2) & emask & (lane != i1), lane, far), axis=-1, keepdims=True)
    r = jnp.exp(m2 - m1)
    w1 = g_w / (1.0 + r)
    w2 = g_w * r / (1.0 + r)
    @pl.when(pl.program_id(0) == 0)
    def _():
        cnt_s[...] = jnp.zeros_like(cnt_s)

    oh1 = jnp.where(lane == i1, 1.0, 0.0)
    oh2 = jnp.where(lane == i2, 1.0, 0.0)
    tri = tri_ref[...]
    run = cnt_s[0:1, :]
    tot1 = jnp.sum(oh1, axis=0, keepdims=True)
    c1 = run + jnp.dot(tri, oh1.astype(BF16), preferred_element_type=F32)
    c2 = run + tot1 + jnp.dot(tri, oh2.astype(BF16), preferred_element_type=F32)
    r1 = jnp.sum(oh1 * c1, axis=-1, keepdims=True)
    r2 = jnp.sum(oh2 * c2, axis=-1, keepdims=True)
    cnt = jnp.broadcast_to(run + tot1 + jnp.sum(oh2, axis=0, keepdims=True), cnt_s.shape)
    cnt_s[...] = cnt
    cnt_ref[...] = cnt
    vals = (i1 - N_GROUPS, i2 - N_GROUPS, w1, w2, r1, r2)
    route = jnp.zeros_like(lg)
    for j, v in enumerate(vals):
        route = jnp.where(lane == float(j), v, route)
    route_ref[...] = route


def _outproj(xa, xb, oa, ob, oc, wa, wb, wc, ln, wr, br, tri):
    tm = TOK_TILE
    n_first = xa.shape[0] // tm
    T = xa.shape[0] + xb.shape[0]
    row = lambda w: pl.BlockSpec((tm, w), lambda i: (i, 0))
    return pl.pallas_call(
        functools.partial(_outproj_kernel, n_first=n_first),
        grid=(T // tm,),
        in_specs=_pair_specs(n_first, tm) + [row(A_WIDTH), row(B_WIDTH), row(C_WIDTH),
                  _full((A_WIDTH, D_MODEL)), _full((B_WIDTH, D_MODEL)), _full((C_WIDTH, D_MODEL)),
                  _full((1, D_MODEL)), _full((D_MODEL, LANE)), _full((1, LANE)), _full((tm, tm))],
        out_specs=[row(D_MODEL), row(D_MODEL), row(LANE), _full((8, LANE))],
        out_shape=[jax.ShapeDtypeStruct((T, D_MODEL), F32), jax.ShapeDtypeStruct((T, D_MODEL), F32),
                   jax.ShapeDtypeStruct((T, LANE), F32), jax.ShapeDtypeStruct((8, LANE), F32)],
        scratch_shapes=[pltpu.VMEM((8, LANE), F32)],
        compiler_params=_cparams(("arbitrary",)),
        name="outproj_router",
    )(xa, xb, oa, ob, oc, wa, wb, wc, ln, wr, br, tri)


def _gather_rows(src_hbm, buf, sem, cur_ids, nxt_ids, n_rows):
    i = pl.program_id(0)
    slot = i % 2

    def row_copy(ids, r, s):
        return pltpu.make_async_copy(src_hbm.at[pl.ds(ids[0, 0, r], 1)], buf.at[s, pl.ds(r, 1)], sem.at[s])

    def start_all(ids, s):
        def body(r, carry):
            row_copy(ids, r, s).start()
            return carry
        lax.fori_loop(0, n_rows, body, 0, unroll=8)

    def wait_all(s):
        pltpu.make_async_copy(src_hbm.at[pl.ds(0, n_rows)], buf.at[s], sem.at[s]).wait()

    @pl.when(i == 0)
    def _():
        start_all(cur_ids, 0)

    start_all(nxt_ids, 1 - slot)
    wait_all(slot)

    @pl.when(i == pl.num_programs(0) - 1)
    def _():
        wait_all(1 - slot)

    return slot


def _scatter_kernel(pends_ref, pos_ref, h_ref, xs_hbm, zero_s, sem):
    tm = TOK_TILE

    @pl.when(pl.program_id(0) == 0)
    def _():
        zero_s[...] = jnp.zeros_like(zero_s)

        def fill(e):
            start = pl.multiple_of(jnp.maximum(pends_ref[e] - MOE_ROWS, 0), MOE_ROWS)
            return pltpu.make_async_copy(zero_s, xs_hbm.at[pl.ds(start, MOE_ROWS)], sem)

        for e in range(N_EXPERTS):
            fill(e).start()
            fill(e).wait()

        def fill_tail(b, carry):
            cp = pltpu.make_async_copy(zero_s, xs_hbm.at[pl.ds(pl.multiple_of(b * MOE_ROWS, MOE_ROWS), MOE_ROWS)], sem)
            cp.start()
            cp.wait()
            return carry

        lax.fori_loop(pends_ref[N_EXPERTS - 1] // MOE_ROWS, xs_hbm.shape[0] // MOE_ROWS, fill_tail, 0)

    def row_copy(r, k):
        return pltpu.make_async_copy(h_ref.at[pl.ds(r, 1)], xs_hbm.at[pl.ds(pos_ref[0, 0, k * tm + r], 1)], sem)

    def issue(r, carry):
        for k in range(TOP_K):
            row_copy(r, k).start()
        return carry

    lax.fori_loop(0, tm, issue, 0, unroll=8)
    for k in range(TOP_K):
        pltpu.make_async_copy(h_ref, xs_hbm.at[pl.ds(0, tm)], sem).wait()


def _scatter(pends, pos, h):
    T = h.shape[0]
    tm = TOK_TILE
    grid_spec = pltpu.PrefetchScalarGridSpec(
        num_scalar_prefetch=1,
        grid=(T // tm,),
        in_specs=[pl.BlockSpec((1, 1, TOP_K * tm), lambda i, pe: (i, 0, 0), memory_space=pltpu.SMEM),
                  pl.BlockSpec((tm, D_MODEL), lambda i, pe: (i, 0))],
        out_specs=pl.BlockSpec(memory_space=pl.ANY),
        scratch_shapes=[pltpu.VMEM((MOE_ROWS, D_MODEL), F32), pltpu.SemaphoreType.DMA(())],
    )
    return pl.pallas_call(
        _scatter_kernel,
        grid_spec=grid_spec,
        out_shape=jax.ShapeDtypeStruct((T * TOP_K + N_EXPERTS * MOE_ROWS, D_MODEL), F32),
        compiler_params=_cparams(("arbitrary",)),
        name="moe_scatter",
    )(pends, pos, h)


def _moe_kernel(blk_e_ref, nact_ref, x_ref, w1_ref, w3_ref, w2_ref, y_ref):
    active = pl.program_id(0) < nact_ref[0]

    @pl.when(active)
    def _():
        x = x_ref[...].astype(BF16)
        a = jnp.dot(x, w1_ref[...], preferred_element_type=F32)
        g = jnp.dot(x, w3_ref[...], preferred_element_type=F32)
        y_ref[...] = jnp.dot((_silu(a) * g).astype(BF16), w2_ref[...], preferred_element_type=F32)

    @pl.when(jnp.logical_not(active))
    def _():
        y_ref[...] = jnp.zeros_like(y_ref)


def _moe(blk_e, nact, xs, w1, w3, w2):
    P = xs.shape[0]
    nblk = P // MOE_ROWS
    grid_spec = pltpu.PrefetchScalarGridSpec(
        num_scalar_prefetch=2,
        grid=(nblk,),
        in_specs=[pl.BlockSpec((MOE_ROWS, D_MODEL), lambda i, be, na: (jnp.minimum(i, na[0] - 1), 0)),
                  pl.BlockSpec((None, D_MODEL, D_EXPERT), lambda i, be, na: (be[i], 0, 0)),
                  pl.BlockSpec((None, D_MODEL, D_EXPERT), lambda i, be, na: (be[i], 0, 0)),
                  pl.BlockSpec((None, D_EXPERT, D_MODEL), lambda i, be, na: (be[i], 0, 0))],
        out_specs=pl.BlockSpec((MOE_ROWS, D_MODEL), lambda i, be, na: (i, 0)),
    )
    return pl.pallas_call(
        _moe_kernel,
        grid_spec=grid_spec,
        out_shape=jax.ShapeDtypeStruct((P, D_MODEL), F32),
        compiler_params=_cparams(("arbitrary",)),
        name="moe_ffn",
    )(blk_e, nact, xs, w1, w3, w2)


def _combine_kernel(pos_ref, nxt_ref, x_ref, route_ref, y_hbm, outa_ref, outb_ref, yb, sem, *, n_first):
    tm = x_ref.shape[0]
    slot = _gather_rows(y_hbm, yb, sem, pos_ref, nxt_ref, TOP_K * tm)
    route = route_ref[...]
    val = x_ref[...] + yb[slot, 0:tm, :] * route[:, 2:3] + yb[slot, tm:2 * tm, :] * route[:, 3:4]
    first = pl.program_id(0) < n_first

    @pl.when(first)
    def _():
        outa_ref[...] = val

    @pl.when(jnp.logical_not(first))
    def _():
        outb_ref[...] = val


def _combine(pos, x1, route, y, n_first):
    T = x1.shape[0]
    tm = TOK_TILE
    nt = T // tm
    return pl.pallas_call(
        functools.partial(_combine_kernel, n_first=n_first),
        grid=(nt,),
        in_specs=[pl.BlockSpec((1, 1, TOP_K * tm), lambda i: (i, 0, 0), memory_space=pltpu.SMEM),
                  pl.BlockSpec((1, 1, TOP_K * tm), lambda i: (jnp.minimum(i + 1, nt - 1), 0, 0),
                               memory_space=pltpu.SMEM),
                  pl.BlockSpec((tm, D_MODEL), lambda i: (i, 0)),
                  pl.BlockSpec((tm, LANE), lambda i: (i, 0)),
                  pl.BlockSpec(memory_space=pl.ANY)],
        out_specs=_pair_specs(n_first, tm),
        out_shape=[jax.ShapeDtypeStruct((n_first * tm, D_MODEL), F32),
                   jax.ShapeDtypeStruct((T - n_first * tm, D_MODEL), F32)],
        scratch_shapes=[pltpu.VMEM((2, TOP_K * tm, D_MODEL), F32), pltpu.SemaphoreType.DMA((2,))],
        compiler_params=_cparams(("arbitrary",)),
        name="moe_combine",
    )(pos, pos, x1, route, y)


def _dispatch(route, cnt):
    T = route.shape[0]
    nblk = (T * TOP_K) // MOE_ROWS + N_EXPERTS
    counts = cnt[0, N_GROUPS:N_GROUPS + N_EXPERTS].astype(jnp.int32)
    padded = (counts + MOE_ROWS - 1) // MOE_ROWS * MOE_ROWS
    pends = jnp.cumsum(padded)
    pstarts = pends - padded
    e = route[:, 0:TOP_K].astype(jnp.int32)
    rank = route[:, 4:4 + TOP_K].astype(jnp.int32)
    sel = e[:, :, None] == jnp.arange(N_EXPERTS, dtype=jnp.int32)
    pos = rank + jnp.sum(jnp.where(sel, pstarts, 0), axis=-1)
    pos = jnp.transpose(pos.reshape(T // TOK_TILE, TOK_TILE, TOP_K), (0, 2, 1))
    pos = pos.reshape(T // TOK_TILE, 1, TOP_K * TOK_TILE)
    blk_start = jnp.arange(nblk, dtype=jnp.int32) * MOE_ROWS
    blk_e = jnp.minimum(jnp.sum(blk_start[:, None] >= pends[None, :], axis=-1), N_EXPERTS - 1).astype(jnp.int32)
    nact = (pends[-1:] // MOE_ROWS).astype(jnp.int32)
    return blk_e, nact, pos, pends.astype(jnp.int32)


def _block_ind(width, seg):
    i = np.arange(width)
    return (i[:, None] // seg == i[None, :] // seg).astype(np.float32)


def _rope_tables(L):
    half = B_ROPE // 2
    inv = ROPE_THETA ** (-jnp.arange(0, B_ROPE, 2, dtype=F32) / B_ROPE)
    ang = jnp.arange(L, dtype=F32)[:, None] * inv[None, :]
    cos, sin = jnp.cos(ang), jnp.sin(ang)
    z = lambda n: jnp.zeros((L, n), F32)
    one = jnp.ones((L, B_NOPE), F32)
    rest = LANE - B_NOPE - B_ROPE
    cq = jnp.concatenate([one, cos, cos, z(rest)], axis=1)
    s1q = jnp.concatenate([z(B_NOPE), -sin, z(half), z(rest)], axis=1)
    s2q = jnp.concatenate([z(B_NOPE), z(half), sin, z(rest)], axis=1)
    ck = jnp.concatenate([cos, cos, z(LANE - B_ROPE)], axis=1)
    s1k = jnp.concatenate([-sin, z(LANE - half)], axis=1)
    s2k = jnp.concatenate([z(half), sin, z(LANE - B_ROPE)], axis=1)
    return jnp.concatenate([cq, s1q, s2q, ck, s1k, s2k], axis=1)


def _layer_params(l, ln1, w_in, conv_w, a_log, dt_bias, a_norm, q_a_norm, w_qb, kv_a_norm, w_kvb, qn_nope, qn_rope,
                  kn_nope, kn_rope, lbs, c_norm, w_out, ln2, w_group, b_group, w_router, b_router, w1, w3, w2):
    p = {}
    w = w_in[l]
    wp = jnp.zeros((D_MODEL, N_PAD), F32)
    wp = wp.at[:, 0:A_IN].set(w[:, 0:A_IN])
    wp = wp.at[:, A_PAD:A_PAD + B_IN].set(w[:, A_IN:A_IN + B_IN])
    wp = wp.at[:, A_PAD + B_PAD:].set(w[:, A_IN + B_IN:])
    p['w_in'] = wp.astype(BF16)
    p['ln1'] = ln1[l].reshape(1, D_MODEL)
    p['convw'] = jnp.zeros((8, 3 * A_WIDTH), F32).at[0:CONV_K].set(conv_w[l])
    gpad = lambda v: jnp.zeros((1, LANE), F32).at[0, 2 * A_HEADS:4 * A_HEADS].set(v.reshape(-1))
    p['alog'] = gpad(a_log[l])
    p['dtb'] = gpad(dt_bias[l])
    p['anorm'] = jnp.tile(a_norm[l], A_HEADS).reshape(1, A_WIDTH)
    half = B_ROPE // 2
    wq = w_qb[l].reshape(B_Q_RANK, B_HEADS, B_NOPE + B_ROPE)
    wq = jnp.pad(wq, ((0, 0), (0, 0), (0, HEAD_PAD - B_NOPE - B_ROPE)))
    p['wqb'] = wq.reshape(B_Q_RANK, B_HEADS * HEAD_PAD).astype(BF16)
    wkv = w_kvb[l].reshape(B_KV_RANK, B_HEADS, B_NOPE + B_V)
    wk = jnp.pad(wkv[:, :, :B_NOPE], ((0, 0), (0, 0), (0, HEAD_PAD - B_NOPE)))
    p['wk'] = wk.reshape(B_KV_RANK, B_HEADS * HEAD_PAD).astype(BF16)
    p['wv'] = wkv[:, :, B_NOPE:].reshape(B_KV_RANK, B_WIDTH).astype(BF16)
    zpad = jnp.zeros((HEAD_PAD - B_NOPE - B_ROPE,), F32)
    p['qw'] = jnp.tile(jnp.concatenate([qn_nope[l], qn_rope[l], zpad]), B_HEADS).reshape(1, -1)
    p['kw'] = jnp.tile(jnp.concatenate([kn_nope[l], jnp.zeros((HEAD_PAD - B_NOPE,), F32)]), B_HEADS).reshape(1, -1)
    p['knr'] = jnp.concatenate([kn_rope[l], jnp.zeros((LANE - B_ROPE,), F32)]).reshape(1, LANE)
    p['qan'] = q_a_norm[l].reshape(1, -1)
    p['kvan'] = kv_a_norm[l].reshape(1, -1)
    p['lb'] = lbs[l]
    p['cnorm'] = jnp.tile(c_norm[l], C_HEADS).reshape(1, C_WIDTH)
    wo = w_out[l].astype(BF16)
    p['wa'], p['wb'], p['wc'] = wo[:A_WIDTH], wo[A_WIDTH:A_WIDTH + B_WIDTH], wo[A_WIDTH + B_WIDTH:]
    p['ln2'] = ln2[l].reshape(1, D_MODEL)
    wr = jnp.zeros((D_MODEL, LANE), F32).at[:, :N_GROUPS].set(w_group[l])
    p['wr'] = wr.at[:, N_GROUPS:N_GROUPS + N_EXPERTS].set(w_router[l])
    br = jnp.zeros((1, LANE), F32).at[0, :N_GROUPS].set(b_group[l])
    p['br'] = br.at[0, N_GROUPS:N_GROUPS + N_EXPERTS].set(b_router[l])
    p['w1'], p['w3'], p['w2'] = w1[l].astype(BF16), w3[l].astype(BF16), w2[l].astype(BF16)
    return p


def _static_tables():
    t = {}
    t['ind_a'] = jnp.asarray(_block_ind(A_WIDTH, A_HEAD_DIM), BF16)
    t['ind_c'] = jnp.asarray(_block_ind(C_WIDTH, C_HEAD_DIM), BF16)
    expb = np.zeros((2, LANE, A_WIDTH), np.float32)
    expg = np.zeros((2, LANE, A_WIDTH), np.float32)
    for d in range(2):
        for h in range(A_HEADS):
            expb[d, A_HEADS * d + h, h * A_HEAD_DIM:(h + 1) * A_HEAD_DIM] = 1.0
            expg[d, 2 * A_HEADS + A_HEADS * d + h, h * A_HEAD_DIM:(h + 1) * A_HEAD_DIM] = 1.0
    t['expb'] = jnp.asarray(expb, BF16)
    t['expg'] = jnp.asarray(expg, BF16)
    t['pick'] = jnp.asarray(np.repeat(np.transpose(expg[:, :, ::A_HEAD_DIM], (0, 2, 1)), CHUNK, axis=1), BF16)
    HW = B_HEADS * HEAD_PAD
    lane = np.arange(HW)
    seg = np.where(lane % HEAD_PAD < B_NOPE, 0, np.where(lane % HEAD_PAD < B_NOPE + B_ROPE, 1, 2))
    head = lane // HEAD_PAD
    seg1 = seg[:HEAD_PAD]
    ind1 = ((seg1[:, None] == seg1[None, :]) & (seg1[:, None] < 2)).astype(np.float32)
    t['indq'] = jnp.asarray(np.concatenate([ind1, ind1], axis=0), BF16)
    t['invn'] = jnp.asarray(np.where(seg == 0, 1.0 / B_NOPE, 1.0 / B_ROPE).astype(np.float32)).reshape(1, HW)
    place = np.zeros((LANE, HW), np.float32)
    for h in range(B_HEADS):
        for r in range(B_ROPE):
            place[r, h * HEAD_PAD + B_NOPE + r] = 1.0
    t['place'] = jnp.asarray(place, BF16)
    t['tri'] = jnp.asarray(np.tril(np.ones((TOK_TILE, TOK_TILE), np.float32), -1), BF16)
    mall, masks = _hgrn_tables()
    t['mall'] = jnp.asarray(mall, BF16)
    t['masks'] = jnp.asarray(masks, F32)
    return t


def _trunk(x_first, x_second, params, tabs, rope):
    L, D = x_first.shape[1:]
    B = x_first.shape[0] + x_second.shape[0]
    T = B * L
    xa, xb = x_first.reshape(-1, D), x_second.reshape(-1, D)
    n_first = xa.shape[0] // TOK_TILE
    for p in params:
        a_in, b_in, c_q, c_logf, c_k, c_v, c_gate = _proj(xa, xb, p['ln1'], p['w_in'], p['lb'])
        o_a = _gdn(a_in.reshape(B, L, A_PAD), p['convw'], p['alog'], p['dtb'], p['anorm'], tabs['ind_a'],
                   tabs['expb'], tabs['expg'], tabs['pick'])
        q, k, v = _mla_prep(b_in.reshape(B, L, B_PAD), rope, p['qan'], p['kvan'], p['wqb'], p['wk'], p['wv'],
                            p['qw'], p['kw'], p['knr'], tabs['indq'], tabs['invn'], tabs['place'])
        o_b = _attn(q, k, v)
        seq = lambda t: t.reshape(B, L, t.shape[-1])
        o_c = _hgrn(seq(c_q), seq(c_logf), seq(c_k), seq(c_v), seq(c_gate), p['cnorm'], tabs['mall'],
                    tabs['masks'], tabs['ind_c'])
        x1, h, route, cnt = _outproj(xa, xb, o_a.reshape(T, A_WIDTH), o_b.reshape(T, B_WIDTH),
                                     o_c.reshape(T, C_WIDTH), p['wa'], p['wb'], p['wc'], p['ln2'], p['wr'], p['br'],
                                     tabs['tri'])
        blk_e, nact, pos, pends = _dispatch(route, cnt)
        xs = _scatter(pends, pos, h)
        y = _moe(blk_e, nact, xs, p['w1'], p['w3'], p['w2'])
        xa, xb = _combine(pos, x1, route, y, n_first)
    return xa.reshape(x_first.shape), xb.reshape(x_second.shape)


def kernel(x_prompt, x_sample, ln1, w_in, conv_w, a_log, dt_bias, a_norm, q_a_norm, w_qb, kv_a_norm, w_kvb, qn_nope,
           qn_rope, kn_nope, kn_rope, c_lower_bounds, c_norm, w_out, ln2, w_group, b_group, w_router, b_router,
           w1, w3, w2):
    lbs = jnp.cumsum(jax.nn.softmax(c_lower_bounds.astype(F32), axis=0), axis=0)
    lbs = lbs - lbs[0:1]
    params = [_layer_params(l, ln1, w_in, conv_w, a_log, dt_bias, a_norm, q_a_norm, w_qb, kv_a_norm, w_kvb, qn_nope,
                            qn_rope, kn_nope, kn_rope, lbs, c_norm, w_out, ln2, w_group, b_group, w_router, b_router,
                            w1, w3, w2) for l in range(DEPTH)]
    tabs = _static_tables()
    assert x_prompt.shape[1:] == x_sample.shape[1:]
    rope = _rope_tables(x_prompt.shape[1])
    assert (x_prompt.shape[0] * x_prompt.shape[1]) % TOK_TILE == 0
    return _trunk(x_prompt, x_sample, params, tabs, rope)
```

```python
import functools
import math

import numpy as np
import jax
import jax.numpy as jnp
from jax import lax
from jax.experimental import pallas as pl
from jax.experimental.pallas import tpu as pltpu

F32 = jnp.float32
BF16 = jnp.bfloat16

D_MODEL = 1024
DEPTH = 2
EPS = 1e-6
A_HEADS, A_HEAD_DIM, CONV_K = 6, 64, 5
A_WIDTH = A_HEADS * A_HEAD_DIM
B_HEADS, B_Q_RANK, B_KV_RANK, B_NOPE, B_ROPE, B_V = 6, 256, 128, 64, 32, 64
B_WIDTH = B_HEADS * B_V
ROPE_THETA = 10000.0
C_HEADS, C_EXPAND, C_HEAD_DIM = 4, 128, 64
C_WIDTH = C_HEADS * C_HEAD_DIM
C_FDIM = C_HEADS * C_EXPAND
D_MIX = A_WIDTH + B_WIDTH + C_WIDTH
A_IN = 4 * A_WIDTH + 4 * A_HEADS
B_IN = B_Q_RANK + B_KV_RANK + B_ROPE
C_IN = 3 * C_FDIM + 2 * C_WIDTH
N_GROUPS, EXPERTS_PER_GROUP, TOP_K, D_EXPERT = 4, 8, 2, 512
N_EXPERTS = N_GROUPS * EXPERTS_PER_GROUP

LANE = 128
CHUNK = 64
A_PAD = 4 * A_WIDTH + LANE
B_PAD = 512
C_PAD = C_IN
N_PAD = A_PAD + B_PAD + C_PAD
HEAD_PAD = LANE
MOE_ROWS = 512
TOK_TILE = 512
VMEM_LIMIT = 56 * 1024 * 1024


def _cparams(sem):
    return pltpu.CompilerParams(dimension_semantics=sem, vmem_limit_bytes=VMEM_LIMIT)


def _dot(a, b):
    return jnp.dot(a.astype(BF16), b.astype(BF16), preferred_element_type=F32)


def _dot_nt(a, b):
    return lax.dot_general(a.astype(BF16), b.astype(BF16), (((1,), (1,)), ((), ())), preferred_element_type=F32)


def _dot_tn(a, b):
    return lax.dot_general(a.astype(BF16), b.astype(BF16), (((0,), (0,)), ((), ())), preferred_element_type=F32)


def _split(x):
    hi = x.astype(BF16)
    lo = (x - hi.astype(F32)).astype(BF16)
    return hi, lo


def _sel_dot(sel, x):
    hi, lo = _split(x)
    return jnp.dot(sel, hi, preferred_element_type=F32) + jnp.dot(sel, lo, preferred_element_type=F32)


def _sel2_dot(sel2, x):
    hi, lo = _split(x)
    return jnp.dot(sel2, jnp.concatenate([hi, lo], axis=0), preferred_element_type=F32)


def _dot_sel(x, sel):
    hi, lo = _split(x)
    return jnp.dot(hi, sel, preferred_element_type=F32) + jnp.dot(lo, sel, preferred_element_type=F32)


def _sigmoid(x):
    return 1.0 / (1.0 + jnp.exp(-x))


def _silu(x):
    return x * _sigmoid(x)


def _softplus(x):
    return jnp.maximum(x, 0.0) + jnp.log(1.0 + jnp.exp(-jnp.abs(x)))


def _rms(x, w):
    return x * lax.rsqrt(jnp.mean(x * x, axis=-1, keepdims=True) + EPS) * w


def _full(shape):
    return pl.BlockSpec(shape, lambda *_: (0,) * len(shape))


def _pair_specs(n_first, tm):
    return [pl.BlockSpec((tm, D_MODEL), lambda i, *_: (jnp.minimum(i, n_first - 1), 0)),
            pl.BlockSpec((tm, D_MODEL), lambda i, *_: (jnp.maximum(i - n_first, 0), 0))]


def _pair_load(xa_ref, xb_ref, n_first):
    return jnp.where(pl.program_id(0) < n_first, xa_ref[...], xb_ref[...])


def _proj_kernel(xa_ref, xb_ref, ln_ref, w_ref, lb_ref, a_ref, b_ref, cq_ref, clf_ref, ck_ref, cv_ref, cg_ref, *,
                 n_first):
    h = _rms(_pair_load(xa_ref, xb_ref, n_first), ln_ref[...]).astype(BF16)
    a_ref[...] = jnp.dot(h, w_ref[:, 0:A_PAD], preferred_element_type=F32)
    b_ref[...] = jnp.dot(h, w_ref[:, A_PAD:A_PAD + B_PAD], preferred_element_type=F32)
    c0 = A_PAD + B_PAD
    F = C_FDIM
    cq_ref[...] = _silu(jnp.dot(h, w_ref[:, c0:c0 + F], preferred_element_type=F32)).astype(BF16)
    for d in range(2):
        fp = jnp.dot(h, w_ref[:, c0 + (1 + d) * F:c0 + (2 + d) * F], preferred_element_type=F32)
        lb = lb_ref[d:d + 1, :]
        e = jnp.exp(-jnp.abs(fp))
        r = 1.0 / (1.0 + e)
        lsig = jnp.minimum(fp, 0.0) - jnp.log(1.0 + e)
        a = jnp.log(lb)
        bb = jnp.log(1.0 - lb) + lsig
        clf_ref[:, d * F:(d + 1) * F] = jnp.maximum(a, bb) + jnp.log(1.0 + jnp.exp(-jnp.abs(a - bb)))
        ck_ref[:, d * F:(d + 1) * F] = ((1.0 - lb) * jnp.where(fp >= 0.0, e, 1.0) * r).astype(BF16)
    tail = jnp.dot(h, w_ref[:, c0 + 3 * F:N_PAD], preferred_element_type=F32)
    cv_ref[...] = tail[:, :C_WIDTH].astype(BF16)
    cg_ref[...] = tail[:, C_WIDTH:]


def _proj(xa, xb, ln, w_pad, lb):
    tm = TOK_TILE
    n_first = xa.shape[0] // tm
    T = xa.shape[0] + xb.shape[0]
    row = lambda w: pl.BlockSpec((tm, w), lambda i: (i, 0))
    sds = jax.ShapeDtypeStruct
    return pl.pallas_call(
        functools.partial(_proj_kernel, n_first=n_first),
        grid=(T // tm,),
        in_specs=_pair_specs(n_first, tm) + [_full((1, D_MODEL)), _full((D_MODEL, N_PAD)), _full((2, C_FDIM))],
        out_specs=[row(A_PAD), row(B_PAD), row(C_FDIM), row(2 * C_FDIM), row(2 * C_FDIM), row(C_WIDTH), row(C_WIDTH)],
        out_shape=[sds((T, A_PAD), F32), sds((T, B_PAD), F32), sds((T, C_FDIM), BF16), sds((T, 2 * C_FDIM), F32),
                   sds((T, 2 * C_FDIM), BF16), sds((T, C_WIDTH), BF16), sds((T, C_WIDTH), F32)],
        compiler_params=_cparams(("parallel",)),
        name="proj",
    )(xa, xb, ln, w_pad, lb)


def _gdn_kernel(qkv_ref, gate_ref, gates_ref, convw_ref, alog_ref, dtb_ref, anorm_ref, ind_ref, expb_ref, expg_ref,
                pick_ref, out_ref, q_s, k_s, v_s, of_s, ob_s, s_s, *, L):
    nc = L // CHUNK
    W = A_WIDTH
    row = lax.broadcasted_iota(jnp.int32, (CHUNK, CHUNK), 0)
    col = lax.broadcasted_iota(jnp.int32, (CHUNK, CHUNK), 1)
    lane = lax.broadcasted_iota(jnp.int32, (CHUNK, LANE), 1)
    row2 = lax.broadcasted_iota(jnp.int32, (CHUNK, 2 * CHUNK), 0)
    col2 = lax.broadcasted_iota(jnp.int32, (CHUNK, 2 * CHUNK), 1) % CHUNK
    ind = ind_ref[...]

    def conv_chunk(c, carry):
        t0 = pl.multiple_of(c * CHUNK, CHUNK)
        tp = pl.multiple_of(jnp.maximum(t0 - 8, 0), 8)
        tn = pl.multiple_of(jnp.minimum(t0 + CHUNK, L - 8), 8)
        for g in range(3):
            cs = slice(g * W, (g + 1) * W)
            main = qkv_ref[pl.ds(t0, CHUNK), cs]
            prev = jnp.where(c > 0, qkv_ref[pl.ds(tp, 8), cs], 0.0)
            nxt = jnp.where(c < nc - 1, qkv_ref[pl.ds(tn, 8), cs], 0.0)
            win = jnp.concatenate([prev, main, nxt], axis=0)
            acc = win * convw_ref[2:3, cs]
            for j in (0, 1, 3, 4):
                acc = acc + pltpu.roll(win, (2 - j) % (CHUNK + 16), 0) * convw_ref[j:j + 1, cs]
            y = _silu(acc[8:8 + CHUNK])
            if g < 2:
                y = y * lax.rsqrt(_dot_sel(y * y, ind) + EPS)
            if g == 0:
                q_s[pl.ds(t0, CHUNK), :] = (y * (A_HEAD_DIM ** -0.5)).astype(BF16)
            elif g == 1:
                k_s[pl.ds(t0, CHUNK), :] = y.astype(BF16)
            else:
                v_s[pl.ds(t0, CHUNK), :] = y
        return carry

    lax.fori_loop(0, nc, conv_chunk, 0)

    coef = -jnp.exp(alog_ref[...])
    dtb = dtb_ref[...]
    g_lane = (lane >= 2 * A_HEADS) & (lane < 4 * A_HEADS)

    def wide(c, d):
        t0 = pl.multiple_of(c * CHUNK, CHUNK)
        qc = q_s[pl.ds(t0, CHUNK), :].astype(F32)
        kc = k_s[pl.ds(t0, CHUNK), :].astype(F32)
        vc = v_s[pl.ds(t0, CHUNK), :]
        gt = gates_ref[pl.ds(t0, CHUNK), :]
        beta = _sigmoid(gt)
        gl = jnp.where(g_lane, coef * _softplus(gt + dtb), 0.0)
        tri2 = ((col2 <= row2) if d == 0 else (col2 >= row2)).astype(BF16)
        gcum = _sel2_dot(tri2, gl)
        total = gcum[CHUNK - 1:CHUNK] if d == 0 else gcum[0:1]
        e_in = jnp.exp(gcum)
        e_out = jnp.exp(total - gcum)
        g_last = jnp.broadcast_to(jnp.exp(total), (8, LANE))
        expb = expb_ref[d]
        expg = expg_ref[d]
        beta_w = jnp.dot(beta.astype(BF16), expb, preferred_element_type=F32)
        ein_w = jnp.dot(e_in.astype(BF16), expg, preferred_element_type=F32)
        eout_w = jnp.dot(e_out.astype(BF16), expg, preferred_element_type=F32)
        kb = kc * beta_w
        return dict(
            t0=t0, q=qc, k=kc, kb=kb, vb=vc * beta_w, kbe=kb * ein_w, qe=qc * ein_w, kd=kc * eout_w,
            glast=_dot_sel(g_last, expg),
            gcum=gcum, gcum_t=gcum.T,
            incl=(col <= row) if d == 0 else (col >= row),
            strict=(col < row) if d == 0 else (col > row))

    unroll = 2

    def scan_group(i, carry):
        wd = [[wide(unroll * i + u, 0), wide(nc - 1 - (unroll * i + u), 1)] for u in range(unroll)]
        dh = [(d, h) for d in (0, 1) for h in range(A_HEADS)]
        ks = [(u, d, h) for u in range(unroll) for d, h in dh]
        hs = lambda h: slice(h * A_HEAD_DIM, (h + 1) * A_HEAD_DIM)
        dec, n, qk, x = {}, {}, {}, {}
        for u, d, h in ks:
            w = wd[u][d]
            lg = 2 * A_HEADS + A_HEADS * d + h
            dec[u, d, h] = jnp.exp(jnp.minimum(w['gcum'][:, lg:lg + 1] - w['gcum_t'][lg:lg + 1, :], 0.0))
        for u, d, h in ks:
            w = wd[u][d]
            n[u, d, h] = jnp.where(w['strict'], _dot_nt(w['kb'][:, hs(h)], w['k'][:, hs(h)]) * dec[u, d, h], 0.0)
        for u, d, h in ks:
            w = wd[u][d]
            qk[u, d, h] = jnp.where(w['incl'], _dot_nt(w['q'][:, hs(h)], w['k'][:, hs(h)]) * dec[u, d, h], 0.0)
            x[u, d, h] = jnp.concatenate([w['vb'][:, hs(h)], w['kbe'][:, hs(h)]], axis=1)
        xw = 2 * A_HEAD_DIM
        z = {k: _dot(n[k], jnp.concatenate([x[k], n[k]], axis=1)) for k in ks}
        x = {k: x[k] - z[k][:, :xw] for k in ks}
        for _ in range(4):
            p = {k: z[k][:, xw:] for k in ks}
            z = {k: _dot(p[k], jnp.concatenate([x[k], p[k]], axis=1)) for k in ks}
            x = {k: x[k] + z[k][:, :xw] for k in ks}
        x = {k: x[k] + _dot(z[k][:, xw:], x[k]) for k in ks}
        s = {(d, h): s_s[d * A_HEADS + h] for d, h in dh}
        for u in range(unroll):
            v_new = {(d, h): x[u, d, h][:, :A_HEAD_DIM] - _dot(x[u, d, h][:, A_HEAD_DIM:], s[d, h]) for d, h in dh}
            o = {(d, h): _dot(wd[u][d]['qe'][:, hs(h)], s[d, h]) + _dot(qk[u, d, h], v_new[d, h]) for d, h in dh}
            s = {(d, h): (s[d, h] * wd[u][d]['glast'][0:1, hs(h)] + _dot_tn(wd[u][d]['kd'][:, hs(h)], v_new[d, h]))
                 for d, h in dh}
            of_s[pl.ds(wd[u][0]['t0'], CHUNK), :] = jnp.concatenate([o[0, h] for h in range(A_HEADS)], axis=1)
            ob_s[pl.ds(wd[u][1]['t0'], CHUNK), :] = jnp.concatenate([o[1, h] for h in range(A_HEADS)], axis=1)
        for d, h in dh:
            s_s[d * A_HEADS + h] = s[d, h]
        return carry

    s_s[...] = jnp.zeros_like(s_s)
    lax.fori_loop(0, nc // unroll, scan_group, 0)

    rows = 256

    def finish(i, carry):
        t0 = pl.multiple_of(i * rows, rows)
        o = of_s[pl.ds(t0, rows), :] + ob_s[pl.ds(t0, rows), :]
        ms = _dot_sel(o * o, ind) * (1.0 / A_HEAD_DIM)
        o = o * lax.rsqrt(ms + EPS) * anorm_ref[...]
        out_ref[pl.ds(t0, rows), :] = o * _silu(gate_ref[pl.ds(t0, rows), :])
        return carry

    lax.fori_loop(0, L // rows, finish, 0)


def _gdn(a_in, convw, alog, dtb, anorm, ind, expb, expg, pick):
    B, L, _ = a_in.shape
    W = A_WIDTH
    kern = functools.partial(_gdn_kernel, L=L)
    one = None
    return pl.pallas_call(
        kern,
        grid=(B,),
        in_specs=[pl.BlockSpec((None, L, 3 * W), lambda b: (b, 0, 0), pipeline_mode=one),
                  pl.BlockSpec((None, L, W), lambda b: (b, 0, 3), pipeline_mode=one),
                  pl.BlockSpec((None, L, LANE), lambda b: (b, 0, 4 * W // LANE), pipeline_mode=one),
                  _full((8, 3 * W)), _full((1, LANE)), _full((1, LANE)), _full((1, W)), _full((W, W)),
                  _full((2, LANE, W)), _full((2, LANE, W)), _full((2, A_HEADS * CHUNK, LANE))],
        out_specs=pl.BlockSpec((None, L, W), lambda b: (b, 0, 0)),
        out_shape=jax.ShapeDtypeStruct((B, L, W), F32),
        scratch_shapes=[pltpu.VMEM((L, W), BF16), pltpu.VMEM((L, W), BF16), pltpu.VMEM((L, W), F32),
                        pltpu.VMEM((L, W), F32), pltpu.VMEM((L, W), F32),
                        pltpu.VMEM((2 * A_HEADS, A_HEAD_DIM, A_HEAD_DIM), F32)],
        compiler_params=_cparams(("parallel",)),
        name="gdn",
    )(a_in, a_in, a_in, convw, alog, dtb, anorm, ind, expb, expg, pick)


def _mla_prep_kernel(b_ref, tab_ref, qan_ref, kvan_ref, wqb_ref, wk_ref, wv_ref, qw_ref, kw_ref, knr_ref,
                     indq_ref, invn_ref, place_ref, q_out, k_out, v_out):
    b = b_ref[...]
    tab = tab_ref[...]
    cq = _rms(b[:, :B_Q_RANK], qan_ref[...])
    ckv = _rms(b[:, B_Q_RANK:B_Q_RANK + B_KV_RANK], kvan_ref[...])
    kr = b[:, B_Q_RANK + B_KV_RANK:]
    ind2 = indq_ref[...]
    invn = invn_ref[...]

    def seg_sums(x2):
        hi, lo = _split(x2)
        hp = lambda t, h: t[:, h * HEAD_PAD:(h + 1) * HEAD_PAD]
        return jnp.concatenate(
            [jnp.dot(jnp.concatenate([hp(hi, h), hp(lo, h)], axis=1), ind2, preferred_element_type=F32)
             for h in range(B_HEADS)], axis=1)

    q = _dot(cq, wqb_ref[...])
    qn = q * lax.rsqrt(seg_sums(q * q) * invn + EPS) * qw_ref[...]
    scale = (B_NOPE + B_ROPE) ** -0.5
    pieces = []
    for h in range(B_HEADS):
        x = qn[:, h * HEAD_PAD:(h + 1) * HEAD_PAD]
        half = B_ROPE // 2
        r = (x * tab[:, 0:LANE] + pltpu.roll(x, LANE - half, 1) * tab[:, LANE:2 * LANE]
             + pltpu.roll(x, half, 1) * tab[:, 2 * LANE:3 * LANE])
        pieces.append(r * scale)
    q_out[...] = jnp.concatenate(pieces, axis=1).astype(BF16)

    kk = _dot(ckv, wk_ref[...])
    kn = kk * lax.rsqrt(seg_sums(kk * kk) * invn + EPS) * kw_ref[...]
    krn = kr * lax.rsqrt(jnp.sum(kr * kr, axis=-1, keepdims=True) * (1.0 / B_ROPE) + EPS) * knr_ref[...]
    half = B_ROPE // 2
    krr = (krn * tab[:, 3 * LANE:4 * LANE] + pltpu.roll(krn, LANE - half, 1) * tab[:, 4 * LANE:5 * LANE]
           + pltpu.roll(krn, half, 1) * tab[:, 5 * LANE:6 * LANE])
    k_out[...] = (kn + jnp.dot(krr.astype(BF16), place_ref[...], preferred_element_type=F32)).astype(BF16)
    v_out[...] = _dot(ckv, wv_ref[...]).astype(BF16)


def _mla_prep(b_in, tab, qan, kvan, wqb, wk, wv, qw, kw, knr, indq, invn, place):
    B, L, _ = b_in.shape
    tm = TOK_TILE
    HW = B_HEADS * HEAD_PAD
    return pl.pallas_call(
        _mla_prep_kernel,
        grid=(B, L // tm),
        in_specs=[pl.BlockSpec((None, tm, B_PAD), lambda b, j: (b, j, 0)),
                  pl.BlockSpec((tm, 6 * LANE), lambda b, j: (j, 0)),
                  _full((1, B_Q_RANK)), _full((1, B_KV_RANK)), _full((B_Q_RANK, HW)), _full((B_KV_RANK, HW)),
                  _full((B_KV_RANK, B_WIDTH)), _full((1, HW)), _full((1, HW)), _full((1, LANE)),
                  _full((2 * HEAD_PAD, HEAD_PAD)), _full((1, HW)), _full((LANE, HW))],
        out_specs=[pl.BlockSpec((None, tm, HW), lambda b, j: (b, j, 0)),
                   pl.BlockSpec((None, tm, HW), lambda b, j: (b, j, 0)),
                   pl.BlockSpec((None, tm, B_WIDTH), lambda b, j: (b, j, 0))],
        out_shape=[jax.ShapeDtypeStruct((B, L, HW), BF16), jax.ShapeDtypeStruct((B, L, HW), BF16),
                   jax.ShapeDtypeStruct((B, L, B_WIDTH), BF16)],
        compiler_params=_cparams(("parallel", "parallel")),
        name="mla_prep",
    )(b_in, tab, qan, kvan, wqb, wk, wv, qw, kw, knr, indq, invn, place)


def _attn_kernel(q_ref, k_ref, v_ref, o_ref, v_aug):
    @pl.when(pl.program_id(2) == 0)
    def _():
        lane = lax.broadcasted_iota(jnp.int32, (v_aug.shape[0], LANE), 1)
        v_aug[:, 0:2 * B_V] = v_ref[...]
        v_aug[:, 2 * B_V:] = jnp.where(lane == 0, 1.0, 0.0).astype(BF16)

    scores = []
    for hh in range(2):
        q = q_ref[:, hh * HEAD_PAD:(hh + 1) * HEAD_PAD]
        k = k_ref[:, hh * HEAD_PAD:(hh + 1) * HEAD_PAD]
        scores.append(lax.dot_general(q, k, (((1,), (1,)), ((), ())), preferred_element_type=F32))
    outs = []
    for s in scores:
        p = jnp.exp((s - jnp.max(s, axis=-1, keepdims=True)).astype(BF16))
        pv = jnp.dot(p, v_aug[...], preferred_element_type=F32)
        outs.append(pv[:, 0:2 * B_V] / pv[:, 2 * B_V:2 * B_V + 1])
    lane = lax.broadcasted_iota(jnp.int32, outs[0].shape, 1)
    o_ref[...] = jnp.where(lane < B_V, outs[0], outs[1])


def _attn(q, k, v):
    B, L, _ = q.shape
    tq = min(1024, L)
    return pl.pallas_call(
        _attn_kernel,
        grid=(B, B_HEADS // 2, L // tq),
        in_specs=[pl.BlockSpec((None, tq, 2 * HEAD_PAD), lambda b, p, j: (b, j, p)),
                  pl.BlockSpec((None, L, 2 * HEAD_PAD), lambda b, p, j: (b, 0, p)),
                  pl.BlockSpec((None, L, 2 * B_V), lambda b, p, j: (b, 0, p))],
        out_specs=pl.BlockSpec((None, tq, 2 * B_V), lambda b, p, j: (b, j, p)),
        out_shape=jax.ShapeDtypeStruct((B, L, B_WIDTH), F32),
        scratch_shapes=[pltpu.VMEM((L, 2 * B_V + LANE), BF16)],
        compiler_params=_cparams(("parallel", "parallel", "arbitrary")),
        name="mla_attn",
    )(q, k, v)


def _hgrn_tables():
    C = CHUNK
    m_all = np.zeros((8 * C, C), np.float32)
    masks = np.zeros((7, C, C), np.float32)
    for lvl in range(6):
        m = 1 << lvl
        for t in range(C):
            bd = (t // (2 * m)) * 2 * m + m - 1
            if t % (2 * m) >= m:
                m_all[lvl * C + t, bd + 1:t + 1] = 1.0
            else:
                m_all[lvl * C + t, t + 1:bd + 1] = 1.0
            for s in range(C):
                if t // (2 * m) == s // (2 * m) and t % (2 * m) >= m and s % (2 * m) < m:
                    masks[lvl, t, s] = 1.0
    for t in range(C):
        m_all[6 * C + t, :t + 1] = 1.0
        m_all[7 * C + t, t + 1:] = 1.0
    masks[6] = np.eye(C, dtype=np.float32)
    m_f = m_all.reshape(8, C, C)
    m_b = m_f[:, ::-1, ::-1]
    mk_b = masks[:, ::-1, ::-1]
    m2 = np.stack([m_f.reshape(8 * C, C), m_b.reshape(8 * C, C)])
    return (np.concatenate([m2, m2], axis=2), np.stack([masks, mk_b]))


def _hgrn_kernel(q_ref, lff_ref, lfb_ref, kf_ref, kb_ref, i_ref, gate_ref, cnorm_ref, mall_ref, masks_ref, ind_ref,
                 out_ref, of_s, ob_s, s_s, *, L):
    nc = L // CHUNK
    C = CHUNK

    def wide(c, d):
        t0 = pl.multiple_of(c * C, C)
        logf = (lff_ref if d == 0 else lfb_ref)[pl.ds(t0, C), :]
        e_all = jnp.exp(_sel2_dot(mall_ref[d], logf))
        return dict(t0=t0, q=q_ref[pl.ds(t0, C), :], k=(kf_ref if d == 0 else kb_ref)[pl.ds(t0, C), :],
                    e=e_all, v=i_ref[pl.ds(t0, C), :])

    unroll = 2

    def scan_group(i, carry):
        wd = [[wide(unroll * i + u, 0), wide(nc - 1 - (unroll * i + u), 1)] for u in range(unroll)]
        dh = [(d, h) for d in (0, 1) for h in range(C_HEADS)]
        ks = [(u, d, h) for u in range(unroll) for d, h in dh]
        ls = lambda h: slice(h * C_EXPAND, (h + 1) * C_EXPAND)
        vs = lambda h: slice(h * C_HEAD_DIM, (h + 1) * C_HEAD_DIM)
        qb = {(u, d, h): wd[u][d]['q'][:, ls(h)] for u, d, h in ks}
        kb = {(u, d, h): wd[u][d]['k'][:, ls(h)] for u, d, h in ks}
        vh = {(u, d, h): wd[u][d]['v'][:, vs(h)] for u, d, h in ks}
        lvl_e = lambda u, d, h, j: wd[u][d]['e'][j * C:(j + 1) * C, ls(h)]
        attn = {(u, d, h): _dot_nt(qb[u, d, h], kb[u, d, h]) * masks_ref[d, 6] for u, d, h in ks}
        for lvl in range(6):
            for u, d, h in ks:
                e = lvl_e(u, d, h, lvl).astype(BF16)
                attn[u, d, h] = attn[u, d, h] + _dot_nt(qb[u, d, h] * e, kb[u, d, h] * e) * masks_ref[d, lvl]
        intra = {k: _dot(attn[k], vh[k]) for k in ks}
        qin = {(u, d, h): qb[u, d, h] * lvl_e(u, d, h, 6).astype(BF16) for u, d, h in ks}
        kout = {(u, d, h): _dot_tn(vh[u, d, h], kb[u, d, h] * lvl_e(u, d, h, 7).astype(BF16)) for u, d, h in ks}
        st = {(d, h): s_s[d * C_HEADS + h] for d, h in dh}
        for u in range(unroll):
            o = {(d, h): intra[u, d, h] + _dot_nt(qin[u, d, h], st[d, h]) for d, h in dh}
            for d, h in dh:
                e_in = lvl_e(u, d, h, 6)
                st[d, h] = st[d, h] * (e_in[C - 1:C] if d == 0 else e_in[0:1]) + kout[u, d, h]
            of_s[pl.ds(wd[u][0]['t0'], C), :] = jnp.concatenate([o[0, h] for h in range(C_HEADS)], axis=1)
            ob_s[pl.ds(wd[u][1]['t0'], C), :] = jnp.concatenate([o[1, h] for h in range(C_HEADS)], axis=1)
        for d, h in dh:
            s_s[d * C_HEADS + h] = st[d, h]
        return carry

    s_s[...] = jnp.zeros_like(s_s)
    lax.fori_loop(0, nc // unroll, scan_group, 0)

    rows = 256
    ind = ind_ref[...]

    def finish(i, carry):
        t0 = pl.multiple_of(i * rows, rows)
        o = of_s[pl.ds(t0, rows), :] + ob_s[pl.ds(t0, rows), :]
        ms = _dot_sel(o * o, ind) * (1.0 / C_HEAD_DIM)
        o = o * lax.rsqrt(ms + EPS) * cnorm_ref[...]
        out_ref[pl.ds(t0, rows), :] = o * _silu(gate_ref[pl.ds(t0, rows), :])
        return carry

    lax.fori_loop(0, L // rows, finish, 0)


def _hgrn(q, logf, kk, v, gate, cnorm, mall, masks, ind):
    B, L, _ = q.shape
    kern = functools.partial(_hgrn_kernel, L=L)
    one = None
    F, W = C_FDIM, C_WIDTH
    seq = lambda w, j: pl.BlockSpec((None, L, w), lambda b: (b, 0, j), pipeline_mode=one)
    return pl.pallas_call(
        kern,
        grid=(B,),
        in_specs=[seq(F, 0), seq(F, 0), seq(F, 1), seq(F, 0), seq(F, 1), seq(W, 0), seq(W, 0),
                  _full((1, W)), _full((2, 8 * CHUNK, 2 * CHUNK)), _full((2, 7, CHUNK, CHUNK)), _full((W, W))],
        out_specs=pl.BlockSpec((None, L, W), lambda b: (b, 0, 0)),
        out_shape=jax.ShapeDtypeStruct((B, L, W), F32),
        scratch_shapes=[pltpu.VMEM((L, W), F32), pltpu.VMEM((L, W), F32),
                        pltpu.VMEM((2 * C_HEADS, C_HEAD_DIM, C_EXPAND), F32)],
        compiler_params=_cparams(("parallel",)),
        name="hgrn2",
    )(q, logf, logf, kk, kk, v, gate, cnorm, mall, masks, ind)


def _outproj_kernel(xa_ref, xb_ref, oa_ref, ob_ref, oc_ref, wa_ref, wb_ref, wc_ref, ln_ref, wr_ref, br_ref, tri_ref,
                    x1_ref, h_ref, route_ref, cnt_ref, cnt_s, *, n_first):
    x1 = (_pair_load(xa_ref, xb_ref, n_first) + _dot(oa_ref[...], wa_ref[...]) + _dot(ob_ref[...], wb_ref[...])
          + _dot(oc_ref[...], wc_ref[...]))
    x1_ref[...] = x1
    h = _rms(x1, ln_ref[...])
    h_ref[...] = h
    hh, hl = _split(h)
    wr = wr_ref[...]
    wh, wl = _split(wr)
    hw = jnp.dot(hh, jnp.concatenate([wh, wl], axis=1), preferred_element_type=F32)
    lg = hw[:, :LANE] + hw[:, LANE:] + jnp.dot(hl, wh, preferred_element_type=F32) + br_ref[...]
    lane = lax.broadcasted_iota(jnp.int32, lg.shape, 1).astype(F32)
    neg = -1e30
    far = 1e9
    gmask = lane < N_GROUPS
    gl = jnp.where(gmask, lg, neg)
    gmax = jnp.max(gl, axis=-1, keepdims=True)
    g_idx = jnp.min(jnp.where(gl == gmax, lane, far), axis=-1, keepdims=True)
    g_w = 1.0 / jnp.sum(jnp.where(gmask, jnp.exp(gl - gmax), 0.0), axis=-1, keepdims=True)
    e_lane = lane - N_GROUPS
    emask = (e_lane >= g_idx * EXPERTS_PER_GROUP) & (e_lane < (g_idx + 1.0) * EXPERTS_PER_GROUP)
    el = jnp.where(emask, lg, neg)
    m1 = jnp.max(el, axis=-1, keepdims=True)
    i1 = jnp.min(jnp.where(el == m1, lane, far), axis=-1, keepdims=True)
    el2 = jnp.where(lane == i1, neg, el)
    m2 = jnp.max(el2, axis=-1, keepdims=True)
    i2 = jnp.min(jnp.where((el2 == m2) & emask & (lane != i1), lane, far), axis=-1, keepdims=True)
    r = jnp.exp(m2 - m1)
    w1 = g_w / (1.0 + r)
    w2 = g_w * r / (1.0 + r)
    @pl.when(pl.program_id(0) == 0)
    def _():
        cnt_s[...] = jnp.zeros_like(cnt_s)

    oh1 = jnp.where(lane == i1, 1.0, 0.0)
    oh2 = jnp.where(lane == i2, 1.0, 0.0)
    tri = tri_ref[...]
    run = cnt_s[0:1, :]
    tot1 = jnp.sum(oh1, axis=0, keepdims=True)
    c1 = run + jnp.dot(tri, oh1.astype(BF16), preferred_element_type=F32)
    c2 = run + tot1 + jnp.dot(tri, oh2.astype(BF16), preferred_element_type=F32)
    r1 = jnp.sum(oh1 * c1, axis=-1, keepdims=True)
    r2 = jnp.sum(oh2 * c2, axis=-1, keepdims=True)
    cnt = jnp.broadcast_to(run + tot1 + jnp.sum(oh2, axis=0, keepdims=True), cnt_s.shape)
    cnt_s[...] = cnt
    cnt_ref[...] = cnt
    vals = (i1 - N_GROUPS, i2 - N_GROUPS, w1, w2, r1, r2)
    route = jnp.zeros_like(lg)
    for j, v in enumerate(vals):
        route = jnp.where(lane == float(j), v, route)
    route_ref[...] = route


def _outproj(xa, xb, oa, ob, oc, wa, wb, wc, ln, wr, br, tri):
    tm = TOK_TILE
    n_first = xa.shape[0] // tm
    T = xa.shape[0] + xb.shape[0]
    row = lambda w: pl.BlockSpec((tm, w), lambda i: (i, 0))
    return pl.pallas_call(
        functools.partial(_outproj_kernel, n_first=n_first),
        grid=(T // tm,),
        in_specs=_pair_specs(n_first, tm) + [row(A_WIDTH), row(B_WIDTH), row(C_WIDTH),
                  _full((A_WIDTH, D_MODEL)), _full((B_WIDTH, D_MODEL)), _full((C_WIDTH, D_MODEL)),
                  _full((1, D_MODEL)), _full((D_MODEL, LANE)), _full((1, LANE)), _full((tm, tm))],
        out_specs=[row(D_MODEL), row(D_MODEL), row(LANE), _full((8, LANE))],
        out_shape=[jax.ShapeDtypeStruct((T, D_MODEL), F32), jax.ShapeDtypeStruct((T, D_MODEL), F32),
                   jax.ShapeDtypeStruct((T, LANE), F32), jax.ShapeDtypeStruct((8, LANE), F32)],
        scratch_shapes=[pltpu.VMEM((8, LANE), F32)],
        compiler_params=_cparams(("arbitrary",)),
        name="outproj_router",
    )(xa, xb, oa, ob, oc, wa, wb, wc, ln, wr, br, tri)


def _gather_rows(src_hbm, buf, sem, cur_ids, nxt_ids, n_rows):
    i = pl.program_id(0)
    slot = i % 2

    def row_copy(ids, r, s):
        return pltpu.make_async_copy(src_hbm.at[pl.ds(ids[0, 0, r], 1)], buf.at[s, pl.ds(r, 1)], sem.at[s])

    def start_all(ids, s):
        def body(r, carry):
            row_copy(ids, r, s).start()
            return carry
        lax.fori_loop(0, n_rows, body, 0, unroll=8)

    def wait_all(s):
        pltpu.make_async_copy(src_hbm.at[pl.ds(0, n_rows)], buf.at[s], sem.at[s]).wait()

    @pl.when(i == 0)
    def _():
        start_all(cur_ids, 0)

    start_all(nxt_ids, 1 - slot)
    wait_all(slot)

    @pl.when(i == pl.num_programs(0) - 1)
    def _():
        wait_all(1 - slot)

    return slot


def _scatter_kernel(pends_ref, pos_ref, h_ref, xs_hbm, zero_s, sem):
    tm = TOK_TILE

    @pl.when(pl.program_id(0) == 0)
    def _():
        zero_s[...] = jnp.zeros_like(zero_s)

        def fill(e):
            start = pl.multiple_of(jnp.maximum(pends_ref[e] - MOE_ROWS, 0), MOE_ROWS)
            return pltpu.make_async_copy(zero_s, xs_hbm.at[pl.ds(start, MOE_ROWS)], sem)

        for e in range(N_EXPERTS):
            fill(e).start()
            fill(e).wait()

        def fill_tail(b, carry):
            cp = pltpu.make_async_copy(zero_s, xs_hbm.at[pl.ds(pl.multiple_of(b * MOE_ROWS, MOE_ROWS), MOE_ROWS)], sem)
            cp.start()
            cp.wait()
            return carry

        lax.fori_loop(pends_ref[N_EXPERTS - 1] // MOE_ROWS, xs_hbm.shape[0] // MOE_ROWS, fill_tail, 0)

    def row_copy(r, k):
        return pltpu.make_async_copy(h_ref.at[pl.ds(r, 1)], xs_hbm.at[pl.ds(pos_ref[0, 0, k * tm + r], 1)], sem)

    def issue(r, carry):
        for k in range(TOP_K):
            row_copy(r, k).start()
        return carry

    lax.fori_loop(0, tm, issue, 0, unroll=8)
    for k in range(TOP_K):
        pltpu.make_async_copy(h_ref, xs_hbm.at[pl.ds(0, tm)], sem).wait()


def _scatter(pends, pos, h):
    T = h.shape[0]
    tm = TOK_TILE
    grid_spec = pltpu.PrefetchScalarGridSpec(
        num_scalar_prefetch=1,
        grid=(T // tm,),
        in_specs=[pl.BlockSpec((1, 1, TOP_K * tm), lambda i, pe: (i, 0, 0), memory_space=pltpu.SMEM),
                  pl.BlockSpec((tm, D_MODEL), lambda i, pe: (i, 0))],
        out_specs=pl.BlockSpec(memory_space=pl.ANY),
        scratch_shapes=[pltpu.VMEM((MOE_ROWS, D_MODEL), F32), pltpu.SemaphoreType.DMA(())],
    )
    return pl.pallas_call(
        _scatter_kernel,
        grid_spec=grid_spec,
        out_shape=jax.ShapeDtypeStruct((T * TOP_K + N_EXPERTS * MOE_ROWS, D_MODEL), F32),
        compiler_params=_cparams(("arbitrary",)),
        name="moe_scatter",
    )(pends, pos, h)


def _moe_kernel(blk_e_ref, nact_ref, x_ref, w1_ref, w3_ref, w2_ref, y_ref, w1_s, w3_s, w2_s):
    i = pl.program_id(0)
    active = i < nact_ref[0]

    @pl.when((i == 0) | (blk_e_ref[i] != blk_e_ref[jnp.maximum(i - 1, 0)]))
    def _():
        w1_s[...] = w1_ref[...].astype(BF16)
        w3_s[...] = w3_ref[...].astype(BF16)
        w2_s[...] = w2_ref[...].astype(BF16)

    @pl.when(active)
    def _():
        x = x_ref[...].astype(BF16)
        a = jnp.dot(x, w1_s[...], preferred_element_type=F32)
        g = jnp.dot(x, w3_s[...], preferred_element_type=F32)
        y_ref[...] = jnp.dot((_silu(a) * g).astype(BF16), w2_s[...], preferred_element_type=F32)

    @pl.when(jnp.logical_not(active))
    def _():
        y_ref[...] = jnp.zeros_like(y_ref)


def _moe(blk_e, nact, xs, w1, w3, w2, layer):
    P = xs.shape[0]
    nblk = P // MOE_ROWS
    wspec = lambda a, b: pl.BlockSpec((None, None, a, b), lambda i, be, na: (layer, be[i], 0, 0))
    grid_spec = pltpu.PrefetchScalarGridSpec(
        num_scalar_prefetch=2,
        grid=(nblk,),
        in_specs=[pl.BlockSpec((MOE_ROWS, D_MODEL), lambda i, be, na: (jnp.minimum(i, na[0] - 1), 0)),
                  wspec(D_MODEL, D_EXPERT), wspec(D_MODEL, D_EXPERT), wspec(D_EXPERT, D_MODEL)],
        out_specs=pl.BlockSpec((MOE_ROWS, D_MODEL), lambda i, be, na: (i, 0)),
        scratch_shapes=[pltpu.VMEM((D_MODEL, D_EXPERT), BF16), pltpu.VMEM((D_MODEL, D_EXPERT), BF16),
                        pltpu.VMEM((D_EXPERT, D_MODEL), BF16)],
    )
    return pl.pallas_call(
        _moe_kernel,
        grid_spec=grid_spec,
        out_shape=jax.ShapeDtypeStruct((P, D_MODEL), F32),
        compiler_params=_cparams(("arbitrary",)),
        name="moe_ffn",
    )(blk_e, nact, xs, w1, w3, w2)


def _combine_kernel(pos_ref, nxt_ref, x_ref, route_ref, y_hbm, outa_ref, outb_ref, yb, sem, *, n_first):
    tm = x_ref.shape[0]
    slot = _gather_rows(y_hbm, yb, sem, pos_ref, nxt_ref, TOP_K * tm)
    route = route_ref[...]
    val = x_ref[...] + yb[slot, 0:tm, :] * route[:, 2:3] + yb[slot, tm:2 * tm, :] * route[:, 3:4]
    first = pl.program_id(0) < n_first

    @pl.when(first)
    def _():
        outa_ref[...] = val

    @pl.when(jnp.logical_not(first))
    def _():
        outb_ref[...] = val


def _combine(pos, x1, route, y, n_first):
    T = x1.shape[0]
    tm = TOK_TILE
    nt = T // tm
    return pl.pallas_call(
        functools.partial(_combine_kernel, n_first=n_first),
        grid=(nt,),
        in_specs=[pl.BlockSpec((1, 1, TOP_K * tm), lambda i: (i, 0, 0), memory_space=pltpu.SMEM),
                  pl.BlockSpec((1, 1, TOP_K * tm), lambda i: (jnp.minimum(i + 1, nt - 1), 0, 0),
                               memory_space=pltpu.SMEM),
                  pl.BlockSpec((tm, D_MODEL), lambda i: (i, 0)),
                  pl.BlockSpec((tm, LANE), lambda i: (i, 0)),
                  pl.BlockSpec(memory_space=pl.ANY)],
        out_specs=_pair_specs(n_first, tm),
        out_shape=[jax.ShapeDtypeStruct((n_first * tm, D_MODEL), F32),
                   jax.ShapeDtypeStruct((T - n_first * tm, D_MODEL), F32)],
        scratch_shapes=[pltpu.VMEM((2, TOP_K * tm, D_MODEL), F32), pltpu.SemaphoreType.DMA((2,))],
        compiler_params=_cparams(("arbitrary",)),
        name="moe_combine",
    )(pos, pos, x1, route, y)


def _dispatch(route, cnt):
    T = route.shape[0]
    nblk = (T * TOP_K) // MOE_ROWS + N_EXPERTS
    counts = cnt[0, N_GROUPS:N_GROUPS + N_EXPERTS].astype(jnp.int32)
    padded = (counts + MOE_ROWS - 1) // MOE_ROWS * MOE_ROWS
    pends = jnp.cumsum(padded)
    pstarts = pends - padded
    e = route[:, 0:TOP_K].astype(jnp.int32)
    rank = route[:, 4:4 + TOP_K].astype(jnp.int32)
    sel = e[:, :, None] == jnp.arange(N_EXPERTS, dtype=jnp.int32)
    pos = rank + jnp.sum(jnp.where(sel, pstarts, 0), axis=-1)
    pos = jnp.transpose(pos.reshape(T // TOK_TILE, TOK_TILE, TOP_K), (0, 2, 1))
    pos = pos.reshape(T // TOK_TILE, 1, TOP_K * TOK_TILE)
    blk_start = jnp.arange(nblk, dtype=jnp.int32) * MOE_ROWS
    blk_e = jnp.minimum(jnp.sum(blk_start[:, None] >= pends[None, :], axis=-1), N_EXPERTS - 1).astype(jnp.int32)
    nact = (pends[-1:] // MOE_ROWS).astype(jnp.int32)
    return blk_e, nact, pos, pends.astype(jnp.int32)


def _block_ind(width, seg):
    i = np.arange(width)
    return (i[:, None] // seg == i[None, :] // seg).astype(np.float32)


def _rope_tables(L):
    half = B_ROPE // 2
    inv = ROPE_THETA ** (-jnp.arange(0, B_ROPE, 2, dtype=F32) / B_ROPE)
    ang = jnp.arange(L, dtype=F32)[:, None] * inv[None, :]
    cos, sin = jnp.cos(ang), jnp.sin(ang)
    z = lambda n: jnp.zeros((L, n), F32)
    one = jnp.ones((L, B_NOPE), F32)
    rest = LANE - B_NOPE - B_ROPE
    cq = jnp.concatenate([one, cos, cos, z(rest)], axis=1)
    s1q = jnp.concatenate([z(B_NOPE), -sin, z(half), z(rest)], axis=1)
    s2q = jnp.concatenate([z(B_NOPE), z(half), sin, z(rest)], axis=1)
    ck = jnp.concatenate([cos, cos, z(LANE - B_ROPE)], axis=1)
    s1k = jnp.concatenate([-sin, z(LANE - half)], axis=1)
    s2k = jnp.concatenate([z(half), sin, z(LANE - B_ROPE)], axis=1)
    return jnp.concatenate([cq, s1q, s2q, ck, s1k, s2k], axis=1)


def _layer_params(l, ln1, w_in, conv_w, a_log, dt_bias, a_norm, q_a_norm, w_qb, kv_a_norm, w_kvb, qn_nope, qn_rope,
                  kn_nope, kn_rope, lbs, c_norm, w_out, ln2, w_group, b_group, w_router, b_router, w1, w3, w2):
    p = {}
    w = w_in[l]
    wp = jnp.zeros((D_MODEL, N_PAD), F32)
    wp = wp.at[:, 0:A_IN].set(w[:, 0:A_IN])
    wp = wp.at[:, A_PAD:A_PAD + B_IN].set(w[:, A_IN:A_IN + B_IN])
    wp = wp.at[:, A_PAD + B_PAD:].set(w[:, A_IN + B_IN:])
    p['w_in'] = wp.astype(BF16)
    p['ln1'] = ln1[l].reshape(1, D_MODEL)
    p['convw'] = jnp.zeros((8, 3 * A_WIDTH), F32).at[0:CONV_K].set(conv_w[l])
    gpad = lambda v: jnp.zeros((1, LANE), F32).at[0, 2 * A_HEADS:4 * A_HEADS].set(v.reshape(-1))
    p['alog'] = gpad(a_log[l])
    p['dtb'] = gpad(dt_bias[l])
    p['anorm'] = jnp.tile(a_norm[l], A_HEADS).reshape(1, A_WIDTH)
    half = B_ROPE // 2
    wq = w_qb[l].reshape(B_Q_RANK, B_HEADS, B_NOPE + B_ROPE)
    wq = jnp.pad(wq, ((0, 0), (0, 0), (0, HEAD_PAD - B_NOPE - B_ROPE)))
    p['wqb'] = wq.reshape(B_Q_RANK, B_HEADS * HEAD_PAD).astype(BF16)
    wkv = w_kvb[l].reshape(B_KV_RANK, B_HEADS, B_NOPE + B_V)
    wk = jnp.pad(wkv[:, :, :B_NOPE], ((0, 0), (0, 0), (0, HEAD_PAD - B_NOPE)))
    p['wk'] = wk.reshape(B_KV_RANK, B_HEADS * HEAD_PAD).astype(BF16)
    p['wv'] = wkv[:, :, B_NOPE:].reshape(B_KV_RANK, B_WIDTH).astype(BF16)
    zpad = jnp.zeros((HEAD_PAD - B_NOPE - B_ROPE,), F32)
    p['qw'] = jnp.tile(jnp.concatenate([qn_nope[l], qn_rope[l], zpad]), B_HEADS).reshape(1, -1)
    p['kw'] = jnp.tile(jnp.concatenate([kn_nope[l], jnp.zeros((HEAD_PAD - B_NOPE,), F32)]), B_HEADS).reshape(1, -1)
    p['knr'] = jnp.concatenate([kn_rope[l], jnp.zeros((LANE - B_ROPE,), F32)]).reshape(1, LANE)
    p['qan'] = q_a_norm[l].reshape(1, -1)
    p['kvan'] = kv_a_norm[l].reshape(1, -1)
    p['lb'] = lbs[l]
    p['cnorm'] = jnp.tile(c_norm[l], C_HEADS).reshape(1, C_WIDTH)
    wo = w_out[l].astype(BF16)
    p['wa'], p['wb'], p['wc'] = wo[:A_WIDTH], wo[A_WIDTH:A_WIDTH + B_WIDTH], wo[A_WIDTH + B_WIDTH:]
    p['ln2'] = ln2[l].reshape(1, D_MODEL)
    wr = jnp.zeros((D_MODEL, LANE), F32).at[:, :N_GROUPS].set(w_group[l])
    p['wr'] = wr.at[:, N_GROUPS:N_GROUPS + N_EXPERTS].set(w_router[l])
    br = jnp.zeros((1, LANE), F32).at[0, :N_GROUPS].set(b_group[l])
    p['br'] = br.at[0, N_GROUPS:N_GROUPS + N_EXPERTS].set(b_router[l])
    p['w1'], p['w3'], p['w2'], p['layer'] = w1, w3, w2, l
    return p


def _static_tables():
    t = {}
    t['ind_a'] = jnp.asarray(_block_ind(A_WIDTH, A_HEAD_DIM), BF16)
    t['ind_c'] = jnp.asarray(_block_ind(C_WIDTH, C_HEAD_DIM), BF16)
    expb = np.zeros((2, LANE, A_WIDTH), np.float32)
    expg = np.zeros((2, LANE, A_WIDTH), np.float32)
    for d in range(2):
        for h in range(A_HEADS):
            expb[d, A_HEADS * d + h, h * A_HEAD_DIM:(h + 1) * A_HEAD_DIM] = 1.0
            expg[d, 2 * A_HEADS + A_HEADS * d + h, h * A_HEAD_DIM:(h + 1) * A_HEAD_DIM] = 1.0
    t['expb'] = jnp.asarray(expb, BF16)
    t['expg'] = jnp.asarray(expg, BF16)
    t['pick'] = jnp.asarray(np.repeat(np.transpose(expg[:, :, ::A_HEAD_DIM], (0, 2, 1)), CHUNK, axis=1), BF16)
    HW = B_HEADS * HEAD_PAD
    lane = np.arange(HW)
    seg = np.where(lane % HEAD_PAD < B_NOPE, 0, np.where(lane % HEAD_PAD < B_NOPE + B_ROPE, 1, 2))
    head = lane // HEAD_PAD
    seg1 = seg[:HEAD_PAD]
    ind1 = ((seg1[:, None] == seg1[None, :]) & (seg1[:, None] < 2)).astype(np.float32)
    t['indq'] = jnp.asarray(np.concatenate([ind1, ind1], axis=0), BF16)
    t['invn'] = jnp.asarray(np.where(seg == 0, 1.0 / B_NOPE, 1.0 / B_ROPE).astype(np.float32)).reshape(1, HW)
    place = np.zeros((LANE, HW), np.float32)
    for h in range(B_HEADS):
        for r in range(B_ROPE):
            place[r, h * HEAD_PAD + B_NOPE + r] = 1.0
    t['place'] = jnp.asarray(place, BF16)
    t['tri'] = jnp.asarray(np.tril(np.ones((TOK_TILE, TOK_TILE), np.float32), -1), BF16)
    mall, masks = _hgrn_tables()
    t['mall'] = jnp.asarray(mall, BF16)
    t['masks'] = jnp.asarray(masks, F32)
    return t


def _trunk(x_first, x_second, params, tabs, rope):
    L, D = x_first.shape[1:]
    B = x_first.shape[0] + x_second.shape[0]
    T = B * L
    xa, xb = x_first.reshape(-1, D), x_second.reshape(-1, D)
    n_first = xa.shape[0] // TOK_TILE
    for p in params:
        a_in, b_in, c_q, c_logf, c_k, c_v, c_gate = _proj(xa, xb, p['ln1'], p['w_in'], p['lb'])
        o_a = _gdn(a_in.reshape(B, L, A_PAD), p['convw'], p['alog'], p['dtb'], p['anorm'], tabs['ind_a'],
                   tabs['expb'], tabs['expg'], tabs['pick'])
        q, k, v = _mla_prep(b_in.reshape(B, L, B_PAD), rope, p['qan'], p['kvan'], p['wqb'], p['wk'], p['wv'],
                            p['qw'], p['kw'], p['knr'], tabs['indq'], tabs['invn'], tabs['place'])
        o_b = _attn(q, k, v)
        seq = lambda t: t.reshape(B, L, t.shape[-1])
        o_c = _hgrn(seq(c_q), seq(c_logf), seq(c_k), seq(c_v), seq(c_gate), p['cnorm'], tabs['mall'],
                    tabs['masks'], tabs['ind_c'])
        x1, h, route, cnt = _outproj(xa, xb, o_a.reshape(T, A_WIDTH), o_b.reshape(T, B_WIDTH),
                                     o_c.reshape(T, C_WIDTH), p['wa'], p['wb'], p['wc'], p['ln2'], p['wr'], p['br'],
                                     tabs['tri'])
        blk_e, nact, pos, pends = _dispatch(route, cnt)
        xs = _scatter(pends, pos, h)
        y = _moe(blk_e, nact, xs, p['w1'], p['w3'], p['w2'], p['layer'])
        xa, xb = _combine(pos, x1, route, y, n_first)
    return xa.reshape(x_first.shape), xb.reshape(x_second.shape)


def kernel(x_prompt, x_sample, ln1, w_in, conv_w, a_log, dt_bias, a_norm, q_a_norm, w_qb, kv_a_norm, w_kvb, qn_nope,
           qn_rope, kn_nope, kn_rope, c_lower_bounds, c_norm, w_out, ln2, w_group, b_group, w_router, b_router,
           w1, w3, w2):
    lbs = jnp.cumsum(jax.nn.softmax(c_lower_bounds.astype(F32), axis=0), axis=0)
    lbs = lbs - lbs[0:1]
    params = [_layer_params(l, ln1, w_in, conv_w, a_log, dt_bias, a_norm, q_a_norm, w_qb, kv_a_norm, w_kvb, qn_nope,
                            qn_rope, kn_nope, kn_rope, lbs, c_norm, w_out, ln2, w_group, b_group, w_router, b_router,
                            w1, w3, w2) for l in range(DEPTH)]
    tabs = _static_tables()
    assert x_prompt.shape[1:] == x_sample.shape[1:]
    rope = _rope_tables(x_prompt.shape[1])
    assert (x_prompt.shape[0] * x_prompt.shape[1]) % TOK_TILE == 0
    return _trunk(x_prompt, x_sample, params, tabs, rope)
```

```python
import functools
import math

import numpy as np
import jax
import jax.numpy as jnp
from jax import lax
from jax.experimental import pallas as pl
from jax.experimental.pallas import tpu as pltpu

F32 = jnp.float32
BF16 = jnp.bfloat16

D_MODEL = 1024
DEPTH = 2
EPS = 1e-6
A_HEADS, A_HEAD_DIM, CONV_K = 6, 64, 5
A_WIDTH = A_HEADS * A_HEAD_DIM
B_HEADS, B_Q_RANK, B_KV_RANK, B_NOPE, B_ROPE, B_V = 6, 256, 128, 64, 32, 64
B_WIDTH = B_HEADS * B_V
ROPE_THETA = 10000.0
C_HEADS, C_EXPAND, C_HEAD_DIM = 4, 128, 64
C_WIDTH = C_HEADS * C_HEAD_DIM
C_FDIM = C_HEADS * C_EXPAND
D_MIX = A_WIDTH + B_WIDTH + C_WIDTH
A_IN = 4 * A_WIDTH + 4 * A_HEADS
B_IN = B_Q_RANK + B_KV_RANK + B_ROPE
C_IN = 3 * C_FDIM + 2 * C_WIDTH
N_GROUPS, EXPERTS_PER_GROUP, TOP_K, D_EXPERT = 4, 8, 2, 512
N_EXPERTS = N_GROUPS * EXPERTS_PER_GROUP

LANE = 128
CHUNK = 64
A_PAD = 4 * A_WIDTH + LANE
B_PAD = 512
C_PAD = C_IN
N_PAD = A_PAD + B_PAD + C_PAD
HEAD_PAD = LANE
MOE_ROWS = 512
TOK_TILE = 512
VMEM_LIMIT = 56 * 1024 * 1024


def _cparams(sem):
    return pltpu.CompilerParams(dimension_semantics=sem, vmem_limit_bytes=VMEM_LIMIT)


def _dot(a, b):
    return jnp.dot(a.astype(BF16), b.astype(BF16), preferred_element_type=F32)


def _dot_nt(a, b):
    return lax.dot_general(a.astype(BF16), b.astype(BF16), (((1,), (1,)), ((), ())), preferred_element_type=F32)


def _dot_tn(a, b):
    return lax.dot_general(a.astype(BF16), b.astype(BF16), (((0,), (0,)), ((), ())), preferred_element_type=F32)


def _split(x):
    hi = x.astype(BF16)
    lo = (x - hi.astype(F32)).astype(BF16)
    return hi, lo


def _sel_dot(sel, x):
    hi, lo = _split(x)
    return jnp.dot(sel, hi, preferred_element_type=F32) + jnp.dot(sel, lo, preferred_element_type=F32)


def _sel2_dot(sel2, x):
    hi, lo = _split(x)
    return jnp.dot(sel2, jnp.concatenate([hi, lo], axis=0), preferred_element_type=F32)


def _dot_sel(x, sel):
    hi, lo = _split(x)
    return jnp.dot(hi, sel, preferred_element_type=F32) + jnp.dot(lo, sel, preferred_element_type=F32)


def _sigmoid(x):
    return 1.0 / (1.0 + jnp.exp(-x))


def _silu(x):
    return x * _sigmoid(x)


def _softplus(x):
    return jnp.maximum(x, 0.0) + jnp.log(1.0 + jnp.exp(-jnp.abs(x)))


def _rms(x, w):
    return x * lax.rsqrt(jnp.mean(x * x, axis=-1, keepdims=True) + EPS) * w


def _full(shape):
    return pl.BlockSpec(shape, lambda *_: (0,) * len(shape))


def _pair_specs(n_first, tm):
    return [pl.BlockSpec((tm, D_MODEL), lambda i, *_: (jnp.minimum(i, n_first - 1), 0)),
            pl.BlockSpec((tm, D_MODEL), lambda i, *_: (jnp.maximum(i - n_first, 0), 0))]


def _pair_load(xa_ref, xb_ref, n_first):
    return jnp.where(pl.program_id(0) < n_first, xa_ref[...], xb_ref[...])


def _proj_kernel(xa_ref, xb_ref, ln_ref, w_ref, lb_ref, *refs, n_first):
    mla_refs = refs[:12]
    a_ref, bq_ref, bk_ref, bv_ref, cq_ref, clf_ref, ck_ref, cv_ref, cg_ref = refs[12:]
    h = _rms(_pair_load(xa_ref, xb_ref, n_first), ln_ref[...]).astype(BF16)
    a_ref[...] = jnp.dot(h, w_ref[:, 0:A_PAD], preferred_element_type=F32)
    _mla_prep(jnp.dot(h, w_ref[:, A_PAD:A_PAD + B_PAD], preferred_element_type=F32), *mla_refs,
              bq_ref, bk_ref, bv_ref)
    c0 = A_PAD + B_PAD
    F = C_FDIM
    cq_ref[...] = _silu(jnp.dot(h, w_ref[:, c0:c0 + F], preferred_element_type=F32)).astype(BF16)
    for d in range(2):
        fp = jnp.dot(h, w_ref[:, c0 + (1 + d) * F:c0 + (2 + d) * F], preferred_element_type=F32)
        lb = lb_ref[d:d + 1, :]
        e = jnp.exp(-jnp.abs(fp))
        r = 1.0 / (1.0 + e)
        lsig = jnp.minimum(fp, 0.0) - jnp.log(1.0 + e)
        a = jnp.log(lb)
        bb = jnp.log(1.0 - lb) + lsig
        clf_ref[:, d * F:(d + 1) * F] = jnp.maximum(a, bb) + jnp.log(1.0 + jnp.exp(-jnp.abs(a - bb)))
        ck_ref[:, d * F:(d + 1) * F] = ((1.0 - lb) * jnp.where(fp >= 0.0, e, 1.0) * r).astype(BF16)
    tail = jnp.dot(h, w_ref[:, c0 + 3 * F:N_PAD], preferred_element_type=F32)
    cv_ref[...] = tail[:, :C_WIDTH].astype(BF16)
    cg_ref[...] = tail[:, C_WIDTH:]


def _proj(xa, xb, ln, w_pad, lb, seq_len, tab, qan, kvan, wqb, wk, wv, qw, kw, knr, indq, invn, place):
    tm = TOK_TILE
    n_first = xa.shape[0] // tm
    T = xa.shape[0] + xb.shape[0]
    HW = B_HEADS * HEAD_PAD
    row = lambda w: pl.BlockSpec((tm, w), lambda i: (i, 0))
    sds = jax.ShapeDtypeStruct
    mla_specs = [pl.BlockSpec((tm, 6 * LANE), lambda i: (i % (seq_len // tm), 0)),
                 _full((1, B_Q_RANK)), _full((1, B_KV_RANK)), _full((B_Q_RANK, HW)), _full((B_KV_RANK, HW)),
                 _full((B_KV_RANK, B_WIDTH)), _full((1, HW)), _full((1, HW)), _full((1, LANE)),
                 _full((2 * HEAD_PAD, HEAD_PAD)), _full((1, HW)), _full((LANE, HW))]
    return pl.pallas_call(
        functools.partial(_proj_kernel, n_first=n_first),
        grid=(T // tm,),
        in_specs=(_pair_specs(n_first, tm) + [_full((1, D_MODEL)), _full((D_MODEL, N_PAD)), _full((2, C_FDIM))]
                  + mla_specs),
        out_specs=[row(A_PAD), row(HW), row(HW), row(B_WIDTH), row(C_FDIM), row(2 * C_FDIM), row(2 * C_FDIM),
                   row(C_WIDTH), row(C_WIDTH)],
        out_shape=[sds((T, A_PAD), F32), sds((T, HW), BF16), sds((T, HW), BF16), sds((T, B_WIDTH), BF16),
                   sds((T, C_FDIM), BF16), sds((T, 2 * C_FDIM), F32), sds((T, 2 * C_FDIM), BF16),
                   sds((T, C_WIDTH), BF16), sds((T, C_WIDTH), F32)],
        compiler_params=_cparams(("parallel",)),
        name="proj",
    )(xa, xb, ln, w_pad, lb, tab, qan, kvan, wqb, wk, wv, qw, kw, knr, indq, invn, place)


def _gdn_kernel(qkv_ref, gate_ref, gates_ref, convw_ref, alog_ref, dtb_ref, anorm_ref, ind_ref, expb_ref, expg_ref,
                out_ref, q_s, k_s, v_s, of_s, ob_s, s_s, *, L):
    nc = L // CHUNK
    W = A_WIDTH
    row = lax.broadcasted_iota(jnp.int32, (CHUNK, CHUNK), 0)
    col = lax.broadcasted_iota(jnp.int32, (CHUNK, CHUNK), 1)
    lane = lax.broadcasted_iota(jnp.int32, (CHUNK, LANE), 1)
    row2 = lax.broadcasted_iota(jnp.int32, (CHUNK, 2 * CHUNK), 0)
    col2 = lax.broadcasted_iota(jnp.int32, (CHUNK, 2 * CHUNK), 1) % CHUNK
    ind = ind_ref[...]

    def conv_chunk(c, carry):
        t0 = pl.multiple_of(c * CHUNK, CHUNK)
        tp = pl.multiple_of(jnp.maximum(t0 - 8, 0), 8)
        tn = pl.multiple_of(jnp.minimum(t0 + CHUNK, L - 8), 8)
        for g in range(3):
            cs = slice(g * W, (g + 1) * W)
            main = qkv_ref[pl.ds(t0, CHUNK), cs]
            prev = jnp.where(c > 0, qkv_ref[pl.ds(tp, 8), cs], 0.0)
            nxt = jnp.where(c < nc - 1, qkv_ref[pl.ds(tn, 8), cs], 0.0)
            win = jnp.concatenate([prev, main, nxt], axis=0)
            acc = win * convw_ref[2:3, cs]
            for j in (0, 1, 3, 4):
                acc = acc + pltpu.roll(win, (2 - j) % (CHUNK + 16), 0) * convw_ref[j:j + 1, cs]
            y = _silu(acc[8:8 + CHUNK])
            if g < 2:
                y = y * lax.rsqrt(_dot_sel(y * y, ind) + EPS)
            if g == 0:
                q_s[pl.ds(t0, CHUNK), :] = (y * (A_HEAD_DIM ** -0.5)).astype(BF16)
            elif g == 1:
                k_s[pl.ds(t0, CHUNK), :] = y.astype(BF16)
            else:
                v_s[pl.ds(t0, CHUNK), :] = y
        return carry

    lax.fori_loop(0, nc, conv_chunk, 0)

    coef = -jnp.exp(alog_ref[...])
    dtb = dtb_ref[...]
    g_lane = (lane >= 2 * A_HEADS) & (lane < 4 * A_HEADS)

    def wide(c, d):
        t0 = pl.multiple_of(c * CHUNK, CHUNK)
        qc = q_s[pl.ds(t0, CHUNK), :].astype(F32)
        kc = k_s[pl.ds(t0, CHUNK), :].astype(F32)
        vc = v_s[pl.ds(t0, CHUNK), :]
        gt = gates_ref[pl.ds(t0, CHUNK), :]
        beta = _sigmoid(gt)
        gl = jnp.where(g_lane, coef * _softplus(gt + dtb), 0.0)
        tri2 = ((col2 <= row2) if d == 0 else (col2 >= row2)).astype(BF16)
        gcum = _sel2_dot(tri2, gl)
        total = gcum[CHUNK - 1:CHUNK] if d == 0 else gcum[0:1]
        e_in = jnp.exp(gcum)
        e_out = jnp.exp(total - gcum)
        g_last = jnp.broadcast_to(jnp.exp(total), (8, LANE))
        expb = expb_ref[d]
        expg = expg_ref[d]
        beta_w = jnp.dot(beta.astype(BF16), expb, preferred_element_type=F32)
        ein_w = jnp.dot(e_in.astype(BF16), expg, preferred_element_type=F32)
        eout_w = jnp.dot(e_out.astype(BF16), expg, preferred_element_type=F32)
        kb = kc * beta_w
        return dict(
            t0=t0, q=qc, k=kc, kb=kb, vb=vc * beta_w, kbe=kb * ein_w, qe=qc * ein_w, kd=kc * eout_w,
            glast=_dot_sel(g_last, expg),
            gcum=gcum, gcum_t=gcum.T,
            incl=(col <= row) if d == 0 else (col >= row),
            strict=(col < row) if d == 0 else (col > row))

    unroll = 2

    def scan_group(i, carry):
        wd = [[wide(unroll * i + u, 0), wide(nc - 1 - (unroll * i + u), 1)] for u in range(unroll)]
        dh = [(d, h) for d in (0, 1) for h in range(A_HEADS)]
        ks = [(u, d, h) for u in range(unroll) for d, h in dh]
        hs = lambda h: slice(h * A_HEAD_DIM, (h + 1) * A_HEAD_DIM)
        dec, n, qk, x = {}, {}, {}, {}
        for u, d, h in ks:
            w = wd[u][d]
            lg = 2 * A_HEADS + A_HEADS * d + h
            dec[u, d, h] = jnp.exp(jnp.minimum(w['gcum'][:, lg:lg + 1] - w['gcum_t'][lg:lg + 1, :], 0.0))
        for u, d, h in ks:
            w = wd[u][d]
            n[u, d, h] = jnp.where(w['strict'], _dot_nt(w['kb'][:, hs(h)], w['k'][:, hs(h)]) * dec[u, d, h], 0.0)
        for u, d, h in ks:
            w = wd[u][d]
            qk[u, d, h] = jnp.where(w['incl'], _dot_nt(w['q'][:, hs(h)], w['k'][:, hs(h)]) * dec[u, d, h], 0.0)
            x[u, d, h] = jnp.concatenate([w['vb'][:, hs(h)], w['kbe'][:, hs(h)]], axis=1)
        xw = 2 * A_HEAD_DIM
        z = {k: _dot(n[k], jnp.concatenate([x[k], n[k]], axis=1)) for k in ks}
        x = {k: x[k] - z[k][:, :xw] for k in ks}
        for _ in range(4):
            p = {k: z[k][:, xw:] for k in ks}
            z = {k: _dot(p[k], jnp.concatenate([x[k], p[k]], axis=1)) for k in ks}
            x = {k: x[k] + z[k][:, :xw] for k in ks}
        x = {k: x[k] + _dot(z[k][:, xw:], x[k]) for k in ks}
        s = {(d, h): s_s[d * A_HEADS + h] for d, h in dh}
        for u in range(unroll):
            v_new = {(d, h): x[u, d, h][:, :A_HEAD_DIM] - _dot(x[u, d, h][:, A_HEAD_DIM:], s[d, h]) for d, h in dh}
            o = {(d, h): _dot(wd[u][d]['qe'][:, hs(h)], s[d, h]) + _dot(qk[u, d, h], v_new[d, h]) for d, h in dh}
            s = {(d, h): (s[d, h] * wd[u][d]['glast'][0:1, hs(h)] + _dot_tn(wd[u][d]['kd'][:, hs(h)], v_new[d, h]))
                 for d, h in dh}
            of_s[pl.ds(wd[u][0]['t0'], CHUNK), :] = jnp.concatenate([o[0, h] for h in range(A_HEADS)], axis=1)
            ob_s[pl.ds(wd[u][1]['t0'], CHUNK), :] = jnp.concatenate([o[1, h] for h in range(A_HEADS)], axis=1)
        for d, h in dh:
            s_s[d * A_HEADS + h] = s[d, h]
        return carry

    s_s[...] = jnp.zeros_like(s_s)
    lax.fori_loop(0, nc // unroll, scan_group, 0)

    rows = 256

    def finish(i, carry):
        t0 = pl.multiple_of(i * rows, rows)
        o = of_s[pl.ds(t0, rows), :] + ob_s[pl.ds(t0, rows), :]
        ms = _dot_sel(o * o, ind) * (1.0 / A_HEAD_DIM)
        o = o * lax.rsqrt(ms + EPS) * anorm_ref[...]
        out_ref[pl.ds(t0, rows), :] = o * _silu(gate_ref[pl.ds(t0, rows), :])
        return carry

    lax.fori_loop(0, L // rows, finish, 0)


def _gdn(a_in, convw, alog, dtb, anorm, ind, expb, expg):
    B, L, _ = a_in.shape
    W = A_WIDTH
    kern = functools.partial(_gdn_kernel, L=L)
    one = None
    return pl.pallas_call(
        kern,
        grid=(B,),
        in_specs=[pl.BlockSpec((None, L, 3 * W), lambda b: (b, 0, 0), pipeline_mode=one),
                  pl.BlockSpec((None, L, W), lambda b: (b, 0, 3), pipeline_mode=one),
                  pl.BlockSpec((None, L, LANE), lambda b: (b, 0, 4 * W // LANE), pipeline_mode=one),
                  _full((8, 3 * W)), _full((1, LANE)), _full((1, LANE)), _full((1, W)), _full((W, W)),
                  _full((2, LANE, W)), _full((2, LANE, W))],
        out_specs=pl.BlockSpec((None, L, W), lambda b: (b, 0, 0)),
        out_shape=jax.ShapeDtypeStruct((B, L, W), F32),
        scratch_shapes=[pltpu.VMEM((L, W), BF16), pltpu.VMEM((L, W), BF16), pltpu.VMEM((L, W), F32),
                        pltpu.VMEM((L, W), F32), pltpu.VMEM((L, W), F32),
                        pltpu.VMEM((2 * A_HEADS, A_HEAD_DIM, A_HEAD_DIM), F32)],
        compiler_params=_cparams(("parallel",)),
        name="gdn",
    )(a_in, a_in, a_in, convw, alog, dtb, anorm, ind, expb, expg)


def _mla_prep(b, tab_ref, qan_ref, kvan_ref, wqb_ref, wk_ref, wv_ref, qw_ref, kw_ref, knr_ref,
              indq_ref, invn_ref, place_ref, q_out, k_out, v_out):
    tab = tab_ref[...]
    cq = _rms(b[:, :B_Q_RANK], qan_ref[...])
    ckv = _rms(b[:, B_Q_RANK:B_Q_RANK + B_KV_RANK], kvan_ref[...])
    kr = b[:, B_Q_RANK + B_KV_RANK:]
    ind2 = indq_ref[...]
    invn = invn_ref[...]

    def seg_sums(x2):
        hi, lo = _split(x2)
        hp = lambda t, h: t[:, h * HEAD_PAD:(h + 1) * HEAD_PAD]
        return jnp.concatenate(
            [jnp.dot(jnp.concatenate([hp(hi, h), hp(lo, h)], axis=1), ind2, preferred_element_type=F32)
             for h in range(B_HEADS)], axis=1)

    q = _dot(cq, wqb_ref[...])
    qn = q * lax.rsqrt(seg_sums(q * q) * invn + EPS) * qw_ref[...]
    scale = (B_NOPE + B_ROPE) ** -0.5
    pieces = []
    for h in range(B_HEADS):
        x = qn[:, h * HEAD_PAD:(h + 1) * HEAD_PAD]
        half = B_ROPE // 2
        r = (x * tab[:, 0:LANE] + pltpu.roll(x, LANE - half, 1) * tab[:, LANE:2 * LANE]
             + pltpu.roll(x, half, 1) * tab[:, 2 * LANE:3 * LANE])
        pieces.append(r * scale)
    q_out[...] = jnp.concatenate(pieces, axis=1).astype(BF16)

    kk = _dot(ckv, wk_ref[...])
    kn = kk * lax.rsqrt(seg_sums(kk * kk) * invn + EPS) * kw_ref[...]
    krn = kr * lax.rsqrt(jnp.sum(kr * kr, axis=-1, keepdims=True) * (1.0 / B_ROPE) + EPS) * knr_ref[...]
    half = B_ROPE // 2
    krr = (krn * tab[:, 3 * LANE:4 * LANE] + pltpu.roll(krn, LANE - half, 1) * tab[:, 4 * LANE:5 * LANE]
           + pltpu.roll(krn, half, 1) * tab[:, 5 * LANE:6 * LANE])
    k_out[...] = (kn + jnp.dot(krr.astype(BF16), place_ref[...], preferred_element_type=F32)).astype(BF16)
    v_out[...] = _dot(ckv, wv_ref[...]).astype(BF16)


def _attn_kernel(q_ref, k_ref, v_ref, o_ref, v_aug):
    @pl.when(pl.program_id(2) == 0)
    def _():
        lane = lax.broadcasted_iota(jnp.int32, (v_aug.shape[0], LANE), 1)
        v_aug[:, 0:2 * B_V] = v_ref[...]
        v_aug[:, 2 * B_V:] = jnp.where(lane == 0, 1.0, 0.0).astype(BF16)

    scores = []
    for hh in range(2):
        q = q_ref[:, hh * HEAD_PAD:(hh + 1) * HEAD_PAD]
        k = k_ref[:, hh * HEAD_PAD:(hh + 1) * HEAD_PAD]
        scores.append(lax.dot_general(q, k, (((1,), (1,)), ((), ())), preferred_element_type=F32))
    outs = []
    for s in scores:
        p = jnp.exp((s - jnp.max(s, axis=-1, keepdims=True)).astype(BF16))
        pv = jnp.dot(p, v_aug[...], preferred_element_type=F32)
        outs.append(pv[:, 0:2 * B_V] / pv[:, 2 * B_V:2 * B_V + 1])
    lane = lax.broadcasted_iota(jnp.int32, outs[0].shape, 1)
    o_ref[...] = jnp.where(lane < B_V, outs[0], outs[1])


def _attn(q, k, v):
    B, L, _ = q.shape
    tq = min(1024, L)
    return pl.pallas_call(
        _attn_kernel,
        grid=(B, B_HEADS // 2, L // tq),
        in_specs=[pl.BlockSpec((None, tq, 2 * HEAD_PAD), lambda b, p, j: (b, j, p)),
                  pl.BlockSpec((None, L, 2 * HEAD_PAD), lambda b, p, j: (b, 0, p)),
                  pl.BlockSpec((None, L, 2 * B_V), lambda b, p, j: (b, 0, p))],
        out_specs=pl.BlockSpec((None, tq, 2 * B_V), lambda b, p, j: (b, j, p)),
        out_shape=jax.ShapeDtypeStruct((B, L, B_WIDTH), F32),
        scratch_shapes=[pltpu.VMEM((L, 2 * B_V + LANE), BF16)],
        compiler_params=_cparams(("parallel", "parallel", "arbitrary")),
        name="mla_attn",
    )(q, k, v)


def _hgrn_tables():
    C = CHUNK
    m_all = np.zeros((8 * C, C), np.float32)
    masks = np.zeros((7, C, C), np.float32)
    for lvl in range(6):
        m = 1 << lvl
        for t in range(C):
            bd = (t // (2 * m)) * 2 * m + m - 1
            if t % (2 * m) >= m:
                m_all[lvl * C + t, bd + 1:t + 1] = 1.0
            else:
                m_all[lvl * C + t, t + 1:bd + 1] = 1.0
            for s in range(C):
                if t // (2 * m) == s // (2 * m) and t % (2 * m) >= m and s % (2 * m) < m:
                    masks[lvl, t, s] = 1.0
    for t in range(C):
        m_all[6 * C + t, :t + 1] = 1.0
        m_all[7 * C + t, t + 1:] = 1.0
    masks[6] = np.eye(C, dtype=np.float32)
    m_f = m_all.reshape(8, C, C)
    m_b = m_f[:, ::-1, ::-1]
    mk_b = masks[:, ::-1, ::-1]
    m2 = np.stack([m_f.reshape(8 * C, C), m_b.reshape(8 * C, C)])
    return (np.concatenate([m2, m2], axis=2), np.stack([masks, mk_b]))


def _hgrn_kernel(q_ref, lff_ref, lfb_ref, kf_ref, kb_ref, i_ref, gate_ref, cnorm_ref, mall_ref, masks_ref, ind_ref,
                 out_ref, of_s, ob_s, s_s, *, L):
    nc = L // CHUNK
    C = CHUNK

    def wide(c, d):
        t0 = pl.multiple_of(c * C, C)
        logf = (lff_ref if d == 0 else lfb_ref)[pl.ds(t0, C), :]
        e_all = jnp.exp(_sel2_dot(mall_ref[d], logf))
        return dict(t0=t0, q=q_ref[pl.ds(t0, C), :], k=(kf_ref if d == 0 else kb_ref)[pl.ds(t0, C), :],
                    e=e_all, v=i_ref[pl.ds(t0, C), :])

    unroll = 2

    def scan_group(i, carry):
        wd = [[wide(unroll * i + u, 0), wide(nc - 1 - (unroll * i + u), 1)] for u in range(unroll)]
        dh = [(d, h) for d in (0, 1) for h in range(C_HEADS)]
        ks = [(u, d, h) for u in range(unroll) for d, h in dh]
        ls = lambda h: slice(h * C_EXPAND, (h + 1) * C_EXPAND)
        vs = lambda h: slice(h * C_HEAD_DIM, (h + 1) * C_HEAD_DIM)
        qb = {(u, d, h): wd[u][d]['q'][:, ls(h)] for u, d, h in ks}
        kb = {(u, d, h): wd[u][d]['k'][:, ls(h)] for u, d, h in ks}
        vh = {(u, d, h): wd[u][d]['v'][:, vs(h)] for u, d, h in ks}
        lvl_e = lambda u, d, h, j: wd[u][d]['e'][j * C:(j + 1) * C, ls(h)]
        attn = {(u, d, h): _dot_nt(qb[u, d, h], kb[u, d, h]) * masks_ref[d, 6] for u, d, h in ks}
        for lvl in range(6):
            for u, d, h in ks:
                e = lvl_e(u, d, h, lvl).astype(BF16)
                attn[u, d, h] = attn[u, d, h] + _dot_nt(qb[u, d, h] * e, kb[u, d, h] * e) * masks_ref[d, lvl]
        intra = {k: _dot(attn[k], vh[k]) for k in ks}
        qin = {(u, d, h): qb[u, d, h] * lvl_e(u, d, h, 6).astype(BF16) for u, d, h in ks}
        kout = {(u, d, h): _dot_tn(vh[u, d, h], kb[u, d, h] * lvl_e(u, d, h, 7).astype(BF16)) for u, d, h in ks}
        st = {(d, h): s_s[d * C_HEADS + h] for d, h in dh}
        for u in range(unroll):
            o = {(d, h): intra[u, d, h] + _dot_nt(qin[u, d, h], st[d, h]) for d, h in dh}
            for d, h in dh:
                e_in = lvl_e(u, d, h, 6)
                st[d, h] = st[d, h] * (e_in[C - 1:C] if d == 0 else e_in[0:1]) + kout[u, d, h]
            of_s[pl.ds(wd[u][0]['t0'], C), :] = jnp.concatenate([o[0, h] for h in range(C_HEADS)], axis=1)
            ob_s[pl.ds(wd[u][1]['t0'], C), :] = jnp.concatenate([o[1, h] for h in range(C_HEADS)], axis=1)
        for d, h in dh:
            s_s[d * C_HEADS + h] = st[d, h]
        return carry

    s_s[...] = jnp.zeros_like(s_s)
    lax.fori_loop(0, nc // unroll, scan_group, 0)

    rows = 256
    ind = ind_ref[...]

    def finish(i, carry):
        t0 = pl.multiple_of(i * rows, rows)
        o = of_s[pl.ds(t0, rows), :] + ob_s[pl.ds(t0, rows), :]
        ms = _dot_sel(o * o, ind) * (1.0 / C_HEAD_DIM)
        o = o * lax.rsqrt(ms + EPS) * cnorm_ref[...]
        out_ref[pl.ds(t0, rows), :] = o * _silu(gate_ref[pl.ds(t0, rows), :])
        return carry

    lax.fori_loop(0, L // rows, finish, 0)


def _hgrn(q, logf, kk, v, gate, cnorm, mall, masks, ind):
    B, L, _ = q.shape
    kern = functools.partial(_hgrn_kernel, L=L)
    one = None
    F, W = C_FDIM, C_WIDTH
    seq = lambda w, j: pl.BlockSpec((None, L, w), lambda b: (b, 0, j), pipeline_mode=one)
    return pl.pallas_call(
        kern,
        grid=(B,),
        in_specs=[seq(F, 0), seq(F, 0), seq(F, 1), seq(F, 0), seq(F, 1), seq(W, 0), seq(W, 0),
                  _full((1, W)), _full((2, 8 * CHUNK, 2 * CHUNK)), _full((2, 7, CHUNK, CHUNK)), _full((W, W))],
        out_specs=pl.BlockSpec((None, L, W), lambda b: (b, 0, 0)),
        out_shape=jax.ShapeDtypeStruct((B, L, W), F32),
        scratch_shapes=[pltpu.VMEM((L, W), F32), pltpu.VMEM((L, W), F32),
                        pltpu.VMEM((2 * C_HEADS, C_HEAD_DIM, C_EXPAND), F32)],
        compiler_params=_cparams(("parallel",)),
        name="hgrn2",
    )(q, logf, logf, kk, kk, v, gate, cnorm, mall, masks, ind)


def _outproj_kernel(xa_ref, xb_ref, oa_ref, ob_ref, oc_ref, wa_ref, wb_ref, wc_ref, ln_ref, wr_ref, br_ref, tri_ref,
                    x1_ref, h_ref, route_ref, cnt_ref, cnt_s, *, n_first):
    x1 = (_pair_load(xa_ref, xb_ref, n_first) + _dot(oa_ref[...], wa_ref[...]) + _dot(ob_ref[...], wb_ref[...])
          + _dot(oc_ref[...], wc_ref[...]))
    x1_ref[...] = x1
    h = _rms(x1, ln_ref[...])
    h_ref[...] = h
    hh, hl = _split(h)
    wr = wr_ref[...]
    wh, wl = _split(wr)
    hw = jnp.dot(hh, jnp.concatenate([wh, wl], axis=1), preferred_element_type=F32)
    lg = hw[:, :LANE] + hw[:, LANE:] + jnp.dot(hl, wh, preferred_element_type=F32) + br_ref[...]
    lane = lax.broadcasted_iota(jnp.int32, lg.shape, 1).astype(F32)
    neg = -1e30
    far = 1e9
    gmask = lane < N_GROUPS
    gl = jnp.where(gmask, lg, neg)
    gmax = jnp.max(gl, axis=-1, keepdims=True)
    g_idx = jnp.min(jnp.where(gl == gmax, lane, far), axis=-1, keepdims=True)
    g_w = 1.0 / jnp.sum(jnp.where(gmask, jnp.exp(gl - gmax), 0.0), axis=-1, keepdims=True)
    e_lane = lane - N_GROUPS
    emask = (e_lane >= g_idx * EXPERTS_PER_GROUP) & (e_lane < (g_idx + 1.0) * EXPERTS_PER_GROUP)
    el = jnp.where(emask, lg, neg)
    m1 = jnp.max(el, axis=-1, keepdims=True)
    i1 = jnp.min(jnp.where(el == m1, lane, far), axis=-1, keepdims=True)
    el2 = jnp.where(lane == i1, neg, el)
    m2 = jnp.max(el2, axis=-1, keepdims=True)
    i2 = jnp.min(jnp.where((el2 == m2) & emask & (lane != i1), lane, far), axis=-1, keepdims=True)
    r = jnp.exp(m2 - m1)
    w1 = g_w / (1.0 + r)
    w2 = g_w * r / (1.0 + r)
    @pl.when(pl.program_id(0) == 0)
    def _():
        cnt_s[...] = jnp.zeros_like(cnt_s)

    oh1 = jnp.where(lane == i1, 1.0, 0.0)
    oh2 = jnp.where(lane == i2, 1.0, 0.0)
    tri = tri_ref[...]
    run = cnt_s[0:1, :]
    tot1 = jnp.sum(oh1, axis=0, keepdims=True)
    c1 = run + jnp.dot(tri, oh1.astype(BF16), preferred_element_type=F32)
    c2 = run + tot1 + jnp.dot(tri, oh2.astype(BF16), preferred_element_type=F32)
    r1 = jnp.sum(oh1 * c1, axis=-1, keepdims=True)
    r2 = jnp.sum(oh2 * c2, axis=-1, keepdims=True)
    cnt = jnp.broadcast_to(run + tot1 + jnp.sum(oh2, axis=0, keepdims=True), cnt_s.shape)
    cnt_s[...] = cnt
    cnt_ref[...] = cnt
    vals = (i1 - N_GROUPS, i2 - N_GROUPS, w1, w2, r1, r2)
    route = jnp.zeros_like(lg)
    for j, v in enumerate(vals):
        route = jnp.where(lane == float(j), v, route)
    route_ref[...] = route


def _outproj(xa, xb, oa, ob, oc, wa, wb, wc, ln, wr, br, tri):
    tm = TOK_TILE
    n_first = xa.shape[0] // tm
    T = xa.shape[0] + xb.shape[0]
    row = lambda w: pl.BlockSpec((tm, w), lambda i: (i, 0))
    return pl.pallas_call(
        functools.partial(_outproj_kernel, n_first=n_first),
        grid=(T // tm,),
        in_specs=_pair_specs(n_first, tm) + [row(A_WIDTH), row(B_WIDTH), row(C_WIDTH),
                  _full((A_WIDTH, D_MODEL)), _full((B_WIDTH, D_MODEL)), _full((C_WIDTH, D_MODEL)),
                  _full((1, D_MODEL)), _full((D_MODEL, LANE)), _full((1, LANE)), _full((tm, tm))],
        out_specs=[row(D_MODEL), row(D_MODEL), row(LANE), _full((8, LANE))],
        out_shape=[jax.ShapeDtypeStruct((T, D_MODEL), F32), jax.ShapeDtypeStruct((T, D_MODEL), F32),
                   jax.ShapeDtypeStruct((T, LANE), F32), jax.ShapeDtypeStruct((8, LANE), F32)],
        scratch_shapes=[pltpu.VMEM((8, LANE), F32)],
        compiler_params=_cparams(("arbitrary",)),
        name="outproj_router",
    )(xa, xb, oa, ob, oc, wa, wb, wc, ln, wr, br, tri)


def _gather_rows(src_hbm, buf, sem, cur_ids, nxt_ids, n_rows):
    i = pl.program_id(0)
    slot = i % 2

    def row_copy(ids, r, s):
        return pltpu.make_async_copy(src_hbm.at[pl.ds(ids[0, 0, r], 1)], buf.at[s, pl.ds(r, 1)], sem.at[s])

    def start_all(ids, s):
        def body(r, carry):
            row_copy(ids, r, s).start()
            return carry
        lax.fori_loop(0, n_rows, body, 0, unroll=8)

    def wait_all(s):
        pltpu.make_async_copy(src_hbm.at[pl.ds(0, n_rows)], buf.at[s], sem.at[s]).wait()

    @pl.when(i == 0)
    def _():
        start_all(cur_ids, 0)

    start_all(nxt_ids, 1 - slot)
    wait_all(slot)

    @pl.when(i == pl.num_programs(0) - 1)
    def _():
        wait_all(1 - slot)

    return slot


def _scatter_kernel(pends_ref, pos_ref, h_ref, xs_hbm, zero_s, sem):
    tm = TOK_TILE

    @pl.when(pl.program_id(0) == 0)
    def _():
        zero_s[...] = jnp.zeros_like(zero_s)

        def fill(e):
            start = pl.multiple_of(jnp.maximum(pends_ref[e] - MOE_ROWS, 0), MOE_ROWS)
            return pltpu.make_async_copy(zero_s, xs_hbm.at[pl.ds(start, MOE_ROWS)], sem)

        for e in range(N_EXPERTS):
            fill(e).start()
            fill(e).wait()

        def fill_tail(b, carry):
            cp = pltpu.make_async_copy(zero_s, xs_hbm.at[pl.ds(pl.multiple_of(b * MOE_ROWS, MOE_ROWS), MOE_ROWS)], sem)
            cp.start()
            cp.wait()
            return carry

        lax.fori_loop(pends_ref[N_EXPERTS - 1] // MOE_ROWS, xs_hbm.shape[0] // MOE_ROWS, fill_tail, 0)

    def row_copy(r, k):
        return pltpu.make_async_copy(h_ref.at[pl.ds(r, 1)], xs_hbm.at[pl.ds(pos_ref[0, 0, k * tm + r], 1)], sem)

    def issue(r, carry):
        for k in range(TOP_K):
            row_copy(r, k).start()
        return carry

    lax.fori_loop(0, tm, issue, 0, unroll=8)
    for k in range(TOP_K):
        pltpu.make_async_copy(h_ref, xs_hbm.at[pl.ds(0, tm)], sem).wait()


def _scatter(pends, pos, h):
    T = h.shape[0]
    tm = TOK_TILE
    grid_spec = pltpu.PrefetchScalarGridSpec(
        num_scalar_prefetch=1,
        grid=(T // tm,),
        in_specs=[pl.BlockSpec((1, 1, TOP_K * tm), lambda i, pe: (i, 0, 0), memory_space=pltpu.SMEM),
                  pl.BlockSpec((tm, D_MODEL), lambda i, pe: (i, 0))],
        out_specs=pl.BlockSpec(memory_space=pl.ANY),
        scratch_shapes=[pltpu.VMEM((MOE_ROWS, D_MODEL), F32), pltpu.SemaphoreType.DMA(())],
    )
    return pl.pallas_call(
        _scatter_kernel,
        grid_spec=grid_spec,
        out_shape=jax.ShapeDtypeStruct((T * TOP_K + N_EXPERTS * MOE_ROWS, D_MODEL), F32),
        compiler_params=_cparams(("arbitrary",)),
        name="moe_scatter",
    )(pends, pos, h)


def _moe_kernel(blk_e_ref, nact_ref, x_ref, w1_ref, w3_ref, w2_ref, y_ref, w1_s, w3_s, w2_s):
    i = pl.program_id(0)
    active = i < nact_ref[0]

    @pl.when((i == 0) | (blk_e_ref[i] != blk_e_ref[jnp.maximum(i - 1, 0)]))
    def _():
        w1_s[...] = w1_ref[...].astype(BF16)
        w3_s[...] = w3_ref[...].astype(BF16)
        w2_s[...] = w2_ref[...].astype(BF16)

    @pl.when(active)
    def _():
        x = x_ref[...].astype(BF16)
        a = jnp.dot(x, w1_s[...], preferred_element_type=F32)
        g = jnp.dot(x, w3_s[...], preferred_element_type=F32)
        y_ref[...] = jnp.dot((_silu(a) * g).astype(BF16), w2_s[...], preferred_element_type=F32)

    @pl.when(jnp.logical_not(active))
    def _():
        y_ref[...] = jnp.zeros_like(y_ref)


def _moe(blk_e, nact, xs, w1, w3, w2, layer):
    P = xs.shape[0]
    nblk = P // MOE_ROWS
    wspec = lambda a, b: pl.BlockSpec((None, None, a, b), lambda i, be, na: (layer, be[i], 0, 0))
    grid_spec = pltpu.PrefetchScalarGridSpec(
        num_scalar_prefetch=2,
        grid=(nblk,),
        in_specs=[pl.BlockSpec((MOE_ROWS, D_MODEL), lambda i, be, na: (jnp.minimum(i, na[0] - 1), 0)),
                  wspec(D_MODEL, D_EXPERT), wspec(D_MODEL, D_EXPERT), wspec(D_EXPERT, D_MODEL)],
        out_specs=pl.BlockSpec((MOE_ROWS, D_MODEL), lambda i, be, na: (i, 0)),
        scratch_shapes=[pltpu.VMEM((D_MODEL, D_EXPERT), BF16), pltpu.VMEM((D_MODEL, D_EXPERT), BF16),
                        pltpu.VMEM((D_EXPERT, D_MODEL), BF16)],
    )
    return pl.pallas_call(
        _moe_kernel,
        grid_spec=grid_spec,
        out_shape=jax.ShapeDtypeStruct((P, D_MODEL), F32),
        compiler_params=_cparams(("arbitrary",)),
        name="moe_ffn",
    )(blk_e, nact, xs, w1, w3, w2)


def _combine_kernel(pos_ref, nxt_ref, x_ref, route_ref, y_hbm, outa_ref, outb_ref, yb, sem, *, n_first):
    tm = x_ref.shape[0]
    slot = _gather_rows(y_hbm, yb, sem, pos_ref, nxt_ref, TOP_K * tm)
    route = route_ref[...]
    val = x_ref[...] + yb[slot, 0:tm, :] * route[:, 2:3] + yb[slot, tm:2 * tm, :] * route[:, 3:4]
    first = pl.program_id(0) < n_first

    @pl.when(first)
    def _():
        outa_ref[...] = val

    @pl.when(jnp.logical_not(first))
    def _():
        outb_ref[...] = val


def _combine(pos, x1, route, y, n_first):
    T = x1.shape[0]
    tm = TOK_TILE
    nt = T // tm
    return pl.pallas_call(
        functools.partial(_combine_kernel, n_first=n_first),
        grid=(nt,),
        in_specs=[pl.BlockSpec((1, 1, TOP_K * tm), lambda i: (i, 0, 0), memory_space=pltpu.SMEM),
                  pl.BlockSpec((1, 1, TOP_K * tm), lambda i: (jnp.minimum(i + 1, nt - 1), 0, 0),
                               memory_space=pltpu.SMEM),
                  pl.BlockSpec((tm, D_MODEL), lambda i: (i, 0)),
                  pl.BlockSpec((tm, LANE), lambda i: (i, 0)),
                  pl.BlockSpec(memory_space=pl.ANY)],
        out_specs=_pair_specs(n_first, tm),
        out_shape=[jax.ShapeDtypeStruct((n_first * tm, D_MODEL), F32),
                   jax.ShapeDtypeStruct((T - n_first * tm, D_MODEL), F32)],
        scratch_shapes=[pltpu.VMEM((2, TOP_K * tm, D_MODEL), F32), pltpu.SemaphoreType.DMA((2,))],
        compiler_params=_cparams(("arbitrary",)),
        name="moe_combine",
    )(pos, pos, x1, route, y)


def _dispatch(route, cnt):
    T = route.shape[0]
    nblk = (T * TOP_K) // MOE_ROWS + N_EXPERTS
    counts = cnt[0, N_GROUPS:N_GROUPS + N_EXPERTS].astype(jnp.int32)
    padded = (counts + MOE_ROWS - 1) // MOE_ROWS * MOE_ROWS
    pends = jnp.cumsum(padded)
    pstarts = pends - padded
    e = route[:, 0:TOP_K].astype(jnp.int32)
    rank = route[:, 4:4 + TOP_K].astype(jnp.int32)
    sel = e[:, :, None] == jnp.arange(N_EXPERTS, dtype=jnp.int32)
    pos = rank + jnp.sum(jnp.where(sel, pstarts, 0), axis=-1)
    pos = jnp.transpose(pos.reshape(T // TOK_TILE, TOK_TILE, TOP_K), (0, 2, 1))
    pos = pos.reshape(T // TOK_TILE, 1, TOP_K * TOK_TILE)
    blk_start = jnp.arange(nblk, dtype=jnp.int32) * MOE_ROWS
    blk_e = jnp.minimum(jnp.sum(blk_start[:, None] >= pends[None, :], axis=-1), N_EXPERTS - 1).astype(jnp.int32)
    nact = (pends[-1:] // MOE_ROWS).astype(jnp.int32)
    return blk_e, nact, pos, pends.astype(jnp.int32)


def _block_ind(width, seg):
    i = np.arange(width)
    return (i[:, None] // seg == i[None, :] // seg).astype(np.float32)


def _rope_tables(L):
    half = B_ROPE // 2
    inv = ROPE_THETA ** (-jnp.arange(0, B_ROPE, 2, dtype=F32) / B_ROPE)
    ang = jnp.arange(L, dtype=F32)[:, None] * inv[None, :]
    cos, sin = jnp.cos(ang), jnp.sin(ang)
    z = lambda n: jnp.zeros((L, n), F32)
    one = jnp.ones((L, B_NOPE), F32)
    rest = LANE - B_NOPE - B_ROPE
    cq = jnp.concatenate([one, cos, cos, z(rest)], axis=1)
    s1q = jnp.concatenate([z(B_NOPE), -sin, z(half), z(rest)], axis=1)
    s2q = jnp.concatenate([z(B_NOPE), z(half), sin, z(rest)], axis=1)
    ck = jnp.concatenate([cos, cos, z(LANE - B_ROPE)], axis=1)
    s1k = jnp.concatenate([-sin, z(LANE - half)], axis=1)
    s2k = jnp.concatenate([z(half), sin, z(LANE - B_ROPE)], axis=1)
    return jnp.concatenate([cq, s1q, s2q, ck, s1k, s2k], axis=1)


def _layer_params(l, ln1, w_in, conv_w, a_log, dt_bias, a_norm, q_a_norm, w_qb, kv_a_norm, w_kvb, qn_nope, qn_rope,
                  kn_nope, kn_rope, lbs, c_norm, w_out, ln2, w_group, b_group, w_router, b_router, w1, w3, w2):
    p = {}
    w = w_in[l]
    wp = jnp.zeros((D_MODEL, N_PAD), F32)
    wp = wp.at[:, 0:A_IN].set(w[:, 0:A_IN])
    wp = wp.at[:, A_PAD:A_PAD + B_IN].set(w[:, A_IN:A_IN + B_IN])
    wp = wp.at[:, A_PAD + B_PAD:].set(w[:, A_IN + B_IN:])
    p['w_in'] = wp.astype(BF16)
    p['ln1'] = ln1[l].reshape(1, D_MODEL)
    p['convw'] = jnp.zeros((8, 3 * A_WIDTH), F32).at[0:CONV_K].set(conv_w[l])
    gpad = lambda v: jnp.zeros((1, LANE), F32).at[0, 2 * A_HEADS:4 * A_HEADS].set(v.reshape(-1))
    p['alog'] = gpad(a_log[l])
    p['dtb'] = gpad(dt_bias[l])
    p['anorm'] = jnp.tile(a_norm[l], A_HEADS).reshape(1, A_WIDTH)
    half = B_ROPE // 2
    wq = w_qb[l].reshape(B_Q_RANK, B_HEADS, B_NOPE + B_ROPE)
    wq = jnp.pad(wq, ((0, 0), (0, 0), (0, HEAD_PAD - B_NOPE - B_ROPE)))
    p['wqb'] = wq.reshape(B_Q_RANK, B_HEADS * HEAD_PAD).astype(BF16)
    wkv = w_kvb[l].reshape(B_KV_RANK, B_HEADS, B_NOPE + B_V)
    wk = jnp.pad(wkv[:, :, :B_NOPE], ((0, 0), (0, 0), (0, HEAD_PAD - B_NOPE)))
    p['wk'] = wk.reshape(B_KV_RANK, B_HEADS * HEAD_PAD).astype(BF16)
    p['wv'] = wkv[:, :, B_NOPE:].reshape(B_KV_RANK, B_WIDTH).astype(BF16)
    zpad = jnp.zeros((HEAD_PAD - B_NOPE - B_ROPE,), F32)
    p['qw'] = jnp.tile(jnp.concatenate([qn_nope[l], qn_rope[l], zpad]), B_HEADS).reshape(1, -1)
    p['kw'] = jnp.tile(jnp.concatenate([kn_nope[l], jnp.zeros((HEAD_PAD - B_NOPE,), F32)]), B_HEADS).reshape(1, -1)
    p['knr'] = jnp.concatenate([kn_rope[l], jnp.zeros((LANE - B_ROPE,), F32)]).reshape(1, LANE)
    p['qan'] = q_a_norm[l].reshape(1, -1)
    p['kvan'] = kv_a_norm[l].reshape(1, -1)
    p['lb'] = lbs[l]
    p['cnorm'] = jnp.tile(c_norm[l], C_HEADS).reshape(1, C_WIDTH)
    wo = w_out[l].astype(BF16)
    p['wa'], p['wb'], p['wc'] = wo[:A_WIDTH], wo[A_WIDTH:A_WIDTH + B_WIDTH], wo[A_WIDTH + B_WIDTH:]
    p['ln2'] = ln2[l].reshape(1, D_MODEL)
    wr = jnp.zeros((D_MODEL, LANE), F32).at[:, :N_GROUPS].set(w_group[l])
    p['wr'] = wr.at[:, N_GROUPS:N_GROUPS + N_EXPERTS].set(w_router[l])
    br = jnp.zeros((1, LANE), F32).at[0, :N_GROUPS].set(b_group[l])
    p['br'] = br.at[0, N_GROUPS:N_GROUPS + N_EXPERTS].set(b_router[l])
    p['w1'], p['w3'], p['w2'], p['layer'] = w1, w3, w2, l
    return p


def _static_tables():
    t = {}
    t['ind_a'] = jnp.asarray(_block_ind(A_WIDTH, A_HEAD_DIM), BF16)
    t['ind_c'] = jnp.asarray(_block_ind(C_WIDTH, C_HEAD_DIM), BF16)
    expb = np.zeros((2, LANE, A_WIDTH), np.float32)
    expg = np.zeros((2, LANE, A_WIDTH), np.float32)
    for d in range(2):
        for h in range(A_HEADS):
            expb[d, A_HEADS * d + h, h * A_HEAD_DIM:(h + 1) * A_HEAD_DIM] = 1.0
            expg[d, 2 * A_HEADS + A_HEADS * d + h, h * A_HEAD_DIM:(h + 1) * A_HEAD_DIM] = 1.0
    t['expb'] = jnp.asarray(expb, BF16)
    t['expg'] = jnp.asarray(expg, BF16)
    HW = B_HEADS * HEAD_PAD
    lane = np.arange(HW)
    seg = np.where(lane % HEAD_PAD < B_NOPE, 0, np.where(lane % HEAD_PAD < B_NOPE + B_ROPE, 1, 2))
    head = lane // HEAD_PAD
    seg1 = seg[:HEAD_PAD]
    ind1 = ((seg1[:, None] == seg1[None, :]) & (seg1[:, None] < 2)).astype(np.float32)
    t['indq'] = jnp.asarray(np.concatenate([ind1, ind1], axis=0), BF16)
    t['invn'] = jnp.asarray(np.where(seg == 0, 1.0 / B_NOPE, 1.0 / B_ROPE).astype(np.float32)).reshape(1, HW)
    place = np.zeros((LANE, HW), np.float32)
    for h in range(B_HEADS):
        for r in range(B_ROPE):
            place[r, h * HEAD_PAD + B_NOPE + r] = 1.0
    t['place'] = jnp.asarray(place, BF16)
    t['tri'] = jnp.asarray(np.tril(np.ones((TOK_TILE, TOK_TILE), np.float32), -1), BF16)
    mall, masks = _hgrn_tables()
    t['mall'] = jnp.asarray(mall, BF16)
    t['masks'] = jnp.asarray(masks, F32)
    return t


def _trunk(x_first, x_second, params, tabs, rope):
    L, D = x_first.shape[1:]
    B = x_first.shape[0] + x_second.shape[0]
    T = B * L
    xa, xb = x_first.reshape(-1, D), x_second.reshape(-1, D)
    n_first = xa.shape[0] // TOK_TILE
    for p in params:
        a_in, q, k, v, c_q, c_logf, c_k, c_v, c_gate = _proj(
            xa, xb, p['ln1'], p['w_in'], p['lb'], L, rope, p['qan'], p['kvan'], p['wqb'], p['wk'], p['wv'],
            p['qw'], p['kw'], p['knr'], tabs['indq'], tabs['invn'], tabs['place'])
        seq = lambda t: t.reshape(B, L, t.shape[-1])
        o_a = _gdn(a_in.reshape(B, L, A_PAD), p['convw'], p['alog'], p['dtb'], p['anorm'], tabs['ind_a'],
                   tabs['expb'], tabs['expg'])
        o_b = _attn(seq(q), seq(k), seq(v))
        o_c = _hgrn(seq(c_q), seq(c_logf), seq(c_k), seq(c_v), seq(c_gate), p['cnorm'], tabs['mall'],
                    tabs['masks'], tabs['ind_c'])
        x1, h, route, cnt = _outproj(xa, xb, o_a.reshape(T, A_WIDTH), o_b.reshape(T, B_WIDTH),
                                     o_c.reshape(T, C_WIDTH), p['wa'], p['wb'], p['wc'], p['ln2'], p['wr'], p['br'],
                                     tabs['tri'])
        blk_e, nact, pos, pends = _dispatch(route, cnt)
        xs = _scatter(pends, pos, h)
        y = _moe(blk_e, nact, xs, p['w1'], p['w3'], p['w2'], p['layer'])
        xa, xb = _combine(pos, x1, route, y, n_first)
    return xa.reshape(x_first.shape), xb.reshape(x_second.shape)


def kernel(x_prompt, x_sample, ln1, w_in, conv_w, a_log, dt_bias, a_norm, q_a_norm, w_qb, kv_a_norm, w_kvb, qn_nope,
           qn_rope, kn_nope, kn_rope, c_lower_bounds, c_norm, w_out, ln2, w_group, b_group, w_router, b_router,
           w1, w3, w2):
    lbs = jnp.cumsum(jax.nn.softmax(c_lower_bounds.astype(F32), axis=0), axis=0)
    lbs = lbs - lbs[0:1]
    params = [_layer_params(l, ln1, w_in, conv_w, a_log, dt_bias, a_norm, q_a_norm, w_qb, kv_a_norm, w_kvb, qn_nope,
                            qn_rope, kn_nope, kn_rope, lbs, c_norm, w_out, ln2, w_group, b_group, w_router, b_router,
                            w1, w3, w2) for l in range(DEPTH)]
    tabs = _static_tables()
    assert x_prompt.shape[1:] == x_sample.shape[1:]
    rope = _rope_tables(x_prompt.shape[1])
    assert (x_prompt.shape[0] * x_prompt.shape[1]) % TOK_TILE == 0
    return _trunk(x_prompt, x_sample, params, tabs, rope)
```

```python
import functools
import math

import numpy as np
import jax
import jax.numpy as jnp
from jax import lax
from jax.experimental import pallas as pl
from jax.experimental.pallas import tpu as pltpu

F32 = jnp.float32
BF16 = jnp.bfloat16

D_MODEL = 1024
DEPTH = 2
EPS = 1e-6
A_HEADS, A_HEAD_DIM, CONV_K = 6, 64, 5
A_WIDTH = A_HEADS * A_HEAD_DIM
B_HEADS, B_Q_RANK, B_KV_RANK, B_NOPE, B_ROPE, B_V = 6, 256, 128, 64, 32, 64
B_WIDTH = B_HEADS * B_V
ROPE_THETA = 10000.0
C_HEADS, C_EXPAND, C_HEAD_DIM = 4, 128, 64
C_WIDTH = C_HEADS * C_HEAD_DIM
C_FDIM = C_HEADS * C_EXPAND
D_MIX = A_WIDTH + B_WIDTH + C_WIDTH
A_IN = 4 * A_WIDTH + 4 * A_HEADS
B_IN = B_Q_RANK + B_KV_RANK + B_ROPE
C_IN = 3 * C_FDIM + 2 * C_WIDTH
N_GROUPS, EXPERTS_PER_GROUP, TOP_K, D_EXPERT = 4, 8, 2, 512
N_EXPERTS = N_GROUPS * EXPERTS_PER_GROUP

LANE = 128
CHUNK = 64
A_PAD = 4 * A_WIDTH + LANE
B_PAD = 512
C_PAD = C_IN
N_PAD = A_PAD + B_PAD + C_PAD
HEAD_PAD = LANE
MOE_ROWS = 512
TOK_TILE = 512
VMEM_LIMIT = 56 * 1024 * 1024


def _cparams(sem):
    return pltpu.CompilerParams(dimension_semantics=sem, vmem_limit_bytes=VMEM_LIMIT)


def _dot(a, b):
    return jnp.dot(a.astype(BF16), b.astype(BF16), preferred_element_type=F32)


def _dot_nt(a, b):
    return lax.dot_general(a.astype(BF16), b.astype(BF16), (((1,), (1,)), ((), ())), preferred_element_type=F32)


def _dot_tn(a, b):
    return lax.dot_general(a.astype(BF16), b.astype(BF16), (((0,), (0,)), ((), ())), preferred_element_type=F32)


def _split(x):
    hi = x.astype(BF16)
    lo = (x - hi.astype(F32)).astype(BF16)
    return hi, lo


def _sel_dot(sel, x):
    hi, lo = _split(x)
    return jnp.dot(sel, hi, preferred_element_type=F32) + jnp.dot(sel, lo, preferred_element_type=F32)


def _sel2_dot(sel2, x):
    hi, lo = _split(x)
    return jnp.dot(sel2, jnp.concatenate([hi, lo], axis=0), preferred_element_type=F32)


def _dot_sel(x, sel):
    hi, lo = _split(x)
    return jnp.dot(hi, sel, preferred_element_type=F32) + jnp.dot(lo, sel, preferred_element_type=F32)


def _sigmoid(x):
    return 1.0 / (1.0 + jnp.exp(-x))


def _silu(x):
    return x * _sigmoid(x)


def _softplus(x):
    return jnp.maximum(x, 0.0) + jnp.log(1.0 + jnp.exp(-jnp.abs(x)))


def _rms(x, w):
    return x * lax.rsqrt(jnp.mean(x * x, axis=-1, keepdims=True) + EPS) * w


def _full(shape):
    return pl.BlockSpec(shape, lambda *_: (0,) * len(shape))


def _pair_specs(n_first, tm):
    return [pl.BlockSpec((tm, D_MODEL), lambda i, *_: (jnp.minimum(i, n_first - 1), 0)),
            pl.BlockSpec((tm, D_MODEL), lambda i, *_: (jnp.maximum(i - n_first, 0), 0))]


def _pair_load(xa_ref, xb_ref, n_first):
    return jnp.where(pl.program_id(0) < n_first, xa_ref[...], xb_ref[...])


def _proj_kernel(xa_ref, xb_ref, ln_ref, w_ref, lb_ref, *refs, n_first):
    mla_refs = refs[:12]
    a_ref, bq_ref, bk_ref, bv_ref, cq_ref, clf_ref, ck_ref, cv_ref, cg_ref = refs[12:]
    h = _rms(_pair_load(xa_ref, xb_ref, n_first), ln_ref[...]).astype(BF16)
    a_ref[...] = jnp.dot(h, w_ref[:, 0:A_PAD], preferred_element_type=F32)
    _mla_prep(jnp.dot(h, w_ref[:, A_PAD:A_PAD + B_PAD], preferred_element_type=F32), *mla_refs,
              bq_ref, bk_ref, bv_ref)
    c0 = A_PAD + B_PAD
    F = C_FDIM
    cq_ref[...] = _silu(jnp.dot(h, w_ref[:, c0:c0 + F], preferred_element_type=F32)).astype(BF16)
    for d in range(2):
        fp = jnp.dot(h, w_ref[:, c0 + (1 + d) * F:c0 + (2 + d) * F], preferred_element_type=F32)
        lb = lb_ref[d:d + 1, :]
        e = jnp.exp(-jnp.abs(fp))
        r = 1.0 / (1.0 + e)
        lsig = jnp.minimum(fp, 0.0) - jnp.log(1.0 + e)
        a = jnp.log(lb)
        bb = jnp.log(1.0 - lb) + lsig
        clf_ref[:, d * F:(d + 1) * F] = jnp.maximum(a, bb) + jnp.log(1.0 + jnp.exp(-jnp.abs(a - bb)))
        ck_ref[:, d * F:(d + 1) * F] = ((1.0 - lb) * jnp.where(fp >= 0.0, e, 1.0) * r).astype(BF16)
    tail = jnp.dot(h, w_ref[:, c0 + 3 * F:N_PAD], preferred_element_type=F32)
    cv_ref[...] = tail[:, :C_WIDTH].astype(BF16)
    cg_ref[...] = tail[:, C_WIDTH:]


def _proj(xa, xb, ln, w_pad, lb, seq_len, tab, qan, kvan, wqb, wk, wv, qw, kw, knr, indq, invn, place):
    tm = TOK_TILE
    n_first = xa.shape[0] // tm
    T = xa.shape[0] + xb.shape[0]
    HW = B_HEADS * HEAD_PAD
    row = lambda w: pl.BlockSpec((tm, w), lambda i: (i, 0))
    sds = jax.ShapeDtypeStruct
    mla_specs = [pl.BlockSpec((tm, 6 * LANE), lambda i: (i % (seq_len // tm), 0)),
                 _full((1, B_Q_RANK)), _full((1, B_KV_RANK)), _full((B_Q_RANK, HW)), _full((B_KV_RANK, HW)),
                 _full((B_KV_RANK, B_WIDTH)), _full((1, HW)), _full((1, HW)), _full((1, LANE)),
                 _full((2 * HEAD_PAD, HEAD_PAD)), _full((1, HW)), _full((LANE, HW))]
    return pl.pallas_call(
        functools.partial(_proj_kernel, n_first=n_first),
        grid=(T // tm,),
        in_specs=(_pair_specs(n_first, tm) + [_full((1, D_MODEL)), _full((D_MODEL, N_PAD)), _full((2, C_FDIM))]
                  + mla_specs),
        out_specs=[row(A_PAD), row(HW), row(HW), row(B_WIDTH), row(C_FDIM), row(2 * C_FDIM), row(2 * C_FDIM),
                   row(C_WIDTH), row(C_WIDTH)],
        out_shape=[sds((T, A_PAD), F32), sds((T, HW), BF16), sds((T, HW), BF16), sds((T, B_WIDTH), BF16),
                   sds((T, C_FDIM), BF16), sds((T, 2 * C_FDIM), F32), sds((T, 2 * C_FDIM), BF16),
                   sds((T, C_WIDTH), BF16), sds((T, C_WIDTH), F32)],
        compiler_params=_cparams(("parallel",)),
        name="proj",
    )(xa, xb, ln, w_pad, lb, tab, qan, kvan, wqb, wk, wv, qw, kw, knr, indq, invn, place)


def _gdn_kernel(qkv_ref, gate_ref, gates_ref, convw_ref, alog_ref, dtb_ref, anorm_ref, ind_ref, expb_ref, expg_ref,
                out_ref, q_s, k_s, v_s, of_s, ob_s, s_s, *, L):
    nc = L // CHUNK
    W = A_WIDTH
    row = lax.broadcasted_iota(jnp.int32, (CHUNK, CHUNK), 0)
    col = lax.broadcasted_iota(jnp.int32, (CHUNK, CHUNK), 1)
    lane = lax.broadcasted_iota(jnp.int32, (CHUNK, LANE), 1)
    row2 = lax.broadcasted_iota(jnp.int32, (CHUNK, 2 * CHUNK), 0)
    col2 = lax.broadcasted_iota(jnp.int32, (CHUNK, 2 * CHUNK), 1) % CHUNK
    ind = ind_ref[...]

    def conv_chunk(c, carry):
        t0 = pl.multiple_of(c * CHUNK, CHUNK)
        tp = pl.multiple_of(jnp.maximum(t0 - 8, 0), 8)
        tn = pl.multiple_of(jnp.minimum(t0 + CHUNK, L - 8), 8)
        for g in range(3):
            cs = slice(g * W, (g + 1) * W)
            main = qkv_ref[pl.ds(t0, CHUNK), cs]
            prev = jnp.where(c > 0, qkv_ref[pl.ds(tp, 8), cs], 0.0)
            nxt = jnp.where(c < nc - 1, qkv_ref[pl.ds(tn, 8), cs], 0.0)
            win = jnp.concatenate([prev, main, nxt], axis=0)
            acc = win * convw_ref[2:3, cs]
            for j in (0, 1, 3, 4):
                acc = acc + pltpu.roll(win, (2 - j) % (CHUNK + 16), 0) * convw_ref[j:j + 1, cs]
            y = _silu(acc[8:8 + CHUNK])
            if g < 2:
                y = y * lax.rsqrt(_dot_sel(y * y, ind) + EPS)
            if g == 0:
                q_s[pl.ds(t0, CHUNK), :] = (y * (A_HEAD_DIM ** -0.5)).astype(BF16)
            elif g == 1:
                k_s[pl.ds(t0, CHUNK), :] = y.astype(BF16)
            else:
                v_s[pl.ds(t0, CHUNK), :] = y
        return carry

    lax.fori_loop(0, nc, conv_chunk, 0, unroll=4)

    coef = -jnp.exp(alog_ref[...])
    dtb = dtb_ref[...]
    g_lane = (lane >= 2 * A_HEADS) & (lane < 4 * A_HEADS)

    def wide(c, d):
        t0 = pl.multiple_of(c * CHUNK, CHUNK)
        qc = q_s[pl.ds(t0, CHUNK), :].astype(F32)
        kc = k_s[pl.ds(t0, CHUNK), :].astype(F32)
        vc = v_s[pl.ds(t0, CHUNK), :]
        gt = gates_ref[pl.ds(t0, CHUNK), :]
        beta = _sigmoid(gt)
        gl = jnp.where(g_lane, coef * _softplus(gt + dtb), 0.0)
        tri2 = ((col2 <= row2) if d == 0 else (col2 >= row2)).astype(BF16)
        gcum = _sel2_dot(tri2, gl)
        total = gcum[CHUNK - 1:CHUNK] if d == 0 else gcum[0:1]
        e_in = jnp.exp(gcum)
        e_out = jnp.exp(total - gcum)
        g_last = jnp.broadcast_to(jnp.exp(total), (8, LANE))
        expb = expb_ref[d]
        expg = expg_ref[d]
        beta_w = jnp.dot(beta.astype(BF16), expb, preferred_element_type=F32)
        ein_w = jnp.dot(e_in.astype(BF16), expg, preferred_element_type=F32)
        eout_w = jnp.dot(e_out.astype(BF16), expg, preferred_element_type=F32)
        kb = kc * beta_w
        return dict(
            t0=t0, q=qc, k=kc, kb=kb, vb=vc * beta_w, kbe=kb * ein_w, qe=qc * ein_w, kd=kc * eout_w,
            glast=_dot_sel(g_last, expg),
            gcum=gcum, gcum_t=gcum.T,
            incl=(col <= row) if d == 0 else (col >= row),
            strict=(col < row) if d == 0 else (col > row))

    unroll = 2

    def scan_group(i, carry):
        wd = [[wide(unroll * i + u, 0), wide(nc - 1 - (unroll * i + u), 1)] for u in range(unroll)]
        dh = [(d, h) for d in (0, 1) for h in range(A_HEADS)]
        ks = [(u, d, h) for u in range(unroll) for d, h in dh]
        hs = lambda h: slice(h * A_HEAD_DIM, (h + 1) * A_HEAD_DIM)
        dec, n, qk, x = {}, {}, {}, {}
        for u, d, h in ks:
            w = wd[u][d]
            lg = 2 * A_HEADS + A_HEADS * d + h
            dec[u, d, h] = jnp.exp(jnp.minimum(w['gcum'][:, lg:lg + 1] - w['gcum_t'][lg:lg + 1, :], 0.0))
        for u, d, h in ks:
            w = wd[u][d]
            n[u, d, h] = jnp.where(w['strict'], _dot_nt(w['kb'][:, hs(h)], w['k'][:, hs(h)]) * dec[u, d, h], 0.0)
        for u, d, h in ks:
            w = wd[u][d]
            qk[u, d, h] = jnp.where(w['incl'], _dot_nt(w['q'][:, hs(h)], w['k'][:, hs(h)]) * dec[u, d, h], 0.0)
            x[u, d, h] = jnp.concatenate([w['vb'][:, hs(h)], w['kbe'][:, hs(h)]], axis=1)
        xw = 2 * A_HEAD_DIM
        z = {k: _dot(n[k], jnp.concatenate([x[k], n[k]], axis=1)) for k in ks}
        x = {k: x[k] - z[k][:, :xw] for k in ks}
        for _ in range(4):
            p = {k: z[k][:, xw:] for k in ks}
            z = {k: _dot(p[k], jnp.concatenate([x[k], p[k]], axis=1)) for k in ks}
            x = {k: x[k] + z[k][:, :xw] for k in ks}
        x = {k: x[k] + _dot(z[k][:, xw:], x[k]) for k in ks}
        s = {(d, h): s_s[d * A_HEADS + h] for d, h in dh}
        for u in range(unroll):
            v_new = {(d, h): x[u, d, h][:, :A_HEAD_DIM] - _dot(x[u, d, h][:, A_HEAD_DIM:], s[d, h]) for d, h in dh}
            o = {(d, h): _dot(wd[u][d]['qe'][:, hs(h)], s[d, h]) + _dot(qk[u, d, h], v_new[d, h]) for d, h in dh}
            s = {(d, h): (s[d, h] * wd[u][d]['glast'][0:1, hs(h)] + _dot_tn(wd[u][d]['kd'][:, hs(h)], v_new[d, h]))
                 for d, h in dh}
            of_s[pl.ds(wd[u][0]['t0'], CHUNK), :] = jnp.concatenate([o[0, h] for h in range(A_HEADS)], axis=1)
            ob_s[pl.ds(wd[u][1]['t0'], CHUNK), :] = jnp.concatenate([o[1, h] for h in range(A_HEADS)], axis=1)
        for d, h in dh:
            s_s[d * A_HEADS + h] = s[d, h]
        return carry

    s_s[...] = jnp.zeros_like(s_s)
    lax.fori_loop(0, nc // unroll, scan_group, 0)

    rows = 256

    def finish(i, carry):
        t0 = pl.multiple_of(i * rows, rows)
        o = of_s[pl.ds(t0, rows), :] + ob_s[pl.ds(t0, rows), :]
        ms = _dot_sel(o * o, ind) * (1.0 / A_HEAD_DIM)
        o = o * lax.rsqrt(ms + EPS) * anorm_ref[...]
        out_ref[pl.ds(t0, rows), :] = o * _silu(gate_ref[pl.ds(t0, rows), :])
        return carry

    lax.fori_loop(0, L // rows, finish, 0, unroll=4)


def _gdn(a_in, convw, alog, dtb, anorm, ind, expb, expg):
    B, L, _ = a_in.shape
    W = A_WIDTH
    kern = functools.partial(_gdn_kernel, L=L)
    one = None
    return pl.pallas_call(
        kern,
        grid=(B,),
        in_specs=[pl.BlockSpec((None, L, 3 * W), lambda b: (b, 0, 0), pipeline_mode=one),
                  pl.BlockSpec((None, L, W), lambda b: (b, 0, 3), pipeline_mode=one),
                  pl.BlockSpec((None, L, LANE), lambda b: (b, 0, 4 * W // LANE), pipeline_mode=one),
                  _full((8, 3 * W)), _full((1, LANE)), _full((1, LANE)), _full((1, W)), _full((W, W)),
                  _full((2, LANE, W)), _full((2, LANE, W))],
        out_specs=pl.BlockSpec((None, L, W), lambda b: (b, 0, 0)),
        out_shape=jax.ShapeDtypeStruct((B, L, W), F32),
        scratch_shapes=[pltpu.VMEM((L, W), BF16), pltpu.VMEM((L, W), BF16), pltpu.VMEM((L, W), F32),
                        pltpu.VMEM((L, W), F32), pltpu.VMEM((L, W), F32),
                        pltpu.VMEM((2 * A_HEADS, A_HEAD_DIM, A_HEAD_DIM), F32)],
        compiler_params=_cparams(("parallel",)),
        name="gdn",
    )(a_in, a_in, a_in, convw, alog, dtb, anorm, ind, expb, expg)


def _mla_prep(b, tab_ref, qan_ref, kvan_ref, wqb_ref, wk_ref, wv_ref, qw_ref, kw_ref, knr_ref,
              indq_ref, invn_ref, place_ref, q_out, k_out, v_out):
    tab = tab_ref[...]
    cq = _rms(b[:, :B_Q_RANK], qan_ref[...])
    ckv = _rms(b[:, B_Q_RANK:B_Q_RANK + B_KV_RANK], kvan_ref[...])
    kr = b[:, B_Q_RANK + B_KV_RANK:]
    ind2 = indq_ref[...]
    invn = invn_ref[...]

    def seg_sums(x2):
        hi, lo = _split(x2)
        hp = lambda t, h: t[:, h * HEAD_PAD:(h + 1) * HEAD_PAD]
        return jnp.concatenate(
            [jnp.dot(jnp.concatenate([hp(hi, h), hp(lo, h)], axis=1), ind2, preferred_element_type=F32)
             for h in range(B_HEADS)], axis=1)

    q = _dot(cq, wqb_ref[...])
    qn = q * lax.rsqrt(seg_sums(q * q) * invn + EPS) * qw_ref[...]
    scale = (B_NOPE + B_ROPE) ** -0.5
    pieces = []
    for h in range(B_HEADS):
        x = qn[:, h * HEAD_PAD:(h + 1) * HEAD_PAD]
        half = B_ROPE // 2
        r = (x * tab[:, 0:LANE] + pltpu.roll(x, LANE - half, 1) * tab[:, LANE:2 * LANE]
             + pltpu.roll(x, half, 1) * tab[:, 2 * LANE:3 * LANE])
        pieces.append(r * scale)
    q_out[...] = jnp.concatenate(pieces, axis=1).astype(BF16)

    kk = _dot(ckv, wk_ref[...])
    kn = kk * lax.rsqrt(seg_sums(kk * kk) * invn + EPS) * kw_ref[...]
    krn = kr * lax.rsqrt(jnp.sum(kr * kr, axis=-1, keepdims=True) * (1.0 / B_ROPE) + EPS) * knr_ref[...]
    half = B_ROPE // 2
    krr = (krn * tab[:, 3 * LANE:4 * LANE] + pltpu.roll(krn, LANE - half, 1) * tab[:, 4 * LANE:5 * LANE]
           + pltpu.roll(krn, half, 1) * tab[:, 5 * LANE:6 * LANE])
    k_out[...] = (kn + jnp.dot(krr.astype(BF16), place_ref[...], preferred_element_type=F32)).astype(BF16)
    v_out[...] = _dot(ckv, wv_ref[...]).astype(BF16)


def _attn_kernel(q_ref, k_ref, v_ref, o_ref, v_aug):
    @pl.when(pl.program_id(2) == 0)
    def _():
        lane = lax.broadcasted_iota(jnp.int32, (v_aug.shape[0], LANE), 1)
        v_aug[:, 0:2 * B_V] = v_ref[...]
        v_aug[:, 2 * B_V:] = jnp.where(lane == 0, 1.0, 0.0).astype(BF16)

    scores = []
    for hh in range(2):
        q = q_ref[:, hh * HEAD_PAD:(hh + 1) * HEAD_PAD]
        k = k_ref[:, hh * HEAD_PAD:(hh + 1) * HEAD_PAD]
        scores.append(lax.dot_general(q, k, (((1,), (1,)), ((), ())), preferred_element_type=F32))
    outs = []
    for s in scores:
        p = jnp.exp((s - jnp.max(s, axis=-1, keepdims=True)).astype(BF16))
        pv = jnp.dot(p, v_aug[...], preferred_element_type=F32)
        outs.append(pv[:, 0:2 * B_V] / pv[:, 2 * B_V:2 * B_V + 1])
    lane = lax.broadcasted_iota(jnp.int32, outs[0].shape, 1)
    o_ref[...] = jnp.where(lane < B_V, outs[0], outs[1])


def _attn(q, k, v):
    B, L, _ = q.shape
    tq = min(1024, L)
    return pl.pallas_call(
        _attn_kernel,
        grid=(B, B_HEADS // 2, L // tq),
        in_specs=[pl.BlockSpec((None, tq, 2 * HEAD_PAD), lambda b, p, j: (b, j, p)),
                  pl.BlockSpec((None, L, 2 * HEAD_PAD), lambda b, p, j: (b, 0, p)),
                  pl.BlockSpec((None, L, 2 * B_V), lambda b, p, j: (b, 0, p))],
        out_specs=pl.BlockSpec((None, tq, 2 * B_V), lambda b, p, j: (b, j, p)),
        out_shape=jax.ShapeDtypeStruct((B, L, B_WIDTH), F32),
        scratch_shapes=[pltpu.VMEM((L, 2 * B_V + LANE), BF16)],
        compiler_params=_cparams(("parallel", "parallel", "arbitrary")),
        name="mla_attn",
    )(q, k, v)


def _hgrn_tables():
    C = CHUNK
    m_all = np.zeros((8 * C, C), np.float32)
    masks = np.zeros((7, C, C), np.float32)
    for lvl in range(6):
        m = 1 << lvl
        for t in range(C):
            bd = (t // (2 * m)) * 2 * m + m - 1
            if t % (2 * m) >= m:
                m_all[lvl * C + t, bd + 1:t + 1] = 1.0
            else:
                m_all[lvl * C + t, t + 1:bd + 1] = 1.0
            for s in range(C):
                if t // (2 * m) == s // (2 * m) and t % (2 * m) >= m and s % (2 * m) < m:
                    masks[lvl, t, s] = 1.0
    for t in range(C):
        m_all[6 * C + t, :t + 1] = 1.0
        m_all[7 * C + t, t + 1:] = 1.0
    masks[6] = np.eye(C, dtype=np.float32)
    m_f = m_all.reshape(8, C, C)
    m_b = m_f[:, ::-1, ::-1]
    mk_b = masks[:, ::-1, ::-1]
    m2 = np.stack([m_f.reshape(8 * C, C), m_b.reshape(8 * C, C)])
    return (np.concatenate([m2, m2], axis=2), np.stack([masks, mk_b]))


def _hgrn_kernel(q_ref, lff_ref, lfb_ref, kf_ref, kb_ref, i_ref, gate_ref, cnorm_ref, mall_ref, masks_ref, ind_ref,
                 out_ref, of_s, ob_s, s_s, *, L):
    nc = L // CHUNK
    C = CHUNK

    def wide(c, d):
        t0 = pl.multiple_of(c * C, C)
        logf = (lff_ref if d == 0 else lfb_ref)[pl.ds(t0, C), :]
        e_all = jnp.exp(_sel2_dot(mall_ref[d], logf))
        return dict(t0=t0, q=q_ref[pl.ds(t0, C), :], k=(kf_ref if d == 0 else kb_ref)[pl.ds(t0, C), :],
                    e=e_all, v=i_ref[pl.ds(t0, C), :])

    unroll = 2

    def scan_group(i, carry):
        wd = [[wide(unroll * i + u, 0), wide(nc - 1 - (unroll * i + u), 1)] for u in range(unroll)]
        dh = [(d, h) for d in (0, 1) for h in range(C_HEADS)]
        ks = [(u, d, h) for u in range(unroll) for d, h in dh]
        ls = lambda h: slice(h * C_EXPAND, (h + 1) * C_EXPAND)
        vs = lambda h: slice(h * C_HEAD_DIM, (h + 1) * C_HEAD_DIM)
        qb = {(u, d, h): wd[u][d]['q'][:, ls(h)] for u, d, h in ks}
        kb = {(u, d, h): wd[u][d]['k'][:, ls(h)] for u, d, h in ks}
        vh = {(u, d, h): wd[u][d]['v'][:, vs(h)] for u, d, h in ks}
        lvl_e = lambda u, d, h, j: wd[u][d]['e'][j * C:(j + 1) * C, ls(h)]
        attn = {(u, d, h): _dot_nt(qb[u, d, h], kb[u, d, h]) * masks_ref[d, 6] for u, d, h in ks}
        for lvl in range(6):
            for u, d, h in ks:
                e = lvl_e(u, d, h, lvl).astype(BF16)
                attn[u, d, h] = attn[u, d, h] + _dot_nt(qb[u, d, h] * e, kb[u, d, h] * e) * masks_ref[d, lvl]
        intra = {k: _dot(attn[k], vh[k]) for k in ks}
        qin = {(u, d, h): qb[u, d, h] * lvl_e(u, d, h, 6).astype(BF16) for u, d, h in ks}
        kout = {(u, d, h): _dot_tn(vh[u, d, h], kb[u, d, h] * lvl_e(u, d, h, 7).astype(BF16)) for u, d, h in ks}
        st = {(d, h): s_s[d * C_HEADS + h] for d, h in dh}
        for u in range(unroll):
            o = {(d, h): intra[u, d, h] + _dot_nt(qin[u, d, h], st[d, h]) for d, h in dh}
            for d, h in dh:
                e_in = lvl_e(u, d, h, 6)
                st[d, h] = st[d, h] * (e_in[C - 1:C] if d == 0 else e_in[0:1]) + kout[u, d, h]
            of_s[pl.ds(wd[u][0]['t0'], C), :] = jnp.concatenate([o[0, h] for h in range(C_HEADS)], axis=1)
            ob_s[pl.ds(wd[u][1]['t0'], C), :] = jnp.concatenate([o[1, h] for h in range(C_HEADS)], axis=1)
        for d, h in dh:
            s_s[d * C_HEADS + h] = st[d, h]
        return carry

    s_s[...] = jnp.zeros_like(s_s)
    lax.fori_loop(0, nc // unroll, scan_group, 0)

    rows = 256
    ind = ind_ref[...]

    def finish(i, carry):
        t0 = pl.multiple_of(i * rows, rows)
        o = of_s[pl.ds(t0, rows), :] + ob_s[pl.ds(t0, rows), :]
        ms = _dot_sel(o * o, ind) * (1.0 / C_HEAD_DIM)
        o = o * lax.rsqrt(ms + EPS) * cnorm_ref[...]
        out_ref[pl.ds(t0, rows), :] = o * _silu(gate_ref[pl.ds(t0, rows), :])
        return carry

    lax.fori_loop(0, L // rows, finish, 0, unroll=4)


def _hgrn(q, logf, kk, v, gate, cnorm, mall, masks, ind):
    B, L, _ = q.shape
    kern = functools.partial(_hgrn_kernel, L=L)
    one = None
    F, W = C_FDIM, C_WIDTH
    seq = lambda w, j: pl.BlockSpec((None, L, w), lambda b: (b, 0, j), pipeline_mode=one)
    return pl.pallas_call(
        kern,
        grid=(B,),
        in_specs=[seq(F, 0), seq(F, 0), seq(F, 1), seq(F, 0), seq(F, 1), seq(W, 0), seq(W, 0),
                  _full((1, W)), _full((2, 8 * CHUNK, 2 * CHUNK)), _full((2, 7, CHUNK, CHUNK)), _full((W, W))],
        out_specs=pl.BlockSpec((None, L, W), lambda b: (b, 0, 0)),
        out_shape=jax.ShapeDtypeStruct((B, L, W), F32),
        scratch_shapes=[pltpu.VMEM((L, W), F32), pltpu.VMEM((L, W), F32),
                        pltpu.VMEM((2 * C_HEADS, C_HEAD_DIM, C_EXPAND), F32)],
        compiler_params=_cparams(("parallel",)),
        name="hgrn2",
    )(q, logf, logf, kk, kk, v, gate, cnorm, mall, masks, ind)


def _outproj_kernel(xa_ref, xb_ref, oa_ref, ob_ref, oc_ref, wa_ref, wb_ref, wc_ref, ln_ref, wr_ref, br_ref, tri_ref,
                    x1_ref, h_ref, route_ref, cnt_ref, cnt_s, *, n_first):
    x1 = (_pair_load(xa_ref, xb_ref, n_first) + _dot(oa_ref[...], wa_ref[...]) + _dot(ob_ref[...], wb_ref[...])
          + _dot(oc_ref[...], wc_ref[...]))
    x1_ref[...] = x1
    h = _rms(x1, ln_ref[...])
    h_ref[...] = h
    hh, hl = _split(h)
    wr = wr_ref[...]
    wh, wl = _split(wr)
    hw = jnp.dot(hh, jnp.concatenate([wh, wl], axis=1), preferred_element_type=F32)
    lg = hw[:, :LANE] + hw[:, LANE:] + jnp.dot(hl, wh, preferred_element_type=F32) + br_ref[...]
    lane = lax.broadcasted_iota(jnp.int32, lg.shape, 1).astype(F32)
    neg = -1e30
    far = 1e9
    gmask = lane < N_GROUPS
    gl = jnp.where(gmask, lg, neg)
    gmax = jnp.max(gl, axis=-1, keepdims=True)
    g_idx = jnp.min(jnp.where(gl == gmax, lane, far), axis=-1, keepdims=True)
    g_w = 1.0 / jnp.sum(jnp.where(gmask, jnp.exp(gl - gmax), 0.0), axis=-1, keepdims=True)
    e_lane = lane - N_GROUPS
    emask = (e_lane >= g_idx * EXPERTS_PER_GROUP) & (e_lane < (g_idx + 1.0) * EXPERTS_PER_GROUP)
    el = jnp.where(emask, lg, neg)
    m1 = jnp.max(el, axis=-1, keepdims=True)
    i1 = jnp.min(jnp.where(el == m1, lane, far), axis=-1, keepdims=True)
    el2 = jnp.where(lane == i1, neg, el)
    m2 = jnp.max(el2, axis=-1, keepdims=True)
    i2 = jnp.min(jnp.where((el2 == m2) & emask & (lane != i1), lane, far), axis=-1, keepdims=True)
    r = jnp.exp(m2 - m1)
    w1 = g_w / (1.0 + r)
    w2 = g_w * r / (1.0 + r)
    @pl.when(pl.program_id(0) == 0)
    def _():
        cnt_s[...] = jnp.zeros_like(cnt_s)

    oh1 = jnp.where(lane == i1, 1.0, 0.0)
    oh2 = jnp.where(lane == i2, 1.0, 0.0)
    tri = tri_ref[...]
    run = cnt_s[0:1, :]
    tot1 = jnp.sum(oh1, axis=0, keepdims=True)
    c1 = run + jnp.dot(tri, oh1.astype(BF16), preferred_element_type=F32)
    c2 = run + tot1 + jnp.dot(tri, oh2.astype(BF16), preferred_element_type=F32)
    r1 = jnp.sum(oh1 * c1, axis=-1, keepdims=True)
    r2 = jnp.sum(oh2 * c2, axis=-1, keepdims=True)
    cnt = jnp.broadcast_to(run + tot1 + jnp.sum(oh2, axis=0, keepdims=True), cnt_s.shape)
    cnt_s[...] = cnt
    cnt_ref[...] = cnt
    vals = (i1 - N_GROUPS, i2 - N_GROUPS, w1, w2, r1, r2)
    route = jnp.zeros_like(lg)
    for j, v in enumerate(vals):
        route = jnp.where(lane == float(j), v, route)
    route_ref[...] = route


def _outproj(xa, xb, oa, ob, oc, wa, wb, wc, ln, wr, br, tri):
    tm = TOK_TILE
    n_first = xa.shape[0] // tm
    T = xa.shape[0] + xb.shape[0]
    row = lambda w: pl.BlockSpec((tm, w), lambda i: (i, 0))
    return pl.pallas_call(
        functools.partial(_outproj_kernel, n_first=n_first),
        grid=(T // tm,),
        in_specs=_pair_specs(n_first, tm) + [row(A_WIDTH), row(B_WIDTH), row(C_WIDTH),
                  _full((A_WIDTH, D_MODEL)), _full((B_WIDTH, D_MODEL)), _full((C_WIDTH, D_MODEL)),
                  _full((1, D_MODEL)), _full((D_MODEL, LANE)), _full((1, LANE)), _full((tm, tm))],
        out_specs=[row(D_MODEL), row(D_MODEL), row(LANE), _full((8, LANE))],
        out_shape=[jax.ShapeDtypeStruct((T, D_MODEL), F32), jax.ShapeDtypeStruct((T, D_MODEL), F32),
                   jax.ShapeDtypeStruct((T, LANE), F32), jax.ShapeDtypeStruct((8, LANE), F32)],
        scratch_shapes=[pltpu.VMEM((8, LANE), F32)],
        compiler_params=_cparams(("arbitrary",)),
        name="outproj_router",
    )(xa, xb, oa, ob, oc, wa, wb, wc, ln, wr, br, tri)


def _gather_rows(src_hbm, buf, sem, cur_ids, nxt_ids, n_rows):
    i = pl.program_id(0)
    slot = i % 2

    def row_copy(ids, r, s):
        return pltpu.make_async_copy(src_hbm.at[pl.ds(ids[0, 0, r], 1)], buf.at[s, pl.ds(r, 1)], sem.at[s])

    def start_all(ids, s):
        def body(r, carry):
            row_copy(ids, r, s).start()
            return carry
        lax.fori_loop(0, n_rows, body, 0, unroll=8)

    def wait_all(s):
        pltpu.make_async_copy(src_hbm.at[pl.ds(0, n_rows)], buf.at[s], sem.at[s]).wait()

    @pl.when(i == 0)
    def _():
        start_all(cur_ids, 0)

    start_all(nxt_ids, 1 - slot)
    wait_all(slot)

    @pl.when(i == pl.num_programs(0) - 1)
    def _():
        wait_all(1 - slot)

    return slot


def _scatter_kernel(pends_ref, pos_ref, h_ref, xs_hbm, zero_s, sem):
    tm = TOK_TILE

    @pl.when(pl.program_id(0) == 0)
    def _():
        zero_s[...] = jnp.zeros_like(zero_s)

        def fill(e):
            start = pl.multiple_of(jnp.maximum(pends_ref[e] - MOE_ROWS, 0), MOE_ROWS)
            return pltpu.make_async_copy(zero_s, xs_hbm.at[pl.ds(start, MOE_ROWS)], sem)

        for e in range(N_EXPERTS):
            fill(e).start()
            fill(e).wait()

        def fill_tail(b, carry):
            cp = pltpu.make_async_copy(zero_s, xs_hbm.at[pl.ds(pl.multiple_of(b * MOE_ROWS, MOE_ROWS), MOE_ROWS)], sem)
            cp.start()
            cp.wait()
            return carry

        lax.fori_loop(pends_ref[N_EXPERTS - 1] // MOE_ROWS, xs_hbm.shape[0] // MOE_ROWS, fill_tail, 0)

    def row_copy(r, k):
        return pltpu.make_async_copy(h_ref.at[pl.ds(r, 1)], xs_hbm.at[pl.ds(pos_ref[0, 0, k * tm + r], 1)], sem)

    def issue(r, carry):
        for k in range(TOP_K):
            row_copy(r, k).start()
        return carry

    lax.fori_loop(0, tm, issue, 0, unroll=8)
    for k in range(TOP_K):
        pltpu.make_async_copy(h_ref, xs_hbm.at[pl.ds(0, tm)], sem).wait()


def _scatter(pends, pos, h):
    T = h.shape[0]
    tm = TOK_TILE
    grid_spec = pltpu.PrefetchScalarGridSpec(
        num_scalar_prefetch=1,
        grid=(T // tm,),
        in_specs=[pl.BlockSpec((1, 1, TOP_K * tm), lambda i, pe: (i, 0, 0), memory_space=pltpu.SMEM),
                  pl.BlockSpec((tm, D_MODEL), lambda i, pe: (i, 0))],
        out_specs=pl.BlockSpec(memory_space=pl.ANY),
        scratch_shapes=[pltpu.VMEM((MOE_ROWS, D_MODEL), F32), pltpu.SemaphoreType.DMA(())],
    )
    return pl.pallas_call(
        _scatter_kernel,
        grid_spec=grid_spec,
        out_shape=jax.ShapeDtypeStruct((T * TOP_K + N_EXPERTS * MOE_ROWS, D_MODEL), F32),
        compiler_params=_cparams(("arbitrary",)),
        name="moe_scatter",
    )(pends, pos, h)


def _moe_kernel(blk_e_ref, nact_ref, x_ref, w1_ref, w3_ref, w2_ref, y_ref, w1_s, w3_s, w2_s):
    i = pl.program_id(0)
    active = i < nact_ref[0]

    @pl.when((i == 0) | (blk_e_ref[i] != blk_e_ref[jnp.maximum(i - 1, 0)]))
    def _():
        w1_s[...] = w1_ref[...].astype(BF16)
        w3_s[...] = w3_ref[...].astype(BF16)
        w2_s[...] = w2_ref[...].astype(BF16)

    @pl.when(active)
    def _():
        x = x_ref[...].astype(BF16)
        a = jnp.dot(x, w1_s[...], preferred_element_type=F32)
        g = jnp.dot(x, w3_s[...], preferred_element_type=F32)
        y_ref[...] = jnp.dot((_silu(a) * g).astype(BF16), w2_s[...], preferred_element_type=F32)

    @pl.when(jnp.logical_not(active))
    def _():
        y_ref[...] = jnp.zeros_like(y_ref)


def _moe(blk_e, nact, xs, w1, w3, w2, layer):
    P = xs.shape[0]
    nblk = P // MOE_ROWS
    wspec = lambda a, b: pl.BlockSpec((None, None, a, b), lambda i, be, na: (layer, be[i], 0, 0))
    grid_spec = pltpu.PrefetchScalarGridSpec(
        num_scalar_prefetch=2,
        grid=(nblk,),
        in_specs=[pl.BlockSpec((MOE_ROWS, D_MODEL), lambda i, be, na: (jnp.minimum(i, na[0] - 1), 0)),
                  wspec(D_MODEL, D_EXPERT), wspec(D_MODEL, D_EXPERT), wspec(D_EXPERT, D_MODEL)],
        out_specs=pl.BlockSpec((MOE_ROWS, D_MODEL), lambda i, be, na: (i, 0)),
        scratch_shapes=[pltpu.VMEM((D_MODEL, D_EXPERT), BF16), pltpu.VMEM((D_MODEL, D_EXPERT), BF16),
                        pltpu.VMEM((D_EXPERT, D_MODEL), BF16)],
    )
    return pl.pallas_call(
        _moe_kernel,
        grid_spec=grid_spec,
        out_shape=jax.ShapeDtypeStruct((P, D_MODEL), F32),
        compiler_params=_cparams(("arbitrary",)),
        name="moe_ffn",
    )(blk_e, nact, xs, w1, w3, w2)


def _combine_kernel(pos_ref, nxt_ref, x_ref, route_ref, y_hbm, outa_ref, outb_ref, yb, sem, *, n_first):
    tm = x_ref.shape[0]
    slot = _gather_rows(y_hbm, yb, sem, pos_ref, nxt_ref, TOP_K * tm)
    route = route_ref[...]
    val = x_ref[...] + yb[slot, 0:tm, :] * route[:, 2:3] + yb[slot, tm:2 * tm, :] * route[:, 3:4]
    first = pl.program_id(0) < n_first

    @pl.when(first)
    def _():
        outa_ref[...] = val

    @pl.when(jnp.logical_not(first))
    def _():
        outb_ref[...] = val


def _combine(pos, x1, route, y, n_first):
    T = x1.shape[0]
    tm = TOK_TILE
    nt = T // tm
    return pl.pallas_call(
        functools.partial(_combine_kernel, n_first=n_first),
        grid=(nt,),
        in_specs=[pl.BlockSpec((1, 1, TOP_K * tm), lambda i: (i, 0, 0), memory_space=pltpu.SMEM),
                  pl.BlockSpec((1, 1, TOP_K * tm), lambda i: (jnp.minimum(i + 1, nt - 1), 0, 0),
                               memory_space=pltpu.SMEM),
                  pl.BlockSpec((tm, D_MODEL), lambda i: (i, 0)),
                  pl.BlockSpec((tm, LANE), lambda i: (i, 0)),
                  pl.BlockSpec(memory_space=pl.ANY)],
        out_specs=_pair_specs(n_first, tm),
        out_shape=[jax.ShapeDtypeStruct((n_first * tm, D_MODEL), F32),
                   jax.ShapeDtypeStruct((T - n_first * tm, D_MODEL), F32)],
        scratch_shapes=[pltpu.VMEM((2, TOP_K * tm, D_MODEL), F32), pltpu.SemaphoreType.DMA((2,))],
        compiler_params=_cparams(("arbitrary",)),
        name="moe_combine",
    )(pos, pos, x1, route, y)


def _dispatch(route, cnt):
    T = route.shape[0]
    nblk = (T * TOP_K) // MOE_ROWS + N_EXPERTS
    counts = cnt[0, N_GROUPS:N_GROUPS + N_EXPERTS].astype(jnp.int32)
    padded = (counts + MOE_ROWS - 1) // MOE_ROWS * MOE_ROWS
    pends = jnp.cumsum(padded)
    pstarts = pends - padded
    e = route[:, 0:TOP_K].astype(jnp.int32)
    rank = route[:, 4:4 + TOP_K].astype(jnp.int32)
    sel = e[:, :, None] == jnp.arange(N_EXPERTS, dtype=jnp.int32)
    pos = rank + jnp.sum(jnp.where(sel, pstarts, 0), axis=-1)
    pos = jnp.transpose(pos.reshape(T // TOK_TILE, TOK_TILE, TOP_K), (0, 2, 1))
    pos = pos.reshape(T // TOK_TILE, 1, TOP_K * TOK_TILE)
    blk_start = jnp.arange(nblk, dtype=jnp.int32) * MOE_ROWS
    blk_e = jnp.minimum(jnp.sum(blk_start[:, None] >= pends[None, :], axis=-1), N_EXPERTS - 1).astype(jnp.int32)
    nact = (pends[-1:] // MOE_ROWS).astype(jnp.int32)
    return blk_e, nact, pos, pends.astype(jnp.int32)


def _block_ind(width, seg):
    i = np.arange(width)
    return (i[:, None] // seg == i[None, :] // seg).astype(np.float32)


def _rope_tables(L):
    half = B_ROPE // 2
    inv = ROPE_THETA ** (-jnp.arange(0, B_ROPE, 2, dtype=F32) / B_ROPE)
    ang = jnp.arange(L, dtype=F32)[:, None] * inv[None, :]
    cos, sin = jnp.cos(ang), jnp.sin(ang)
    z = lambda n: jnp.zeros((L, n), F32)
    one = jnp.ones((L, B_NOPE), F32)
    rest = LANE - B_NOPE - B_ROPE
    cq = jnp.concatenate([one, cos, cos, z(rest)], axis=1)
    s1q = jnp.concatenate([z(B_NOPE), -sin, z(half), z(rest)], axis=1)
    s2q = jnp.concatenate([z(B_NOPE), z(half), sin, z(rest)], axis=1)
    ck = jnp.concatenate([cos, cos, z(LANE - B_ROPE)], axis=1)
    s1k = jnp.concatenate([-sin, z(LANE - half)], axis=1)
    s2k = jnp.concatenate([z(half), sin, z(LANE - B_ROPE)], axis=1)
    return jnp.concatenate([cq, s1q, s2q, ck, s1k, s2k], axis=1)


def _layer_params(l, ln1, w_in, conv_w, a_log, dt_bias, a_norm, q_a_norm, w_qb, kv_a_norm, w_kvb, qn_nope, qn_rope,
                  kn_nope, kn_rope, lbs, c_norm, w_out, ln2, w_group, b_group, w_router, b_router, w1, w3, w2):
    p = {}
    w = w_in[l]
    wp = jnp.zeros((D_MODEL, N_PAD), F32)
    wp = wp.at[:, 0:A_IN].set(w[:, 0:A_IN])
    wp = wp.at[:, A_PAD:A_PAD + B_IN].set(w[:, A_IN:A_IN + B_IN])
    wp = wp.at[:, A_PAD + B_PAD:].set(w[:, A_IN + B_IN:])
    p['w_in'] = wp.astype(BF16)
    p['ln1'] = ln1[l].reshape(1, D_MODEL)
    p['convw'] = jnp.zeros((8, 3 * A_WIDTH), F32).at[0:CONV_K].set(conv_w[l])
    gpad = lambda v: jnp.zeros((1, LANE), F32).at[0, 2 * A_HEADS:4 * A_HEADS].set(v.reshape(-1))
    p['alog'] = gpad(a_log[l])
    p['dtb'] = gpad(dt_bias[l])
    p['anorm'] = jnp.tile(a_norm[l], A_HEADS).reshape(1, A_WIDTH)
    half = B_ROPE // 2
    wq = w_qb[l].reshape(B_Q_RANK, B_HEADS, B_NOPE + B_ROPE)
    wq = jnp.pad(wq, ((0, 0), (0, 0), (0, HEAD_PAD - B_NOPE - B_ROPE)))
    p['wqb'] = wq.reshape(B_Q_RANK, B_HEADS * HEAD_PAD).astype(BF16)
    wkv = w_kvb[l].reshape(B_KV_RANK, B_HEADS, B_NOPE + B_V)
    wk = jnp.pad(wkv[:, :, :B_NOPE], ((0, 0), (0, 0), (0, HEAD_PAD - B_NOPE)))
    p['wk'] = wk.reshape(B_KV_RANK, B_HEADS * HEAD_PAD).astype(BF16)
    p['wv'] = wkv[:, :, B_NOPE:].reshape(B_KV_RANK, B_WIDTH).astype(BF16)
    zpad = jnp.zeros((HEAD_PAD - B_NOPE - B_ROPE,), F32)
    p['qw'] = jnp.tile(jnp.concatenate([qn_nope[l], qn_rope[l], zpad]), B_HEADS).reshape(1, -1)
    p['kw'] = jnp.tile(jnp.concatenate([kn_nope[l], jnp.zeros((HEAD_PAD - B_NOPE,), F32)]), B_HEADS).reshape(1, -1)
    p['knr'] = jnp.concatenate([kn_rope[l], jnp.zeros((LANE - B_ROPE,), F32)]).reshape(1, LANE)
    p['qan'] = q_a_norm[l].reshape(1, -1)
    p['kvan'] = kv_a_norm[l].reshape(1, -1)
    p['lb'] = lbs[l]
    p['cnorm'] = jnp.tile(c_norm[l], C_HEADS).reshape(1, C_WIDTH)
    wo = w_out[l].astype(BF16)
    p['wa'], p['wb'], p['wc'] = wo[:A_WIDTH], wo[A_WIDTH:A_WIDTH + B_WIDTH], wo[A_WIDTH + B_WIDTH:]
    p['ln2'] = ln2[l].reshape(1, D_MODEL)
    wr = jnp.zeros((D_MODEL, LANE), F32).at[:, :N_GROUPS].set(w_group[l])
    p['wr'] = wr.at[:, N_GROUPS:N_GROUPS + N_EXPERTS].set(w_router[l])
    br = jnp.zeros((1, LANE), F32).at[0, :N_GROUPS].set(b_group[l])
    p['br'] = br.at[0, N_GROUPS:N_GROUPS + N_EXPERTS].set(b_router[l])
    p['w1'], p['w3'], p['w2'], p['layer'] = w1, w3, w2, l
    return p


def _static_tables():
    t = {}
    t['ind_a'] = jnp.asarray(_block_ind(A_WIDTH, A_HEAD_DIM), BF16)
    t['ind_c'] = jnp.asarray(_block_ind(C_WIDTH, C_HEAD_DIM), BF16)
    expb = np.zeros((2, LANE, A_WIDTH), np.float32)
    expg = np.zeros((2, LANE, A_WIDTH), np.float32)
    for d in range(2):
        for h in range(A_HEADS):
            expb[d, A_HEADS * d + h, h * A_HEAD_DIM:(h + 1) * A_HEAD_DIM] = 1.0
            expg[d, 2 * A_HEADS + A_HEADS * d + h, h * A_HEAD_DIM:(h + 1) * A_HEAD_DIM] = 1.0
    t['expb'] = jnp.asarray(expb, BF16)
    t['expg'] = jnp.asarray(expg, BF16)
    HW = B_HEADS * HEAD_PAD
    lane = np.arange(HW)
    seg = np.where(lane % HEAD_PAD < B_NOPE, 0, np.where(lane % HEAD_PAD < B_NOPE + B_ROPE, 1, 2))
    head = lane // HEAD_PAD
    seg1 = seg[:HEAD_PAD]
    ind1 = ((seg1[:, None] == seg1[None, :]) & (seg1[:, None] < 2)).astype(np.float32)
    t['indq'] = jnp.asarray(np.concatenate([ind1, ind1], axis=0), BF16)
    t['invn'] = jnp.asarray(np.where(seg == 0, 1.0 / B_NOPE, 1.0 / B_ROPE).astype(np.float32)).reshape(1, HW)
    place = np.zeros((LANE, HW), np.float32)
    for h in range(B_HEADS):
        for r in range(B_ROPE):
            place[r, h * HEAD_PAD + B_NOPE + r] = 1.0
    t['place'] = jnp.asarray(place, BF16)
    t['tri'] = jnp.asarray(np.tril(np.ones((TOK_TILE, TOK_TILE), np.float32), -1), BF16)
    mall, masks = _hgrn_tables()
    t['mall'] = jnp.asarray(mall, BF16)
    t['masks'] = jnp.asarray(masks, F32)
    return t


def _trunk(x_first, x_second, params, tabs, rope):
    L, D = x_first.shape[1:]
    B = x_first.shape[0] + x_second.shape[0]
    T = B * L
    xa, xb = x_first.reshape(-1, D), x_second.reshape(-1, D)
    n_first = xa.shape[0] // TOK_TILE
    for p in params:
        a_in, q, k, v, c_q, c_logf, c_k, c_v, c_gate = _proj(
            xa, xb, p['ln1'], p['w_in'], p['lb'], L, rope, p['qan'], p['kvan'], p['wqb'], p['wk'], p['wv'],
            p['qw'], p['kw'], p['knr'], tabs['indq'], tabs['invn'], tabs['place'])
        seq = lambda t: t.reshape(B, L, t.shape[-1])
        o_a = _gdn(a_in.reshape(B, L, A_PAD), p['convw'], p['alog'], p['dtb'], p['anorm'], tabs['ind_a'],
                   tabs['expb'], tabs['expg'])
        o_b = _attn(seq(q), seq(k), seq(v))
        o_c = _hgrn(seq(c_q), seq(c_logf), seq(c_k), seq(c_v), seq(c_gate), p['cnorm'], tabs['mall'],
                    tabs['masks'], tabs['ind_c'])
        x1, h, route, cnt = _outproj(xa, xb, o_a.reshape(T, A_WIDTH), o_b.reshape(T, B_WIDTH),
                                     o_c.reshape(T, C_WIDTH), p['wa'], p['wb'], p['wc'], p['ln2'], p['wr'], p['br'],
                                     tabs['tri'])
        blk_e, nact, pos, pends = _dispatch(route, cnt)
        xs = _scatter(pends, pos, h)
        y = _moe(blk_e, nact, xs, p['w1'], p['w3'], p['w2'], p['layer'])
        xa, xb = _combine(pos, x1, route, y, n_first)
    return xa.reshape(x_first.shape), xb.reshape(x_second.shape)


def kernel(x_prompt, x_sample, ln1, w_in, conv_w, a_log, dt_bias, a_norm, q_a_norm, w_qb, kv_a_norm, w_kvb, qn_nope,
           qn_rope, kn_nope, kn_rope, c_lower_bounds, c_norm, w_out, ln2, w_group, b_group, w_router, b_router,
           w1, w3, w2):
    lbs = jnp.cumsum(jax.nn.softmax(c_lower_bounds.astype(F32), axis=0), axis=0)
    lbs = lbs - lbs[0:1]
    params = [_layer_params(l, ln1, w_in, conv_w, a_log, dt_bias, a_norm, q_a_norm, w_qb, kv_a_norm, w_kvb, qn_nope,
                            qn_rope, kn_nope, kn_rope, lbs, c_norm, w_out, ln2, w_group, b_group, w_router, b_router,
                            w1, w3, w2) for l in range(DEPTH)]
    tabs = _static_tables()
    assert x_prompt.shape[1:] == x_sample.shape[1:]
    rope = _rope_tables(x_prompt.shape[1])
    assert (x_prompt.shape[0] * x_prompt.shape[1]) % TOK_TILE == 0
    return _trunk(x_prompt, x_sample, params, tabs, rope)
```
